```python
import math
import jax, jax.numpy as jnp
from jax import lax
import numpy as np

D_MODEL = 2048
BATCH = 2
SEQ = 4096
DEPTH = 2

GRID_W = 64
CTX_LEN = 256
HEAD_DIM = 128
N_HEAD_SLOTS = D_MODEL // HEAD_DIM
MLSTM_HEADS = N_HEAD_SLOTS // 4
MLSTM_DK = HEAD_DIM
MLSTM_DV = HEAD_DIM
GLA_HEADS = N_HEAD_SLOTS // 4
GLA_DK = HEAD_DIM // 2
GLA_DV = HEAD_DIM
GLA_RANK = 16
GLA_TAU = 16.0
DIFF_HEADS = N_HEAD_SLOTS // 2
DIFF_DQK = HEAD_DIM // 2
DIFF_DV = HEAD_DIM
CONV_W = 3
CHUNK = 64
Q_BLOCK = 128
ROPE_BASE = 10000.0
ROPE_AXIS_DIM = DIFF_DQK // 2
FFN_HIDDEN = ((8 * D_MODEL + 3 * 256 - 1) // (3 * 256)) * 256
NORM_EPS = 1e-6

M_QK = MLSTM_HEADS * MLSTM_DK
M_V = MLSTM_HEADS * MLSTM_DV
M_GATES = 2 * 2 * MLSTM_HEADS
G_QK = GLA_HEADS * GLA_DK
G_V = GLA_HEADS * GLA_DV
G_LR = 2 * GLA_RANK
D_QK = DIFF_HEADS * 2 * DIFF_DQK
D_V = DIFF_HEADS * DIFF_DV
MIX_SPLITS = (M_QK, M_QK, M_V, M_V, M_GATES, G_QK, G_QK, G_V, G_V, G_LR, D_QK, D_QK, D_V)
MIX_OFFSETS = [int(o) for o in np.cumsum(MIX_SPLITS)[:-1]]
IN_COLS = sum(MIX_SPLITS)
MIX_WIDTH = M_V + G_V + D_V

kernel_name = "hybrid_mlstm_gla_diffattn_dit_block"


def rmsnorm(x, w):
    xf = x.astype(jnp.float32)
    y = xf * lax.rsqrt(jnp.mean(xf * xf, axis=-1, keepdims=True) + NORM_EPS)
    return (y * w.astype(jnp.float32)).astype(x.dtype)


def split_heads(a, n_heads):
    b, n, _ = a.shape
    return a.reshape(b, n, n_heads, -1).transpose(0, 2, 1, 3)


def merge_heads(a):
    b, h, n, d = a.shape
    return a.transpose(0, 2, 1, 3).reshape(b, n, h * d)


def centred_conv(a, w, bias):
    half = CONV_W // 2
    n = a.shape[1]
    ap = jnp.pad(a, ((0, 0), (half, half), (0, 0)))
    return sum(ap[:, j:j + n] * w[j] for j in range(CONV_W)) + bias


def axial_rope_tables(n_tokens):
    rows = n_tokens // GRID_W
    row = jnp.repeat(jnp.arange(rows, dtype=jnp.float32), GRID_W)
    col = jnp.tile(jnp.arange(GRID_W, dtype=jnp.float32), rows)
    half = ROPE_AXIS_DIM // 2
    inv_freq = ROPE_BASE ** (-jnp.arange(half, dtype=jnp.float32) / half)
    ang_r = row[:, None] * inv_freq
    ang_c = col[:, None] * inv_freq
    ang = jnp.concatenate([ang_r, ang_r, ang_c, ang_c], axis=-1)
    return jnp.cos(ang), jnp.sin(ang)


def apply_rope(x, cos, sin):
    def rot_half(a):
        a1, a2 = jnp.split(a, 2, axis=-1)
        return jnp.concatenate([-a2, a1], axis=-1)
    x_row, x_col = jnp.split(x, 2, axis=-1)
    rotated = jnp.concatenate([rot_half(x_row), rot_half(x_col)], axis=-1)
    cs = cos[None, :, None, None, :]
    sn = sin[None, :, None, None, :]
    return (x * cs + rotated * sn).astype(x.dtype)


def to_chunks(a):
    b, h, t = a.shape[:3]
    return jnp.moveaxis(a.reshape(b, h, t // CHUNK, CHUNK, *a.shape[3:]), 2, 0)


def from_chunks(a):
    nc, b, h, l, d = a.shape
    return jnp.moveaxis(a, 0, 2).reshape(b, h, nc * l, d)


def mlstm_scan(q, k, v, log_i, log_f):
    f32 = jnp.float32
    q, k, v = q.astype(f32), k.astype(f32), v.astype(f32)
    b, h, _, dk = q.shape
    dv = v.shape[-1]
    tril = jnp.tril(jnp.ones((CHUNK, CHUNK), dtype=bool))

    def step(carry, inp):
        c_state, n_state, m_state = carry
        qc, kc, vc, ic, fc = inp
        cum_f = jnp.cumsum(fc, axis=-1)
        dmat = jnp.where(tril, cum_f[..., :, None] - cum_f[..., None, :] + ic[..., None, :], -jnp.inf)
        m_inter = cum_f + m_state[..., None]
        m_t = jnp.maximum(m_inter, jnp.max(dmat, axis=-1))
        w_inter = jnp.exp(m_inter - m_t)
        s = jnp.einsum('bhtd,bhsd->bhts', qc, kc) * jnp.exp(dmat - m_t[..., None])
        num = w_inter[..., None] * jnp.einsum('bhtd,bhde->bhte', qc, c_state) + jnp.einsum('bhts,bhse->bhte', s, vc)
        den = w_inter * jnp.einsum('bhtd,bhd->bht', qc, n_state) + jnp.sum(s, axis=-1)
        out = num / jnp.maximum(jnp.abs(den), jnp.exp(-m_t))[..., None]
        f_end = cum_f[..., -1]
        dec = f_end[..., None] - cum_f + ic
        m_new = jnp.maximum(f_end + m_state, jnp.max(dec, axis=-1))
        a_prev = jnp.exp(f_end + m_state - m_new)
        ws = jnp.exp(dec - m_new[..., None])
        c_new = a_prev[..., None, None] * c_state + jnp.einsum('bhs,bhsd,bhse->bhde', ws, kc, vc)
        n_new = a_prev[..., None] * n_state + jnp.einsum('bhs,bhsd->bhd', ws, kc)
        return (c_new, n_new, m_new), out

    init = (jnp.zeros((b, h, dk, dv), f32), jnp.zeros((b, h, dk), f32), jnp.zeros((b, h), f32))
    xs = (to_chunks(q), to_chunks(k), to_chunks(v), to_chunks(log_i.astype(f32)), to_chunks(log_f.astype(f32)))
    _, out = lax.scan(step, init, xs)
    return from_chunks(out)


def gla_scan(q, k, v, log_a):
    f32 = jnp.float32
    q, k, v, log_a = q.astype(f32), k.astype(f32), v.astype(f32), log_a.astype(f32)
    b, h, _, dk = q.shape
    dv = v.shape[-1]
    tril = jnp.tril(jnp.ones((CHUNK, CHUNK), dtype=bool))[:, :, None]

    def step(state, inp):
        qc, kc, vc, gc = inp
        g = jnp.cumsum(gc, axis=-2)
        inter = jnp.einsum('bhtd,bhde->bhte', qc * jnp.exp(g), state)
        gap = jnp.where(tril, g[..., :, None, :] - g[..., None, :, :], -jnp.inf)
        att = jnp.einsum('bhtd,bhsd,bhtsd->bhts', qc, kc, jnp.exp(gap))
        out = inter + jnp.einsum('bhts,bhse->bhte', att, vc)
        g_end = g[..., -1:, :]
        state_new = jnp.exp(g_end[..., 0, :])[..., None] * state + jnp.einsum('bhsd,bhse->bhde', kc * jnp.exp(g_end - g), vc)
        return state_new, out

    init = jnp.zeros((b, h, dk, dv), f32)
    _, out = lax.scan(step, init, (to_chunks(q), to_chunks(k), to_chunks(v), to_chunks(log_a)))
    return from_chunks(out)


def join_segments(a_ctx, a_lat, rev):
    if rev:
        a_ctx, a_lat = jnp.flip(a_ctx, 2), jnp.flip(a_lat, 2)
    return jnp.concatenate([a_ctx, a_lat], axis=2)


def bidirectional(scan_fn, ctx_dirs, lat_dirs):
    outs_c, outs_l = [], []
    for d in range(2):
        rev = d == 1
        seqs = [join_segments(a_c, a_l, rev) for a_c, a_l in zip(ctx_dirs[d], lat_dirs[d])]
        out = scan_fn(*seqs)
        n_ctx = ctx_dirs[d][0].shape[2]
        o_c, o_l = out[:, :, :n_ctx], out[:, :, n_ctx:]
        if rev:
            o_c, o_l = jnp.flip(o_c, 2), jnp.flip(o_l, 2)
        outs_c.append(o_c)
        outs_l.append(o_l)
    return outs_c[0] + outs_c[1], outs_l[0] + outs_l[1]


def diff_block(q, k, v, lam):
    s = jnp.einsum('bhqmd,bhkmd->bhmqk', q, k).astype(jnp.float32) * (DIFF_DQK ** -0.5)
    p = jax.nn.softmax(s, axis=-1)
    weights = p[:, :, 0] - lam * p[:, :, 1]
    return jnp.einsum('bhqk,bhkd->bhqd', weights.astype(v.dtype), v)


def diff_attention_blocked(q, k, v, lam):
    b, h, n = q.shape[:3]
    nb = n // Q_BLOCK
    qb = jnp.moveaxis(q.reshape(b, h, nb, Q_BLOCK, 2, DIFF_DQK), 2, 0)
    out = lax.map(lambda q_blk: diff_block(q_blk, k, v, lam), qb)
    return jnp.moveaxis(out, 0, 2).reshape(b, h, n, DIFF_DV)


def token_mixers(pc, pl, conv_w, conv_b, m_gate_b, m_norm, g_w2, g_b, g_norm, d_lam, d_subln,
                 lam_init, rope_cos, rope_sin, need_ctx):
    f32 = jnp.float32

    def mlstm_prep(p):
        mq, mk, mv, _, mg = p[0:5]
        b, n, _ = mq.shape
        qk = jax.nn.silu(centred_conv(jnp.concatenate([mq, mk], axis=-1), conv_w, conv_b))
        q, k = jnp.split(qk, 2, axis=-1)
        gates = (mg + m_gate_b).astype(f32).reshape(b, n, 2, 2, MLSTM_HEADS).transpose(2, 3, 0, 4, 1)
        return (split_heads(q, MLSTM_HEADS) * (MLSTM_DK ** -0.5), split_heads(k, MLSTM_HEADS),
                split_heads(mv, MLSTM_HEADS), gates)

    def mlstm_dir(m, d):
        return (m[0], m[1], m[2], m[3][d, 0], jax.nn.log_sigmoid(m[3][d, 1]))

    m_c, m_l = mlstm_prep(pc), mlstm_prep(pl)
    hm_c, hm_l = bidirectional(mlstm_scan, [mlstm_dir(m_c, d) for d in range(2)],
                               [mlstm_dir(m_l, d) for d in range(2)])
    m_w = m_norm.reshape(MLSTM_HEADS, 1, MLSTM_DV)

    def mlstm_out(hm, p):
        return merge_heads(rmsnorm(hm, m_w)) * jax.nn.sigmoid(p[3])

    def gla_prep(p):
        gq, gk, gv, _, glr = p[5:10]
        b, n, _ = gq.shape
        lr = glr.reshape(b, n, 2, GLA_RANK)
        log_a = [split_heads(jax.nn.log_sigmoid((jnp.einsum('bnr,rk->bnk', lr[:, :, d], g_w2[d]) + g_b[d]).astype(f32)) / GLA_TAU, GLA_HEADS)
                 for d in range(2)]
        return (split_heads(gq, GLA_HEADS) * (GLA_DK ** -0.5), split_heads(gk, GLA_HEADS),
                split_heads(gv, GLA_HEADS), log_a)

    g_c, g_l = gla_prep(pc), gla_prep(pl)
    hg_c, hg_l = bidirectional(gla_scan, [(g_c[0], g_c[1], g_c[2], g_c[3][d]) for d in range(2)],
                               [(g_l[0], g_l[1], g_l[2], g_l[3][d]) for d in range(2)])
    g_w = g_norm.reshape(GLA_HEADS, 1, GLA_DV)

    def gla_out(hg, p):
        return merge_heads(rmsnorm(hg, g_w)) * jax.nn.silu(p[8])

    def diff_prep(p, rotary):
        dq, dk, dv = p[10:13]
        b, n, _ = dq.shape
        q = dq.reshape(b, n, DIFF_HEADS, 2, DIFF_DQK)
        k = dk.reshape(b, n, DIFF_HEADS, 2, DIFF_DQK)
        if rotary:
            q, k = apply_rope(q, rope_cos, rope_sin), apply_rope(k, rope_cos, rope_sin)
        return q.transpose(0, 2, 1, 3, 4), k.transpose(0, 2, 1, 3, 4), split_heads(dv, DIFF_HEADS)

    lam = (jnp.exp(jnp.sum(d_lam[0] * d_lam[1])) - jnp.exp(jnp.sum(d_lam[2] * d_lam[3]))).astype(f32) + lam_init
    q_c, k_c, v_c = diff_prep(pc, False)
    q_l, k_l, v_l = diff_prep(pl, True)
    k_all = jnp.concatenate([k_c, k_l], axis=2)
    v_all = jnp.concatenate([v_c, v_l], axis=2)
    hd_l = diff_attention_blocked(q_l, k_all, v_all, lam)

    def diff_out(hd):
        return merge_heads(rmsnorm(hd, d_subln) * (1.0 - lam_init))

    y_lat = jnp.concatenate([mlstm_out(hm_l, pl), gla_out(hg_l, pl), diff_out(hd_l)], axis=-1)
    y_ctx = None
    if need_ctx:
        hd_c = diff_block(q_c, k_c, v_c, lam)
        y_ctx = jnp.concatenate([mlstm_out(hm_c, pc), gla_out(hg_c, pc), diff_out(hd_c)], axis=-1)
    return y_ctx, y_lat


def swiglu(h, w_gate, w_up, w_down):
    return (jax.nn.silu(h @ w_gate) * (h @ w_up)) @ w_down


def hybrid_layer(xc, xl, mod_c, mod_l, n_mix_pre, n_mix_post, n_ffn_pre, n_ffn_post, w_in,
                 conv_w, conv_b, m_gate_b, m_norm, g_w2, g_b, g_norm, d_lam, d_subln, w_out,
                 w_gate, w_up, w_down, lam_init, rope_cos, rope_sin, need_ctx):
    def pre(x, w, mod, i):
        return rmsnorm(x, w) * (1.0 + mod[:, i + 1]) + mod[:, i]

    hc = pre(xc, n_mix_pre, mod_c, 0)
    hl = pre(xl, n_mix_pre, mod_l, 0)
    pc = jnp.split(hc @ w_in, MIX_OFFSETS, axis=-1)
    pl = jnp.split(hl @ w_in, MIX_OFFSETS, axis=-1)
    y_ctx, y_lat = token_mixers(pc, pl, conv_w, conv_b, m_gate_b, m_norm, g_w2, g_b, g_norm,
                                d_lam, d_subln, lam_init, rope_cos, rope_sin, need_ctx)
    xl = xl + mod_l[:, 2] * rmsnorm(y_lat.astype(xl.dtype) @ w_out, n_mix_post)
    xl = xl + mod_l[:, 5] * rmsnorm(swiglu(pre(xl, n_ffn_pre, mod_l, 3), w_gate, w_up, w_down), n_ffn_post)
    if need_ctx:
        xc = xc + mod_c[:, 2] * rmsnorm(y_ctx.astype(xc.dtype) @ w_out, n_mix_post)
        xc = xc + mod_c[:, 5] * rmsnorm(swiglu(pre(xc, n_ffn_pre, mod_c, 3), w_gate, w_up, w_down), n_ffn_post)
    return xc, xl


def setup_inputs(seed: int = 0) -> dict:
    key = jax.random.key(seed)
    ks = iter(jax.random.split(key, 32))
    f32 = jnp.float32

    def nrm(shape, scale):
        return jax.random.normal(next(ks), shape, f32) * scale

    def gain(shape):
        return 1.0 + nrm(shape, 0.05)

    gate_offset = jnp.stack([jnp.zeros((MLSTM_HEADS,), f32), jnp.linspace(3.0, 6.0, MLSTM_HEADS, dtype=f32)])
    gate_offset = jnp.tile(gate_offset[None], (2, 1, 1)).reshape(-1)
    return {
        "x": nrm((BATCH, SEQ, D_MODEL), 1.0),
        "c": nrm((BATCH, D_MODEL), 1.0),
        "ctx": nrm((BATCH, CTX_LEN, D_MODEL), 1.0),
        "c_ctx": nrm((D_MODEL,), 1.0),
        "w_mod": nrm((DEPTH, D_MODEL, 6 * D_MODEL), 0.5 * D_MODEL ** -0.5),
        "b_mod": nrm((DEPTH, 6 * D_MODEL), 0.02),
        "norm_mix_pre": gain((DEPTH, D_MODEL)),
        "norm_mix_post": gain((DEPTH, D_MODEL)),
        "norm_ffn_pre": gain((DEPTH, D_MODEL)),
        "norm_ffn_post": gain((DEPTH, D_MODEL)),
        "w_in": nrm((DEPTH, D_MODEL, IN_COLS), D_MODEL ** -0.5),
        "mlstm_conv_w": nrm((DEPTH, CONV_W, 2 * M_QK), CONV_W ** -0.5),
        "mlstm_conv_b": nrm((DEPTH, 2 * M_QK), 0.02),
        "mlstm_gate_b": gate_offset[None] + nrm((DEPTH, M_GATES), 0.1),
        "mlstm_norm": gain((DEPTH, M_V)),
        "gla_gate_w2": nrm((DEPTH, 2, GLA_RANK, G_QK), GLA_RANK ** -0.5),
        "gla_gate_b": nrm((DEPTH, 2, G_QK), 0.1),
        "gla_norm": gain((DEPTH, G_V)),
        "diff_lambda": nrm((DEPTH, 4, DIFF_DQK), 0.1),
        "diff_subln": gain((DEPTH, DIFF_DV)),
        "w_out": nrm((DEPTH, MIX_WIDTH, D_MODEL), MIX_WIDTH ** -0.5),
        "w_ffn_gate": nrm((DEPTH, D_MODEL, FFN_HIDDEN), D_MODEL ** -0.5),
        "w_ffn_up": nrm((DEPTH, D_MODEL, FFN_HIDDEN), D_MODEL ** -0.5),
        "w_ffn_down": nrm((DEPTH, FFN_HIDDEN, D_MODEL), FFN_HIDDEN ** -0.5),
    }


def reference(x, c, ctx, c_ctx, w_mod, b_mod, norm_mix_pre, norm_mix_post, norm_ffn_pre,
              norm_ffn_post, w_in, mlstm_conv_w, mlstm_conv_b, mlstm_gate_b, mlstm_norm,
              gla_gate_w2, gla_gate_b, gla_norm, diff_lambda, diff_subln, w_out,
              w_ffn_gate, w_ffn_up, w_ffn_down):
    n_lat = x.shape[1]
    rope_cos, rope_sin = axial_rope_tables(n_lat)
    d = x.shape[-1]
    xc, xl = ctx, x
    for layer in range(DEPTH):
        mod_l = (jax.nn.silu(c) @ w_mod[layer] + b_mod[layer]).reshape(c.shape[0], 6, 1, d)
        mod_c = (jax.nn.silu(c_ctx) @ w_mod[layer] + b_mod[layer]).reshape(1, 6, 1, d)
        lam_init = 0.8 - 0.6 * math.exp(-0.3 * layer)
        xc, xl = hybrid_layer(
            xc, xl, mod_c, mod_l, norm_mix_pre[layer], norm_mix_post[layer], norm_ffn_pre[layer],
            norm_ffn_post[layer], w_in[layer], mlstm_conv_w[layer], mlstm_conv_b[layer],
            mlstm_gate_b[layer], mlstm_norm[layer], gla_gate_w2[layer], gla_gate_b[layer],
            gla_norm[layer], diff_lambda[layer], diff_subln[layer], w_out[layer],
            w_ffn_gate[layer], w_ffn_up[layer], w_ffn_down[layer], lam_init, rope_cos, rope_sin,
            layer < DEPTH - 1)
    return xl
```

```python
import dataclasses
import functools
import math

import jax
import jax.numpy as jnp
from jax import lax
from jax.experimental import pallas as pl
from jax.experimental.pallas import tpu as pltpu

F32 = jnp.float32
BF16 = jnp.bfloat16
NORM_EPS = 1e-6
CHUNK = 64
GRID_W = 64
ROPE_BASE = 10000.0
GLA_TAU = 16.0
HEAD_DIM = 128
VMEM_LIMIT_BYTES = 56 * 1024 * 1024
HIGHEST = lax.Precision.HIGHEST


@dataclasses.dataclass(frozen=True)
class Cfg:
    B: int = 2
    n_ctx: int = 256
    n_lat: int = 4096
    D: int = 2048
    F: int = 5632

    @property
    def T(self):
        return self.B * (self.n_ctx + self.n_lat)

    @property
    def TL(self):
        return self.B * self.n_lat


def _cparams(n_axes):
    return pltpu.CompilerParams(dimension_semantics=("arbitrary",) * n_axes,
                                vmem_limit_bytes=VMEM_LIMIT_BYTES)


def _mod_row(i, cfg, tm):
    lt = cfg.n_lat // tm
    return jnp.where(i < cfg.B * lt, 1 + i // lt, 0)


def _chunk_block(b, c, cfg):
    ncc, ncl = cfg.n_ctx // CHUNK, cfg.n_lat // CHUNK
    return jnp.where(c < ncc, cfg.B * ncl + b * ncc + c, b * ncl + (c - ncc))


def _bwd_chunk(s, cfg):
    ncc, ncl = cfg.n_ctx // CHUNK, cfg.n_lat // CHUNK
    return jnp.where(s < ncc, ncc - 1 - s, ncc + ncl - 1 - (s - ncc))


def _log_sigmoid(x):
    return jnp.minimum(x, 0.0) - jnp.log1p(jnp.exp(-jnp.abs(x)))


def _silu(x):
    return x * jax.nn.sigmoid(x)


def _mod_kernel(c_ref, w_ref, b_ref, o_ref):
    s = _silu(c_ref[...]).astype(BF16)
    o_ref[0] = jnp.dot(s, w_ref[0].astype(BF16), preferred_element_type=F32) + b_ref[0]


def _mod_call(c8, w_mod, b_mod, tn=1024):
    depth, d, n = w_mod.shape
    return pl.pallas_call(
        _mod_kernel,
        grid=(depth, n // tn),
        in_specs=[pl.BlockSpec((8, d), lambda l, j: (0, 0)),
                  pl.BlockSpec((1, d, tn), lambda l, j: (l, 0, j)),
                  pl.BlockSpec((1, 1, tn), lambda l, j: (l, 0, j))],
        out_specs=pl.BlockSpec((1, 8, tn), lambda l, j: (l, 0, j)),
        out_shape=jax.ShapeDtypeStruct((depth, 8, n), F32),
        compiler_params=_cparams(2),
        name="adaln_mod",
    )(c8, w_mod, b_mod.reshape(depth, 1, n))


def _prenorm(x, nw, shift, scale):
    ms = jnp.mean(x * x, axis=-1, keepdims=True)
    return (x * lax.rsqrt(ms + NORM_EPS) * nw) * (1.0 + scale) + shift


def _in_kernel(x_ref, nw_ref, shift_ref, scale_ref, w_ref, ws_ref, p_ref, ps_ref, h_scr):
    @pl.when(pl.program_id(1) == 0)
    def _():
        h = _prenorm(x_ref[...], nw_ref[...], shift_ref[0], scale_ref[0]).astype(BF16)
        h_scr[...] = h
        ps_ref[...] = jnp.dot(h, ws_ref[...], preferred_element_type=F32)

    p_ref[...] = jnp.dot(h_scr[...], w_ref[...], preferred_element_type=F32)


def _in_call(x, nw, mod, w_main, w_small, cfg, tm=512, tn=512):
    t, d = x.shape
    nm = w_main.shape[1]
    row = functools.partial(_mod_row, cfg=cfg, tm=tm)
    return pl.pallas_call(
        _in_kernel,
        grid=(t // tm, nm // tn),
        in_specs=[pl.BlockSpec((tm, d), lambda i, j: (i, 0)),
                  pl.BlockSpec((1, d), lambda i, j: (0, 0)),
                  pl.BlockSpec((1, 1, d), lambda i, j: (row(i) * 6 + 0, 0, 0)),
                  pl.BlockSpec((1, 1, d), lambda i, j: (row(i) * 6 + 1, 0, 0)),
                  pl.BlockSpec((d, tn), lambda i, j: (0, j)),
                  pl.BlockSpec((d, HEAD_DIM), lambda i, j: (0, 0))],
        out_specs=[pl.BlockSpec((tm, tn), lambda i, j: (i, j)),
                   pl.BlockSpec((tm, HEAD_DIM), lambda i, j: (i, 0))],
        out_shape=[jax.ShapeDtypeStruct((t, nm), F32), jax.ShapeDtypeStruct((t, HEAD_DIM), F32)],
        scratch_shapes=[pltpu.VMEM((tm, d), BF16)],
        compiler_params=_cparams(2),
        name="in_proj",
    )(x, nw, mod, mod, w_main, w_small)


CONV_TILE = 256


def _conv_kernel(xq_ref, xk_ref, pq_ref, pk_ref, nq_ref, nk_ref, w_ref, b_ref, q_ref, k_ref, *,
                 starts, ends):
    i = pl.program_id(0)
    is_start = functools.reduce(jnp.logical_or, [i == s for s in starts])
    is_end = functools.reduce(jnp.logical_or, [i == s for s in ends])
    tr = CONV_TILE
    rows = lax.broadcasted_iota(jnp.int32, (tr, 512), 0)

    def conv(x_ref, p_ref, n_ref, half):
        x = x_ref[...]
        prev_row = jnp.where(is_start, 0.0, p_ref[7:8, :])
        next_row = jnp.where(is_end, 0.0, n_ref[0:1, :])
        xp = jnp.where(rows == 0, prev_row, pltpu.roll(x, 1, 0))
        xn = jnp.where(rows == tr - 1, next_row, pltpu.roll(x, tr - 1, 0))
        lo, hi = half * 512, (half + 1) * 512
        y = xp * w_ref[0:1, lo:hi] + x * w_ref[1:2, lo:hi] + xn * w_ref[2:3, lo:hi] + b_ref[:, lo:hi]
        return _silu(y)

    q_ref[...] = conv(xq_ref, pq_ref, nq_ref, 0) * (HEAD_DIM ** -0.5)
    k_ref[...] = conv(xk_ref, pk_ref, nk_ref, 1)


def _conv_call(p, conv_w, conv_b, cfg):
    t = p.shape[0]
    tr = CONV_TILE
    nt = t // tr
    lt, ct = cfg.n_lat // tr, cfg.n_ctx // tr
    seg_first = [b * lt for b in range(cfg.B)] + [cfg.B * lt + b * ct for b in range(cfg.B)]
    seg_last = [b * lt + lt - 1 for b in range(cfg.B)] + [cfg.B * lt + b * ct + ct - 1 for b in range(cfg.B)]
    r8 = tr // 8
    last8 = t // 8 - 1
    kern = functools.partial(_conv_kernel, starts=tuple(seg_first), ends=tuple(seg_last))
    prev = lambda c: (lambda i: (jnp.maximum(i * r8 - 1, 0), c))
    nxt = lambda c: (lambda i: (jnp.minimum((i + 1) * r8, last8), c))
    return pl.pallas_call(
        kern,
        grid=(nt,),
        in_specs=[pl.BlockSpec((tr, 512), lambda i: (i, 0)),
                  pl.BlockSpec((tr, 512), lambda i: (i, 1)),
                  pl.BlockSpec((8, 512), prev(0)), pl.BlockSpec((8, 512), prev(1)),
                  pl.BlockSpec((8, 512), nxt(0)), pl.BlockSpec((8, 512), nxt(1)),
                  pl.BlockSpec((3, 1024), lambda i: (0, 0)),
                  pl.BlockSpec((1, 1024), lambda i: (0, 0))],
        out_specs=[pl.BlockSpec((tr, 512), lambda i: (i, 0)), pl.BlockSpec((tr, 512), lambda i: (i, 0))],
        out_shape=[jax.ShapeDtypeStruct((t, 512), F32)] * 2,
        compiler_params=_cparams(1),
        name="mlstm_conv",
    )(p, p, p, p, p, p, conv_w, conv_b.reshape(1, 1024))


M_HEADS = 4


def _mlstm_kernel(qf, kf, vf, gf, qb, kb, vb, gb, brow, bcol, of, ob, c_scr, n_scr, m_scr):
    @pl.when(pl.program_id(1) == 0)
    def _():
        c_scr[...] = jnp.zeros_like(c_scr)
        n_scr[...] = jnp.zeros_like(n_scr)
        m_scr[...] = jnp.zeros_like(m_scr)

    L = CHUNK
    row = lax.broadcasted_iota(jnp.int32, (L, L), 0)
    col = lax.broadcasted_iota(jnp.int32, (L, L), 1)
    neg_inf = jnp.float32(-jnp.inf)

    for d, (q_ref, k_ref, v_ref, g_ref, o_ref) in enumerate(((qf, kf, vf, gf, of), (qb, kb, vb, gb, ob))):
        valid = (col <= row) if d == 0 else (col >= row)
        tri = valid.astype(F32)
        g = g_ref[...] + brow[...]
        gt = g_ref[...].T + bcol[...]
        cum_c = jnp.dot(tri, _log_sigmoid(g), precision=HIGHEST, preferred_element_type=F32)
        cum_r = lax.dot_general(_log_sigmoid(gt), tri, (((1,), (1,)), ((), ())), precision=HIGHEST,
                                preferred_element_type=F32)
        for h in range(M_HEADS):
            ci, cf = d * 8 + h, d * 8 + 4 + h
            hs = slice(h * HEAD_DIM, (h + 1) * HEAD_DIM)
            q = q_ref[:, hs].astype(BF16)
            k32 = k_ref[:, hs]
            k = k32.astype(BF16)
            v = v_ref[:, hs].astype(BF16)
            cf_c = cum_c[:, cf:cf + 1]
            cf_r = cum_r[cf:cf + 1, :]
            ic_c = g[:, ci:ci + 1]
            ic_r = gt[ci:ci + 1, :]
            c_st = c_scr[d, h]
            n_st = n_scr[d, h]
            m_st = m_scr[d, h][:, 0:1]

            dmat = jnp.where(valid, cf_c - cf_r + ic_r, neg_inf)
            m_inter = cf_c + m_st
            m_t = jnp.maximum(m_inter, jnp.max(dmat, axis=-1, keepdims=True))
            w_inter = jnp.exp(m_inter - m_t)
            qk = lax.dot_general(q, k, (((1,), (1,)), ((), ())), preferred_element_type=F32)
            s = qk * jnp.exp(dmat - m_t)
            num = (w_inter * jnp.dot(q, c_st.astype(BF16), preferred_element_type=F32)
                   + jnp.dot(s.astype(BF16), v, preferred_element_type=F32))
            qn = jnp.sum(q_ref[:, hs] * n_st, axis=-1, keepdims=True)
            den = w_inter * qn + jnp.sum(s, axis=-1, keepdims=True)
            o_ref[:, hs] = num / jnp.maximum(jnp.abs(den), jnp.exp(-m_t))

            f_end = cf_c[L - 1:L] if d == 0 else cf_c[0:1]
            dec = f_end - cf_c + ic_c
            m_new = jnp.maximum(f_end + m_st, jnp.max(dec, axis=0, keepdims=True))
            a_prev = jnp.exp(f_end + m_st - m_new)
            ws = jnp.exp(dec - m_new)
            wk = ws * k32
            c_scr[d, h] = a_prev * c_st + lax.dot_general(wk.astype(BF16), v, (((0,), (0,)), ((), ())),
                                                          preferred_element_type=F32)
            n_scr[d, h] = a_prev * n_st + jnp.sum(wk, axis=0, keepdims=True)
            m_scr[d, h] = jnp.broadcast_to(m_new, (1, HEAD_DIM))


def _mlstm_call(q, k, p, ps, gate_b, cfg):
    t = q.shape[0]
    nc = (cfg.n_ctx + cfg.n_lat) // CHUNK
    fwd = lambda col: (lambda b, s: (_chunk_block(b, s, cfg), col))
    bwd = lambda col: (lambda b, s: (_chunk_block(b, _bwd_chunk(s, cfg), cfg), col))
    blk = lambda w, f: pl.BlockSpec((CHUNK, w), f)
    brow = jnp.zeros((1, HEAD_DIM), F32).at[0, :16].set(gate_b)
    const = lambda b, s: (0, 0)
    return pl.pallas_call(
        _mlstm_kernel,
        grid=(cfg.B, nc),
        in_specs=[blk(512, fwd(0)), blk(512, fwd(0)), blk(512, fwd(2)), blk(HEAD_DIM, fwd(0)),
                  blk(512, bwd(0)), blk(512, bwd(0)), blk(512, bwd(2)), blk(HEAD_DIM, bwd(0)),
                  pl.BlockSpec((1, HEAD_DIM), const), pl.BlockSpec((HEAD_DIM, 1), const)],
        out_specs=[blk(512, fwd(0)), blk(512, bwd(0))],
        out_shape=[jax.ShapeDtypeStruct((t, 512), F32)] * 2,
        scratch_shapes=[pltpu.VMEM((2, M_HEADS, HEAD_DIM, HEAD_DIM), F32),
                        pltpu.VMEM((2, M_HEADS, 1, HEAD_DIM), F32),
                        pltpu.VMEM((2, M_HEADS, 1, HEAD_DIM), F32)],
        compiler_params=_cparams(2),
        name="mlstm_scan",
    )(q, k, p, ps, q, k, p, ps, brow, brow.reshape(HEAD_DIM, 1))


G_QK = 256
G_V = 512


def _gla_kernel(qkf, vf, lf, qkb, vb, lb, w2_ref, b2_ref, of, ob, s_scr, g_scr):
    @pl.when(pl.program_id(1) == 0)
    def _():
        s_scr[...] = jnp.zeros_like(s_scr)

    L = CHUNK
    row = lax.broadcasted_iota(jnp.int32, (L, L), 0)
    col = lax.broadcasted_iota(jnp.int32, (L, L), 1)
    rows_k = lax.broadcasted_iota(jnp.int32, (L, G_QK), 0)
    neg_inf = jnp.float32(-jnp.inf)
    he_r = lax.broadcasted_iota(jnp.int32, (G_QK, G_V), 0) // 64
    he_c = lax.broadcasted_iota(jnp.int32, (G_QK, G_V), 1) // HEAD_DIM
    head_expand = (he_r == he_c).astype(BF16)
    bd_r = lax.broadcasted_iota(jnp.int32, (G_V, G_QK), 0) // HEAD_DIM
    bd_c = lax.broadcasted_iota(jnp.int32, (G_V, G_QK), 1) // 64
    block_diag = (bd_r == bd_c).astype(F32)

    for d, (qk_ref, v_ref, l_ref, o_ref) in enumerate(((qkf, vf, lf, of), (qkb, vb, lb, ob))):
        cs = slice(d * G_QK, (d + 1) * G_QK)
        z = jnp.dot(l_ref[...], w2_ref[:, cs], precision=HIGHEST, preferred_element_type=F32) + b2_ref[:, cs]
        log_a = _log_sigmoid(z) * (1.0 / GLA_TAU)
        tri = ((col <= row) if d == 0 else (col >= row)).astype(F32)
        g = jnp.dot(tri, log_a, precision=HIGHEST, preferred_element_type=F32)
        g_scr[d] = g
        q = qk_ref[:, 0:G_QK] * (64 ** -0.5)
        k = qk_ref[:, G_QK:2 * G_QK]
        v = v_ref[...]
        g_end = g[L - 1:L, :] if d == 0 else g[0:1, :]
        st = s_scr[d]
        inter = lax.dot_general((q * jnp.exp(g)).astype(BF16), st.astype(BF16), (((1,), (1,)), ((), ())),
                                preferred_element_type=F32)

        def body(i, acc, d=d, g=g, q=q, qk_ref=qk_ref, v_ref=v_ref):
            k_i = qk_ref[pl.ds(i, 1), G_QK:2 * G_QK]
            g_i = g_scr[d, pl.ds(i, 1), :]
            v_i = v_ref[pl.ds(i, 1), :]
            visible = (rows_k >= i) if d == 0 else (rows_k <= i)
            p_i = (q * k_i) * jnp.exp(jnp.where(visible, g - g_i, neg_inf))
            att = jnp.dot(p_i.astype(BF16), head_expand, preferred_element_type=F32)
            return acc + att * v_i

        intra = lax.fori_loop(0, L, body, jnp.zeros((L, G_V), F32))
        o_ref[...] = inter + intra

        k_hat = k * jnp.exp(g_end - g)
        upd = lax.dot_general(v.astype(BF16), k_hat.astype(BF16), (((0,), (0,)), ((), ())),
                              preferred_element_type=F32)
        s_scr[d] = st * jnp.exp(g_end) + upd * block_diag


def _gla_call(p, ps, w2p, b2p, cfg):
    t = p.shape[0]
    nc = (cfg.n_ctx + cfg.n_lat) // CHUNK
    fwd = lambda col: (lambda b, s: (_chunk_block(b, s, cfg), col))
    bwd = lambda col: (lambda b, s: (_chunk_block(b, _bwd_chunk(s, cfg), cfg), col))
    blk = lambda w, f: pl.BlockSpec((CHUNK, w), f)
    const = lambda b, s: (0, 0)
    return pl.pallas_call(
        _gla_kernel,
        grid=(cfg.B, nc),
        in_specs=[blk(512, fwd(4)), blk(512, fwd(5)), blk(HEAD_DIM, fwd(0)),
                  blk(512, bwd(4)), blk(512, bwd(5)), blk(HEAD_DIM, bwd(0)),
                  pl.BlockSpec((HEAD_DIM, 2 * G_QK), const), pl.BlockSpec((1, 2 * G_QK), const)],
        out_specs=[blk(512, fwd(0)), blk(512, bwd(0))],
        out_shape=[jax.ShapeDtypeStruct((t, G_V), F32)] * 2,
        scratch_shapes=[pltpu.VMEM((2, G_V, G_QK), F32), pltpu.VMEM((2, CHUNK, G_QK), F32)],
        compiler_params=_cparams(2),
        name="gla_scan",
    )(p, p, ps, p, p, ps, w2p, b2p)


ROPE_TILE = 256
D_Q_BLOCK = 7


def _rope_kernel(x_ref, cos_ref, sin_ref, o_ref):
    j = pl.program_id(1)

    @pl.when(j < 4)
    def _():
        lane = lax.broadcasted_iota(jnp.int32, (ROPE_TILE, HEAD_DIM), 1)
        low = (lane % 32) < 16
        cos, sin = cos_ref[...], sin_ref[...]
        scale = jnp.where(j < 2, jnp.float32(64 ** -0.5), jnp.float32(1.0))
        for hh in range(4):
            x = x_ref[:, hh * HEAD_DIM:(hh + 1) * HEAD_DIM]
            rot = jnp.where(low, -pltpu.roll(x, HEAD_DIM - 16, 1), pltpu.roll(x, 16, 1))
            o_ref[:, hh * HEAD_DIM:(hh + 1) * HEAD_DIM] = ((x * cos + rot * sin) * scale).astype(BF16)

    @pl.when(j >= 4)
    def _():
        o_ref[...] = x_ref[...].astype(BF16)


def _rope_call(p, cos_t, sin_t):
    t = p.shape[0]
    tr = ROPE_TILE
    return pl.pallas_call(
        _rope_kernel,
        grid=(t // tr, 6),
        in_specs=[pl.BlockSpec((tr, 512), lambda i, j: (i, D_Q_BLOCK + j)),
                  pl.BlockSpec((tr, HEAD_DIM), lambda i, j: (i, 0)),
                  pl.BlockSpec((tr, HEAD_DIM), lambda i, j: (i, 0))],
        out_specs=pl.BlockSpec((tr, 512), lambda i, j: (i, j)),
        out_shape=jax.ShapeDtypeStruct((t, 3072), BF16),
        compiler_params=_cparams(2),
        name="diff_rope",
    )(p, cos_t, sin_t)


D_HEADS = 8
ATT_TQ = 256
ATT_KC = 256


def _attn_kernel(*refs, has_lat, lam_init, n_lat):
    if has_lat:
        q_ref, kc_ref, vc_ref, kl_ref, vl_ref, dl_ref, sub_ref, _, o_ref = refs
    else:
        q_ref, kc_ref, vc_ref, dl_ref, sub_ref, _, o_ref = refs
    q = q_ref[...]
    tq = q.shape[0]
    lane = lax.broadcasted_iota(jnp.int32, q.shape, 1)
    zero = jnp.zeros_like(q)
    qs = (jnp.where(lane < 64, q, zero), jnp.where(lane >= 64, q, zero))
    dl = dl_ref[...]
    lam = (jnp.exp(jnp.sum(dl[0:1] * dl[1:2], axis=-1, keepdims=True))
           - jnp.exp(jnp.sum(dl[2:3] * dl[3:4], axis=-1, keepdims=True)) + lam_init)

    def step(k, v, state):
        new = []
        for qm, (m, l, acc) in zip(qs, state):
            s = lax.dot_general(qm, k, (((1,), (1,)), ((), ())), preferred_element_type=F32)
            m_new = jnp.maximum(m, jnp.max(s, axis=-1, keepdims=True))
            alpha = jnp.exp(m - m_new)
            p = jnp.exp(s - m_new)
            l_new = alpha * l + jnp.sum(p, axis=-1, keepdims=True)
            acc_new = alpha * acc + jnp.dot(p.astype(BF16), v, preferred_element_type=F32)
            new.append((m_new, l_new, acc_new))
        return tuple(new)

    init = (jnp.full((tq, 1), -jnp.inf, F32), jnp.zeros((tq, 1), F32), jnp.zeros((tq, HEAD_DIM), F32))
    state = step(kc_ref[...], vc_ref[...], (init, init))
    if has_lat:
        def body(c, st):
            off = pl.multiple_of(c * ATT_KC, ATT_KC)
            return step(kl_ref[pl.ds(off, ATT_KC), :], vl_ref[pl.ds(off, ATT_KC), :], st)
        state = lax.fori_loop(0, n_lat // ATT_KC, body, state)
    (_, l0, a0), (_, l1, a1) = state
    out = a0 / l0 - lam * (a1 / l1)
    ms = jnp.mean(out * out, axis=-1, keepdims=True)
    o_ref[...] = (out * lax.rsqrt(ms + NORM_EPS) * sub_ref[...]) * (1.0 - lam_init)


def _attn_call(qkv, hd_prev, d_lam, d_subln, lam_init, cfg, latent):
    t = qkv.shape[0]
    nlb = cfg.n_lat // ATT_TQ
    ctx_row0 = cfg.B * cfg.n_lat // cfg.n_ctx
    kern = functools.partial(_attn_kernel, has_lat=latent, lam_init=lam_init, n_lat=cfg.n_lat)
    kc = pl.BlockSpec((cfg.n_ctx, HEAD_DIM), lambda b, h, i: (ctx_row0 + b, D_HEADS + h))
    vc = pl.BlockSpec((cfg.n_ctx, HEAD_DIM), lambda b, h, i: (ctx_row0 + b, 2 * D_HEADS + h))
    small = [pl.BlockSpec((4, 64), lambda b, h, i: (0, 0)), pl.BlockSpec((1, HEAD_DIM), lambda b, h, i: (0, 0))]
    if latent:
        tq = ATT_TQ
        grid = (cfg.B, D_HEADS, nlb)
        q_spec = pl.BlockSpec((tq, HEAD_DIM), lambda b, h, i: (b * nlb + i, h))
        kv = [kc, vc,
              pl.BlockSpec((cfg.n_lat, HEAD_DIM), lambda b, h, i: (b, D_HEADS + h)),
              pl.BlockSpec((cfg.n_lat, HEAD_DIM), lambda b, h, i: (b, 2 * D_HEADS + h))]
        out_spec = pl.BlockSpec((tq, HEAD_DIM), lambda b, h, i: (b * nlb + i, h))
    else:
        tq = cfg.n_ctx
        grid = (cfg.B, D_HEADS, 1)
        q_spec = pl.BlockSpec((tq, HEAD_DIM), lambda b, h, i: (ctx_row0 + b, h))
        kv = [kc, vc]
        out_spec = pl.BlockSpec((tq, HEAD_DIM), lambda b, h, i: (ctx_row0 + b, h))
    if hd_prev is None:
        hd_prev = jnp.zeros((8, HEAD_DIM), F32)
        aliases = {}
    else:
        aliases = {len(kv) + 3: 0}
    args = [qkv] * (1 + len(kv)) + [d_lam, d_subln.reshape(1, HEAD_DIM), hd_prev]
    return pl.pallas_call(
        kern,
        grid=grid,
        in_specs=[q_spec] + kv + small + [pl.BlockSpec(memory_space=pl.ANY)],
        out_specs=out_spec,
        out_shape=jax.ShapeDtypeStruct((t, D_HEADS * HEAD_DIM), F32),
        input_output_aliases=aliases,
        compiler_params=_cparams(3),
        name="diff_attn_lat" if latent else "diff_attn_ctx",
    )(*args)


OUT_TILE = 256


def _group_rmsnorm(x, w, groups):
    parts = []
    for gi in range(groups):
        xs = x[:, gi * HEAD_DIM:(gi + 1) * HEAD_DIM]
        ms = jnp.mean(xs * xs, axis=-1, keepdims=True)
        parts.append(xs * lax.rsqrt(ms + NORM_EPS) * w[:, gi * HEAD_DIM:(gi + 1) * HEAD_DIM])
    return jnp.concatenate(parts, axis=-1)


def _out_kernel(x_ref, hmf, hmb, hgf, hgb, hd, mo, go, mn, gn, w_ref, nw_ref, gate_ref, o_ref):
    ym = _group_rmsnorm(hmf[...] + hmb[...], mn[...], 4) * jax.nn.sigmoid(mo[...])
    yg = _group_rmsnorm(hgf[...] + hgb[...], gn[...], 4) * _silu(go[...])
    y = jnp.concatenate([ym.astype(BF16), yg.astype(BF16), hd[...].astype(BF16)], axis=-1)
    z = jnp.dot(y, w_ref[...], preferred_element_type=F32)
    ms = jnp.mean(z * z, axis=-1, keepdims=True)
    o_ref[...] = x_ref[...] + gate_ref[0] * (z * lax.rsqrt(ms + NORM_EPS) * nw_ref[...])


def _out_call(x, hmf, hmb, hgf, hgb, hd, p, m_norm, g_norm, w_out, nw, mod, cfg, n_rows):
    d = cfg.D
    tm = OUT_TILE
    row = functools.partial(_mod_row, cfg=cfg, tm=tm)
    rt = lambda w, c: pl.BlockSpec((tm, w), lambda i: (i, c))
    const = lambda i: (0, 0)
    return pl.pallas_call(
        _out_kernel,
        grid=(n_rows // tm,),
        in_specs=[rt(d, 0), rt(512, 0), rt(512, 0), rt(512, 0), rt(512, 0), rt(1024, 0),
                  rt(512, 3), rt(512, 6),
                  pl.BlockSpec((1, 512), const), pl.BlockSpec((1, 512), const),
                  pl.BlockSpec((d, d), const), pl.BlockSpec((1, d), const),
                  pl.BlockSpec((1, 1, d), lambda i: (row(i) * 6 + 2, 0, 0))],
        out_specs=rt(d, 0),
        out_shape=jax.ShapeDtypeStruct((n_rows, d), F32),
        compiler_params=_cparams(1),
        name="out_proj",
    )(x, hmf, hmb, hgf, hgb, hd, p, p, m_norm, g_norm, w_out, nw, mod)


def _ffn_kernel(x_ref, nw_ref, shift_ref, scale_ref, wg_ref, wu_ref, wd_ref, pw_ref, gate_ref, o_ref,
                h_scr, acc_scr):
    j = pl.program_id(1)

    @pl.when(j == 0)
    def _():
        h_scr[...] = _prenorm(x_ref[...], nw_ref[...], shift_ref[0], scale_ref[0]).astype(BF16)
        acc_scr[...] = jnp.zeros_like(acc_scr)

    h = h_scr[...]
    a = jnp.dot(h, wg_ref[...], preferred_element_type=F32)
    u = jnp.dot(h, wu_ref[...], preferred_element_type=F32)
    acc_scr[...] += jnp.dot((_silu(a) * u).astype(BF16), wd_ref[...], preferred_element_type=F32)

    @pl.when(j == pl.num_programs(1) - 1)
    def _():
        z = acc_scr[...]
        ms = jnp.mean(z * z, axis=-1, keepdims=True)
        o_ref[...] = x_ref[...] + gate_ref[0] * (z * lax.rsqrt(ms + NORM_EPS) * pw_ref[...])


def _ffn_call(x, nw_pre, nw_post, mod, wg, wu, wd, cfg, n_rows, tm=512, tf=512):
    d, f = wg.shape
    row = functools.partial(_mod_row, cfg=cfg, tm=tm)
    const = lambda i, j: (0, 0)
    modspec = lambda kk: pl.BlockSpec((1, 1, d), lambda i, j: (row(i) * 6 + kk, 0, 0))
    return pl.pallas_call(
        _ffn_kernel,
        grid=(n_rows // tm, f // tf),
        in_specs=[pl.BlockSpec((tm, d), lambda i, j: (i, 0)),
                  pl.BlockSpec((1, d), const), modspec(3), modspec(4),
                  pl.BlockSpec((d, tf), lambda i, j: (0, j)),
                  pl.BlockSpec((d, tf), lambda i, j: (0, j)),
                  pl.BlockSpec((tf, d), lambda i, j: (j, 0)),
                  pl.BlockSpec((1, d), const), modspec(5)],
        out_specs=pl.BlockSpec((tm, d), lambda i, j: (i, 0)),
        out_shape=jax.ShapeDtypeStruct((n_rows, d), F32),
        scratch_shapes=[pltpu.VMEM((tm, d), BF16), pltpu.VMEM((tm, d), F32)],
        compiler_params=_cparams(2),
        name="ffn",
    )(x, nw_pre, mod, mod, wg, wu, wd, nw_post, mod)


_MIX = {}
_off = 0
for _name, _w in (("m_q", 512), ("m_k", 512), ("m_v", 512), ("m_o", 512), ("m_gates", 16),
                  ("g_q", 256), ("g_k", 256), ("g_v", 512), ("g_out", 512), ("g_lr", 32),
                  ("d_q", 1024), ("d_k", 1024), ("d_v", 1024)):
    _MIX[_name] = (_off, _w)
    _off += _w
_MAIN_ORDER = ("m_q", "m_k", "m_v", "m_o", "g_q", "g_k", "g_v", "g_out", "d_q", "d_k", "d_v")


def _split_w_in(w_in):
    cols = lambda n: w_in[:, _MIX[n][0]:_MIX[n][0] + _MIX[n][1]]
    w_main = jnp.concatenate([cols(n) for n in _MAIN_ORDER], axis=1).astype(BF16)
    w_small = jnp.concatenate([cols("m_gates"), cols("g_lr"),
                               jnp.zeros((w_in.shape[0], HEAD_DIM - 48), w_in.dtype)], axis=1).astype(BF16)
    return w_main, w_small


def _rope_tables(cfg):
    rows = cfg.n_lat // GRID_W
    r = jnp.repeat(jnp.arange(rows, dtype=F32), GRID_W)
    c = jnp.tile(jnp.arange(GRID_W, dtype=F32), rows)
    half = 16
    inv_freq = ROPE_BASE ** (-jnp.arange(half, dtype=F32) / half)
    ang_r, ang_c = r[:, None] * inv_freq, c[:, None] * inv_freq
    ang = jnp.concatenate([ang_r, ang_r, ang_c, ang_c], axis=-1)
    ang = jnp.tile(ang, (cfg.B, 2))
    n_c = cfg.B * cfg.n_ctx
    cos_t = jnp.concatenate([jnp.cos(ang), jnp.ones((n_c, HEAD_DIM), F32)], axis=0)
    sin_t = jnp.concatenate([jnp.sin(ang), jnp.zeros((n_c, HEAD_DIM), F32)], axis=0)
    return cos_t, sin_t


def _layer(xt, mod, lw, lam_init, rope, cfg, need_ctx):
    d = cfg.D
    w_main, w_small = _split_w_in(lw["w_in"])
    p, ps = _in_call(xt, lw["norm_mix_pre"].reshape(1, d), mod, w_main, w_small, cfg)

    mq, mk = _conv_call(p, lw["mlstm_conv_w"], lw["mlstm_conv_b"], cfg)
    hmf, hmb = _mlstm_call(mq, mk, p, ps, lw["mlstm_gate_b"], cfg)

    w2 = lw["gla_gate_w2"]
    w2p = jnp.zeros((HEAD_DIM, 2 * G_QK), F32)
    w2p = w2p.at[16:32, 0:G_QK].set(w2[0]).at[32:48, G_QK:].set(w2[1])
    hgf, hgb = _gla_call(p, ps, w2p, lw["gla_gate_b"].reshape(1, 2 * G_QK), cfg)

    qkv = _rope_call(p, *rope)
    hd = _attn_call(qkv, None, lw["diff_lambda"], lw["diff_subln"], lam_init, cfg, latent=True)
    if need_ctx:
        hd = _attn_call(qkv, hd, lw["diff_lambda"], lw["diff_subln"], lam_init, cfg, latent=False)

    n_rows = cfg.T if need_ctx else cfg.TL
    xt = _out_call(xt, hmf, hmb, hgf, hgb, hd, p, lw["mlstm_norm"].reshape(1, 512),
                   lw["gla_norm"].reshape(1, 512), lw["w_out"].astype(BF16),
                   lw["norm_mix_post"].reshape(1, d), mod, cfg, n_rows)
    xt = _ffn_call(xt, lw["norm_ffn_pre"].reshape(1, d), lw["norm_ffn_post"].reshape(1, d), mod,
                   lw["w_ffn_gate"].astype(BF16), lw["w_ffn_up"].astype(BF16), lw["w_ffn_down"].astype(BF16),
                   cfg, n_rows)
    return xt


_LAYER_KEYS = ("norm_mix_pre", "norm_mix_post", "norm_ffn_pre", "norm_ffn_post", "w_in", "mlstm_conv_w",
               "mlstm_conv_b", "mlstm_gate_b", "mlstm_norm", "gla_gate_w2", "gla_gate_b", "gla_norm",
               "diff_lambda", "diff_subln", "w_out", "w_ffn_gate", "w_ffn_up", "w_ffn_down")


def kernel(x, c, ctx, c_ctx, w_mod, b_mod, norm_mix_pre, norm_mix_post, norm_ffn_pre, norm_ffn_post, w_in, mlstm_conv_w, mlstm_conv_b, mlstm_gate_b, mlstm_norm, gla_gate_w2, gla_gate_b, gla_norm, diff_lambda, diff_subln, w_out, w_ffn_gate, w_ffn_up, w_ffn_down):
    weights = dict(zip(_LAYER_KEYS, (norm_mix_pre, norm_mix_post, norm_ffn_pre, norm_ffn_post, w_in,
                                     mlstm_conv_w, mlstm_conv_b, mlstm_gate_b, mlstm_norm, gla_gate_w2,
                                     gla_gate_b, gla_norm, diff_lambda, diff_subln, w_out, w_ffn_gate,
                                     w_ffn_up, w_ffn_down)))
    b, n_lat, d = x.shape
    cfg = Cfg(B=b, n_ctx=ctx.shape[1], n_lat=n_lat, D=d, F=w_ffn_gate.shape[-1])
    depth = w_mod.shape[0]
    c8 = jnp.zeros((8, d), F32).at[0].set(c_ctx).at[1:1 + b].set(c)
    mods = _mod_call(c8, w_mod, b_mod).reshape(depth, 8 * 6, 1, d)
    rope = _rope_tables(cfg)
    xt = jnp.concatenate([x.reshape(b * n_lat, d), ctx.reshape(b * ctx.shape[1], d)], axis=0)
    for layer in range(depth):
        lw = {k: v[layer] for k, v in weights.items()}
        lam_init = 0.8 - 0.6 * math.exp(-0.3 * layer)
        xt = _layer(xt, mods[layer], lw, lam_init, rope, cfg, need_ctx=layer < depth - 1)
    return xt.reshape(b, n_lat, d)
```

```python
import dataclasses
import functools
import math

import jax
import jax.numpy as jnp
from jax import lax
from jax.experimental import pallas as pl
from jax.experimental.pallas import tpu as pltpu

F32 = jnp.float32
BF16 = jnp.bfloat16
NORM_EPS = 1e-6
CHUNK = 64
GRID_W = 64
ROPE_BASE = 10000.0
GLA_TAU = 16.0
HEAD_DIM = 128
VMEM_LIMIT_BYTES = 56 * 1024 * 1024
HIGHEST = lax.Precision.HIGHEST


@dataclasses.dataclass(frozen=True)
class Cfg:
    B: int = 2
    n_ctx: int = 256
    n_lat: int = 4096
    D: int = 2048
    F: int = 5632

    @property
    def T(self):
        return self.B * (self.n_ctx + self.n_lat)

    @property
    def TL(self):
        return self.B * self.n_lat


def _cparams(n_axes):
    return pltpu.CompilerParams(dimension_semantics=("arbitrary",) * n_axes,
                                vmem_limit_bytes=VMEM_LIMIT_BYTES)


def _mod_row(i, cfg, tm):
    lt = cfg.n_lat // tm
    return jnp.where(i < cfg.B * lt, 1 + i // lt, 0)


def _chunk_block(b, c, cfg):
    ncc, ncl = cfg.n_ctx // CHUNK, cfg.n_lat // CHUNK
    return jnp.where(c < ncc, cfg.B * ncl + b * ncc + c, b * ncl + (c - ncc))


def _bwd_chunk(s, cfg):
    ncc, ncl = cfg.n_ctx // CHUNK, cfg.n_lat // CHUNK
    return jnp.where(s < ncc, ncc - 1 - s, ncc + ncl - 1 - (s - ncc))


def _log_sigmoid(x):
    return jnp.minimum(x, 0.0) - jnp.log1p(jnp.exp(-jnp.abs(x)))


def _silu(x):
    return x * jax.nn.sigmoid(x)


def _mod_kernel(c_ref, w_ref, b_ref, o_ref):
    s = _silu(c_ref[...]).astype(BF16)
    o_ref[0] = jnp.dot(s, w_ref[0].astype(BF16), preferred_element_type=F32) + b_ref[0]


def _mod_call(c8, w_mod, b_mod, tn=1024):
    depth, d, n = w_mod.shape
    return pl.pallas_call(
        _mod_kernel,
        grid=(depth, n // tn),
        in_specs=[pl.BlockSpec((8, d), lambda l, j: (0, 0)),
                  pl.BlockSpec((1, d, tn), lambda l, j: (l, 0, j)),
                  pl.BlockSpec((1, 1, tn), lambda l, j: (l, 0, j))],
        out_specs=pl.BlockSpec((1, 8, tn), lambda l, j: (l, 0, j)),
        out_shape=jax.ShapeDtypeStruct((depth, 8, n), F32),
        compiler_params=_cparams(2),
        name="adaln_mod",
    )(c8, w_mod, b_mod.reshape(depth, 1, n))


def _prenorm(x, nw, shift, scale):
    ms = jnp.mean(x * x, axis=-1, keepdims=True)
    return (x * lax.rsqrt(ms + NORM_EPS) * nw) * (1.0 + scale) + shift


def _in_kernel(x_ref, nw_ref, shift_ref, scale_ref, w_ref, ws_ref, p_ref, ps_ref, h_scr):
    @pl.when(pl.program_id(1) == 0)
    def _():
        h = _prenorm(x_ref[...], nw_ref[...], shift_ref[0], scale_ref[0]).astype(BF16)
        h_scr[...] = h
        ps_ref[...] = jnp.dot(h, ws_ref[...], preferred_element_type=F32)

    p_ref[...] = jnp.dot(h_scr[...], w_ref[...], preferred_element_type=F32)


def _in_call(x, nw, mod, w_main, w_small, cfg, tm=512, tn=512):
    t, d = x.shape
    nm = w_main.shape[1]
    row = functools.partial(_mod_row, cfg=cfg, tm=tm)
    return pl.pallas_call(
        _in_kernel,
        grid=(t // tm, nm // tn),
        in_specs=[pl.BlockSpec((tm, d), lambda i, j: (i, 0)),
                  pl.BlockSpec((1, d), lambda i, j: (0, 0)),
                  pl.BlockSpec((1, 1, d), lambda i, j: (row(i) * 6 + 0, 0, 0)),
                  pl.BlockSpec((1, 1, d), lambda i, j: (row(i) * 6 + 1, 0, 0)),
                  pl.BlockSpec((d, tn), lambda i, j: (0, j)),
                  pl.BlockSpec((d, HEAD_DIM), lambda i, j: (0, 0))],
        out_specs=[pl.BlockSpec((tm, tn), lambda i, j: (i, j)),
                   pl.BlockSpec((tm, HEAD_DIM), lambda i, j: (i, 0))],
        out_shape=[jax.ShapeDtypeStruct((t, nm), F32), jax.ShapeDtypeStruct((t, HEAD_DIM), F32)],
        scratch_shapes=[pltpu.VMEM((tm, d), BF16)],
        compiler_params=_cparams(2),
        name="in_proj",
    )(x, nw, mod, mod, w_main, w_small)


CONV_TILE = 256


def _conv_kernel(xq_ref, xk_ref, pq_ref, pk_ref, nq_ref, nk_ref, w_ref, b_ref, q_ref, k_ref, *,
                 starts, ends):
    i = pl.program_id(0)
    is_start = functools.reduce(jnp.logical_or, [i == s for s in starts])
    is_end = functools.reduce(jnp.logical_or, [i == s for s in ends])
    tr = CONV_TILE
    rows = lax.broadcasted_iota(jnp.int32, (tr, 512), 0)

    def conv(x_ref, p_ref, n_ref, half):
        x = x_ref[...]
        prev_row = jnp.where(is_start, 0.0, p_ref[7:8, :])
        next_row = jnp.where(is_end, 0.0, n_ref[0:1, :])
        xp = jnp.where(rows == 0, prev_row, pltpu.roll(x, 1, 0))
        xn = jnp.where(rows == tr - 1, next_row, pltpu.roll(x, tr - 1, 0))
        lo, hi = half * 512, (half + 1) * 512
        y = xp * w_ref[0:1, lo:hi] + x * w_ref[1:2, lo:hi] + xn * w_ref[2:3, lo:hi] + b_ref[:, lo:hi]
        return _silu(y)

    q_ref[...] = conv(xq_ref, pq_ref, nq_ref, 0) * (HEAD_DIM ** -0.5)
    k_ref[...] = conv(xk_ref, pk_ref, nk_ref, 1)


def _conv_call(p, conv_w, conv_b, cfg):
    t = p.shape[0]
    tr = CONV_TILE
    nt = t // tr
    lt, ct = cfg.n_lat // tr, cfg.n_ctx // tr
    seg_first = [b * lt for b in range(cfg.B)] + [cfg.B * lt + b * ct for b in range(cfg.B)]
    seg_last = [b * lt + lt - 1 for b in range(cfg.B)] + [cfg.B * lt + b * ct + ct - 1 for b in range(cfg.B)]
    r8 = tr // 8
    last8 = t // 8 - 1
    kern = functools.partial(_conv_kernel, starts=tuple(seg_first), ends=tuple(seg_last))
    prev = lambda c: (lambda i: (jnp.maximum(i * r8 - 1, 0), c))
    nxt = lambda c: (lambda i: (jnp.minimum((i + 1) * r8, last8), c))
    return pl.pallas_call(
        kern,
        grid=(nt,),
        in_specs=[pl.BlockSpec((tr, 512), lambda i: (i, 0)),
                  pl.BlockSpec((tr, 512), lambda i: (i, 1)),
                  pl.BlockSpec((8, 512), prev(0)), pl.BlockSpec((8, 512), prev(1)),
                  pl.BlockSpec((8, 512), nxt(0)), pl.BlockSpec((8, 512), nxt(1)),
                  pl.BlockSpec((3, 1024), lambda i: (0, 0)),
                  pl.BlockSpec((1, 1024), lambda i: (0, 0))],
        out_specs=[pl.BlockSpec((tr, 512), lambda i: (i, 0)), pl.BlockSpec((tr, 512), lambda i: (i, 0))],
        out_shape=[jax.ShapeDtypeStruct((t, 512), F32)] * 2,
        compiler_params=_cparams(1),
        name="mlstm_conv",
    )(p, p, p, p, p, p, conv_w, conv_b.reshape(1, 1024))


M_HEADS = 4


def _mlstm_kernel(qf, kf, vf, gf, qb, kb, vb, gb, brow, bcol, of, ob, c_scr, n_scr, m_scr):
    @pl.when(pl.program_id(1) == 0)
    def _():
        c_scr[...] = jnp.zeros_like(c_scr)
        n_scr[...] = jnp.zeros_like(n_scr)
        m_scr[...] = jnp.zeros_like(m_scr)

    L = CHUNK
    row = lax.broadcasted_iota(jnp.int32, (L, L), 0)
    col = lax.broadcasted_iota(jnp.int32, (L, L), 1)
    neg_inf = jnp.float32(-jnp.inf)

    for d, (q_ref, k_ref, v_ref, g_ref, o_ref) in enumerate(((qf, kf, vf, gf, of), (qb, kb, vb, gb, ob))):
        valid = (col <= row) if d == 0 else (col >= row)
        tri = valid.astype(F32)
        g = g_ref[...] + brow[...]
        gt = g_ref[...].T + bcol[...]
        cum_c = jnp.dot(tri, _log_sigmoid(g), precision=HIGHEST, preferred_element_type=F32)
        cum_r = lax.dot_general(_log_sigmoid(gt), tri, (((1,), (1,)), ((), ())), precision=HIGHEST,
                                preferred_element_type=F32)
        for h in range(M_HEADS):
            ci, cf = d * 8 + h, d * 8 + 4 + h
            hs = slice(h * HEAD_DIM, (h + 1) * HEAD_DIM)
            q = q_ref[:, hs].astype(BF16)
            k32 = k_ref[:, hs]
            k = k32.astype(BF16)
            v = v_ref[:, hs].astype(BF16)
            cf_c = cum_c[:, cf:cf + 1]
            cf_r = cum_r[cf:cf + 1, :]
            ic_c = g[:, ci:ci + 1]
            ic_r = gt[ci:ci + 1, :]
            c_st = c_scr[d, h]
            n_st = n_scr[d, h]
            m_st = m_scr[d, h][:, 0:1]

            dmat = jnp.where(valid, cf_c - cf_r + ic_r, neg_inf)
            m_inter = cf_c + m_st
            m_t = jnp.maximum(m_inter, jnp.max(dmat, axis=-1, keepdims=True))
            w_inter = jnp.exp(m_inter - m_t)
            qk = lax.dot_general(q, k, (((1,), (1,)), ((), ())), preferred_element_type=F32)
            s = qk * jnp.exp(dmat - m_t)
            num = (w_inter * jnp.dot(q, c_st.astype(BF16), preferred_element_type=F32)
                   + jnp.dot(s.astype(BF16), v, preferred_element_type=F32))
            qn = jnp.sum(q_ref[:, hs] * n_st, axis=-1, keepdims=True)
            den = w_inter * qn + jnp.sum(s, axis=-1, keepdims=True)
            o_ref[:, hs] = num / jnp.maximum(jnp.abs(den), jnp.exp(-m_t))

            f_end = cf_c[L - 1:L] if d == 0 else cf_c[0:1]
            dec = f_end - cf_c + ic_c
            m_new = jnp.maximum(f_end + m_st, jnp.max(dec, axis=0, keepdims=True))
            a_prev = jnp.exp(f_end + m_st - m_new)
            ws = jnp.exp(dec - m_new)
            wk = ws * k32
            c_scr[d, h] = a_prev * c_st + lax.dot_general(wk.astype(BF16), v, (((0,), (0,)), ((), ())),
                                                          preferred_element_type=F32)
            n_scr[d, h] = a_prev * n_st + jnp.sum(wk, axis=0, keepdims=True)
            m_scr[d, h] = jnp.broadcast_to(m_new, (1, HEAD_DIM))


def _mlstm_call(q, k, p, ps, gate_b, cfg):
    t = q.shape[0]
    nc = (cfg.n_ctx + cfg.n_lat) // CHUNK
    fwd = lambda col: (lambda b, s: (_chunk_block(b, s, cfg), col))
    bwd = lambda col: (lambda b, s: (_chunk_block(b, _bwd_chunk(s, cfg), cfg), col))
    blk = lambda w, f: pl.BlockSpec((CHUNK, w), f)
    brow = jnp.zeros((1, HEAD_DIM), F32).at[0, :16].set(gate_b)
    const = lambda b, s: (0, 0)
    return pl.pallas_call(
        _mlstm_kernel,
        grid=(cfg.B, nc),
        in_specs=[blk(512, fwd(0)), blk(512, fwd(0)), blk(512, fwd(2)), blk(HEAD_DIM, fwd(0)),
                  blk(512, bwd(0)), blk(512, bwd(0)), blk(512, bwd(2)), blk(HEAD_DIM, bwd(0)),
                  pl.BlockSpec((1, HEAD_DIM), const), pl.BlockSpec((HEAD_DIM, 1), const)],
        out_specs=[blk(512, fwd(0)), blk(512, bwd(0))],
        out_shape=[jax.ShapeDtypeStruct((t, 512), F32)] * 2,
        scratch_shapes=[pltpu.VMEM((2, M_HEADS, HEAD_DIM, HEAD_DIM), F32),
                        pltpu.VMEM((2, M_HEADS, 1, HEAD_DIM), F32),
                        pltpu.VMEM((2, M_HEADS, 1, HEAD_DIM), F32)],
        compiler_params=_cparams(2),
        name="mlstm_scan",
    )(q, k, p, ps, q, k, p, ps, brow, brow.reshape(HEAD_DIM, 1))


G_QK = 256
G_V = 512


def _gla_kernel(qkf, vf, lf, qkb, vb, lb, w2_ref, b2_ref, of, ob, s_scr, g_scr):
    @pl.when(pl.program_id(1) == 0)
    def _():
        s_scr[...] = jnp.zeros_like(s_scr)

    L = CHUNK
    row = lax.broadcasted_iota(jnp.int32, (L, L), 0)
    col = lax.broadcasted_iota(jnp.int32, (L, L), 1)
    rows8 = lax.broadcasted_iota(jnp.int32, (8, G_QK), 0)
    neg_inf = jnp.float32(-jnp.inf)
    he_r = lax.broadcasted_iota(jnp.int32, (G_QK, G_V), 0) // 64
    he_c = lax.broadcasted_iota(jnp.int32, (G_QK, G_V), 1) // HEAD_DIM
    head_expand = (he_r == he_c).astype(BF16)
    bd_r = lax.broadcasted_iota(jnp.int32, (G_V, G_QK), 0) // HEAD_DIM
    bd_c = lax.broadcasted_iota(jnp.int32, (G_V, G_QK), 1) // 64
    block_diag = (bd_r == bd_c).astype(F32)

    for d, (qk_ref, v_ref, l_ref, o_ref) in enumerate(((qkf, vf, lf, of), (qkb, vb, lb, ob))):
        cs = slice(d * G_QK, (d + 1) * G_QK)
        z = jnp.dot(l_ref[...], w2_ref[:, cs], precision=HIGHEST, preferred_element_type=F32) + b2_ref[:, cs]
        log_a = _log_sigmoid(z) * (1.0 / GLA_TAU)
        tri = ((col <= row) if d == 0 else (col >= row)).astype(F32)
        g = jnp.dot(tri, log_a, precision=HIGHEST, preferred_element_type=F32)
        g_scr[d] = g
        q = qk_ref[:, 0:G_QK] * (64 ** -0.5)
        k = qk_ref[:, G_QK:2 * G_QK]
        v = v_ref[...]
        g_end = g[L - 1:L, :] if d == 0 else g[0:1, :]
        st = s_scr[d]
        inter = lax.dot_general((q * jnp.exp(g)).astype(BF16), st.astype(BF16), (((1,), (1,)), ((), ())),
                                preferred_element_type=F32)

        acc = inter
        for i in range(L):
            lo, hi = ((i // 8) * 8, L) if d == 0 else (0, (i // 8) * 8 + 8)
            edge = (lo, lo + 8) if d == 0 else (hi - 8, hi)
            k_i = qk_ref[i:i + 1, G_QK:2 * G_QK]
            g_i = g_scr[d, i:i + 1, :]
            v_i = v_ref[i:i + 1, :]
            diff = g[lo:hi] - g_i
            vis = (rows8 >= i % 8) if d == 0 else (rows8 <= i % 8)
            diff_edge = jnp.where(vis, g[edge[0]:edge[1]] - g_i, neg_inf)
            if d == 0:
                diff = jnp.concatenate([diff_edge, diff[8:]], axis=0) if hi - lo > 8 else diff_edge
            else:
                diff = jnp.concatenate([diff[:-8], diff_edge], axis=0) if hi - lo > 8 else diff_edge
            p_i = (q[lo:hi] * k_i) * jnp.exp(diff)
            att = jnp.dot(p_i.astype(BF16), head_expand, preferred_element_type=F32)
            upd = acc[lo:hi] + att * v_i
            parts = ([acc[:lo]] if lo > 0 else []) + [upd] + ([acc[hi:]] if hi < L else [])
            acc = jnp.concatenate(parts, axis=0) if len(parts) > 1 else upd
        o_ref[...] = acc

        k_hat = k * jnp.exp(g_end - g)
        upd = lax.dot_general(v.astype(BF16), k_hat.astype(BF16), (((0,), (0,)), ((), ())),
                              preferred_element_type=F32)
        s_scr[d] = st * jnp.exp(g_end) + upd * block_diag


def _gla_call(p, ps, w2p, b2p, cfg):
    t = p.shape[0]
    nc = (cfg.n_ctx + cfg.n_lat) // CHUNK
    fwd = lambda col: (lambda b, s: (_chunk_block(b, s, cfg), col))
    bwd = lambda col: (lambda b, s: (_chunk_block(b, _bwd_chunk(s, cfg), cfg), col))
    blk = lambda w, f: pl.BlockSpec((CHUNK, w), f)
    const = lambda b, s: (0, 0)
    return pl.pallas_call(
        _gla_kernel,
        grid=(cfg.B, nc),
        in_specs=[blk(512, fwd(4)), blk(512, fwd(5)), blk(HEAD_DIM, fwd(0)),
                  blk(512, bwd(4)), blk(512, bwd(5)), blk(HEAD_DIM, bwd(0)),
                  pl.BlockSpec((HEAD_DIM, 2 * G_QK), const), pl.BlockSpec((1, 2 * G_QK), const)],
        out_specs=[blk(512, fwd(0)), blk(512, bwd(0))],
        out_shape=[jax.ShapeDtypeStruct((t, G_V), F32)] * 2,
        scratch_shapes=[pltpu.VMEM((2, G_V, G_QK), F32), pltpu.VMEM((2, CHUNK, G_QK), F32)],
        compiler_params=_cparams(2),
        name="gla_scan",
    )(p, p, ps, p, p, ps, w2p, b2p)


ROPE_TILE = 256
D_Q_BLOCK = 7


def _rope_kernel(x_ref, cos_ref, sin_ref, o_ref):
    j = pl.program_id(1)

    @pl.when(j < 4)
    def _():
        lane = lax.broadcasted_iota(jnp.int32, (ROPE_TILE, HEAD_DIM), 1)
        low = (lane % 32) < 16
        cos, sin = cos_ref[...], sin_ref[...]
        scale = jnp.where(j < 2, jnp.float32(64 ** -0.5 * math.log2(math.e)), jnp.float32(1.0))
        for hh in range(4):
            x = x_ref[:, hh * HEAD_DIM:(hh + 1) * HEAD_DIM]
            rot = jnp.where(low, -pltpu.roll(x, HEAD_DIM - 16, 1), pltpu.roll(x, 16, 1))
            o_ref[:, hh * HEAD_DIM:(hh + 1) * HEAD_DIM] = ((x * cos + rot * sin) * scale).astype(BF16)

    @pl.when(j >= 4)
    def _():
        o_ref[...] = x_ref[...].astype(BF16)


def _rope_call(p, cos_t, sin_t):
    t = p.shape[0]
    tr = ROPE_TILE
    return pl.pallas_call(
        _rope_kernel,
        grid=(t // tr, 6),
        in_specs=[pl.BlockSpec((tr, 512), lambda i, j: (i, D_Q_BLOCK + j)),
                  pl.BlockSpec((tr, HEAD_DIM), lambda i, j: (i, 0)),
                  pl.BlockSpec((tr, HEAD_DIM), lambda i, j: (i, 0))],
        out_specs=pl.BlockSpec((tr, 512), lambda i, j: (i, j)),
        out_shape=jax.ShapeDtypeStruct((t, 3072), BF16),
        compiler_params=_cparams(2),
        name="diff_rope",
    )(p, cos_t, sin_t)


D_HEADS = 8
ATT_TQ = 256


def _attn_kernel(*refs, has_lat, lam_init):
    if has_lat:
        q_ref, kc_ref, vc_ref, kl_ref, vl_ref, dl_ref, sub_ref, _, o_ref, s_scr, p_scr, vo_scr = refs
        kv = ((kc_ref, vc_ref), (kl_ref, vl_ref))
    else:
        q_ref, kc_ref, vc_ref, dl_ref, sub_ref, _, o_ref, s_scr, p_scr, vo_scr = refs
        kv = ((kc_ref, vc_ref),)

    @pl.when(pl.program_id(2) == 0)
    def _():
        off = 0
        for _, v_ref in kv:
            n = v_ref.shape[0]
            vo_scr[off:off + n, 0:HEAD_DIM] = v_ref[...]
            vo_scr[off:off + n, HEAD_DIM:2 * HEAD_DIM] = jnp.ones((n, HEAD_DIM), BF16)
            off += n

    q = q_ref[...]
    lane = lax.broadcasted_iota(jnp.int32, q.shape, 1)
    zero = jnp.zeros_like(q)
    dl = dl_ref[...]
    lam = (jnp.exp(jnp.sum(dl[0:1] * dl[1:2], axis=-1, keepdims=True))
           - jnp.exp(jnp.sum(dl[2:3] * dl[3:4], axis=-1, keepdims=True)) + lam_init)
    row_maxes = []
    for m, qm in enumerate((jnp.where(lane < 64, q, zero), jnp.where(lane >= 64, q, zero))):
        off, row_max = 0, None
        for k_ref, _ in kv:
            n = k_ref.shape[0]
            s = lax.dot_general(qm, k_ref[...], (((1,), (1,)), ((), ())), preferred_element_type=F32)
            s_scr[m, :, off:off + n] = s
            smax = jnp.max(s, axis=-1, keepdims=True)
            row_max = smax if row_max is None else jnp.maximum(row_max, smax)
            off += n
        row_maxes.append(row_max)
    outs = []
    for m in range(2):
        p_scr[m] = jnp.exp2(s_scr[m] - row_maxes[m]).astype(BF16)
        acc = jnp.dot(p_scr[m], vo_scr[...], preferred_element_type=F32)
        outs.append(acc[:, 0:HEAD_DIM] / acc[:, HEAD_DIM:HEAD_DIM + 1])
    out = outs[0] - lam * outs[1]
    ms = jnp.mean(out * out, axis=-1, keepdims=True)
    o_ref[...] = (out * lax.rsqrt(ms + NORM_EPS) * sub_ref[...]) * (1.0 - lam_init)


def _attn_call(qkv, hd_prev, d_lam, d_subln, lam_init, cfg, latent):
    t = qkv.shape[0]
    nlb = cfg.n_lat // ATT_TQ
    ctx_row0 = cfg.B * cfg.n_lat // cfg.n_ctx
    kern = functools.partial(_attn_kernel, has_lat=latent, lam_init=lam_init)
    n_keys = cfg.n_ctx + (cfg.n_lat if latent else 0)
    kc = pl.BlockSpec((cfg.n_ctx, HEAD_DIM), lambda b, h, i: (ctx_row0 + b, D_HEADS + h))
    vc = pl.BlockSpec((cfg.n_ctx, HEAD_DIM), lambda b, h, i: (ctx_row0 + b, 2 * D_HEADS + h))
    small = [pl.BlockSpec((4, 64), lambda b, h, i: (0, 0)), pl.BlockSpec((1, HEAD_DIM), lambda b, h, i: (0, 0))]
    if latent:
        tq = ATT_TQ
        grid = (cfg.B, D_HEADS, nlb)
        q_spec = pl.BlockSpec((tq, HEAD_DIM), lambda b, h, i: (b * nlb + i, h))
        kv = [kc, vc,
              pl.BlockSpec((cfg.n_lat, HEAD_DIM), lambda b, h, i: (b, D_HEADS + h)),
              pl.BlockSpec((cfg.n_lat, HEAD_DIM), lambda b, h, i: (b, 2 * D_HEADS + h))]
        out_spec = pl.BlockSpec((tq, HEAD_DIM), lambda b, h, i: (b * nlb + i, h))
    else:
        tq = cfg.n_ctx
        grid = (cfg.B, D_HEADS, 1)
        q_spec = pl.BlockSpec((tq, HEAD_DIM), lambda b, h, i: (ctx_row0 + b, h))
        kv = [kc, vc]
        out_spec = pl.BlockSpec((tq, HEAD_DIM), lambda b, h, i: (ctx_row0 + b, h))
    if hd_prev is None:
        hd_prev = jnp.zeros((8, HEAD_DIM), F32)
        aliases = {}
    else:
        aliases = {len(kv) + 3: 0}
    args = [qkv] * (1 + len(kv)) + [d_lam, d_subln.reshape(1, HEAD_DIM), hd_prev]
    return pl.pallas_call(
        kern,
        grid=grid,
        in_specs=[q_spec] + kv + small + [pl.BlockSpec(memory_space=pl.ANY)],
        out_specs=out_spec,
        out_shape=jax.ShapeDtypeStruct((t, D_HEADS * HEAD_DIM), F32),
        input_output_aliases=aliases,
        scratch_shapes=[pltpu.VMEM((2, tq, n_keys), F32), pltpu.VMEM((2, tq, n_keys), BF16),
                        pltpu.VMEM((n_keys, 2 * HEAD_DIM), BF16)],
        compiler_params=_cparams(3),
        name="diff_attn_lat" if latent else "diff_attn_ctx",
    )(*args)


OUT_TILE = 256


def _group_rmsnorm(x, w, groups):
    parts = []
    for gi in range(groups):
        xs = x[:, gi * HEAD_DIM:(gi + 1) * HEAD_DIM]
        ms = jnp.mean(xs * xs, axis=-1, keepdims=True)
        parts.append(xs * lax.rsqrt(ms + NORM_EPS) * w[:, gi * HEAD_DIM:(gi + 1) * HEAD_DIM])
    return jnp.concatenate(parts, axis=-1)


def _out_kernel(x_ref, hmf, hmb, hgf, hgb, hd, mo, go, mn, gn, w_ref, nw_ref, gate_ref, o_ref):
    ym = _group_rmsnorm(hmf[...] + hmb[...], mn[...], 4) * jax.nn.sigmoid(mo[...])
    yg = _group_rmsnorm(hgf[...] + hgb[...], gn[...], 4) * _silu(go[...])
    y = jnp.concatenate([ym.astype(BF16), yg.astype(BF16), hd[...].astype(BF16)], axis=-1)
    z = jnp.dot(y, w_ref[...], preferred_element_type=F32)
    ms = jnp.mean(z * z, axis=-1, keepdims=True)
    o_ref[...] = x_ref[...] + gate_ref[0] * (z * lax.rsqrt(ms + NORM_EPS) * nw_ref[...])


def _out_call(x, hmf, hmb, hgf, hgb, hd, p, m_norm, g_norm, w_out, nw, mod, cfg, n_rows):
    d = cfg.D
    tm = OUT_TILE
    row = functools.partial(_mod_row, cfg=cfg, tm=tm)
    rt = lambda w, c: pl.BlockSpec((tm, w), lambda i: (i, c))
    const = lambda i: (0, 0)
    return pl.pallas_call(
        _out_kernel,
        grid=(n_rows // tm,),
        in_specs=[rt(d, 0), rt(512, 0), rt(512, 0), rt(512, 0), rt(512, 0), rt(1024, 0),
                  rt(512, 3), rt(512, 6),
                  pl.BlockSpec((1, 512), const), pl.BlockSpec((1, 512), const),
                  pl.BlockSpec((d, d), const), pl.BlockSpec((1, d), const),
                  pl.BlockSpec((1, 1, d), lambda i: (row(i) * 6 + 2, 0, 0))],
        out_specs=rt(d, 0),
        out_shape=jax.ShapeDtypeStruct((n_rows, d), F32),
        compiler_params=_cparams(1),
        name="out_proj",
    )(x, hmf, hmb, hgf, hgb, hd, p, p, m_norm, g_norm, w_out, nw, mod)


def _ffn_kernel(x_ref, nw_ref, shift_ref, scale_ref, wg_ref, wu_ref, wd_ref, pw_ref, gate_ref, o_ref,
                h_scr, acc_scr):
    j = pl.program_id(1)

    @pl.when(j == 0)
    def _():
        h_scr[...] = _prenorm(x_ref[...], nw_ref[...], shift_ref[0], scale_ref[0]).astype(BF16)
        acc_scr[...] = jnp.zeros_like(acc_scr)

    h = h_scr[...]
    a = jnp.dot(h, wg_ref[...], preferred_element_type=F32)
    u = jnp.dot(h, wu_ref[...], preferred_element_type=F32)
    acc_scr[...] += jnp.dot((_silu(a) * u).astype(BF16), wd_ref[...], preferred_element_type=F32)

    @pl.when(j == pl.num_programs(1) - 1)
    def _():
        z = acc_scr[...]
        ms = jnp.mean(z * z, axis=-1, keepdims=True)
        o_ref[...] = x_ref[...] + gate_ref[0] * (z * lax.rsqrt(ms + NORM_EPS) * pw_ref[...])


def _ffn_call(x, nw_pre, nw_post, mod, wg, wu, wd, cfg, n_rows, tm=512, tf=512):
    d, f = wg.shape
    row = functools.partial(_mod_row, cfg=cfg, tm=tm)
    const = lambda i, j: (0, 0)
    modspec = lambda kk: pl.BlockSpec((1, 1, d), lambda i, j: (row(i) * 6 + kk, 0, 0))
    return pl.pallas_call(
        _ffn_kernel,
        grid=(n_rows // tm, f // tf),
        in_specs=[pl.BlockSpec((tm, d), lambda i, j: (i, 0)),
                  pl.BlockSpec((1, d), const), modspec(3), modspec(4),
                  pl.BlockSpec((d, tf), lambda i, j: (0, j)),
                  pl.BlockSpec((d, tf), lambda i, j: (0, j)),
                  pl.BlockSpec((tf, d), lambda i, j: (j, 0)),
                  pl.BlockSpec((1, d), const), modspec(5)],
        out_specs=pl.BlockSpec((tm, d), lambda i, j: (i, 0)),
        out_shape=jax.ShapeDtypeStruct((n_rows, d), F32),
        scratch_shapes=[pltpu.VMEM((tm, d), BF16), pltpu.VMEM((tm, d), F32)],
        compiler_params=_cparams(2),
        name="ffn",
    )(x, nw_pre, mod, mod, wg, wu, wd, nw_post, mod)


_MIX = {}
_off = 0
for _name, _w in (("m_q", 512), ("m_k", 512), ("m_v", 512), ("m_o", 512), ("m_gates", 16),
                  ("g_q", 256), ("g_k", 256), ("g_v", 512), ("g_out", 512), ("g_lr", 32),
                  ("d_q", 1024), ("d_k", 1024), ("d_v", 1024)):
    _MIX[_name] = (_off, _w)
    _off += _w
_MAIN_ORDER = ("m_q", "m_k", "m_v", "m_o", "g_q", "g_k", "g_v", "g_out", "d_q", "d_k", "d_v")


def _split_w_in(w_in):
    cols = lambda n: w_in[:, _MIX[n][0]:_MIX[n][0] + _MIX[n][1]]
    w_main = jnp.concatenate([cols(n) for n in _MAIN_ORDER], axis=1).astype(BF16)
    w_small = jnp.concatenate([cols("m_gates"), cols("g_lr"),
                               jnp.zeros((w_in.shape[0], HEAD_DIM - 48), w_in.dtype)], axis=1).astype(BF16)
    return w_main, w_small


def _rope_tables(cfg):
    rows = cfg.n_lat // GRID_W
    r = jnp.repeat(jnp.arange(rows, dtype=F32), GRID_W)
    c = jnp.tile(jnp.arange(GRID_W, dtype=F32), rows)
    half = 16
    inv_freq = ROPE_BASE ** (-jnp.arange(half, dtype=F32) / half)
    ang_r, ang_c = r[:, None] * inv_freq, c[:, None] * inv_freq
    ang = jnp.concatenate([ang_r, ang_r, ang_c, ang_c], axis=-1)
    ang = jnp.tile(ang, (cfg.B, 2))
    n_c = cfg.B * cfg.n_ctx
    cos_t = jnp.concatenate([jnp.cos(ang), jnp.ones((n_c, HEAD_DIM), F32)], axis=0)
    sin_t = jnp.concatenate([jnp.sin(ang), jnp.zeros((n_c, HEAD_DIM), F32)], axis=0)
    return cos_t, sin_t


def _layer(xt, mod, lw, lam_init, rope, cfg, need_ctx):
    d = cfg.D
    w_main, w_small = _split_w_in(lw["w_in"])
    p, ps = _in_call(xt, lw["norm_mix_pre"].reshape(1, d), mod, w_main, w_small, cfg)

    mq, mk = _conv_call(p, lw["mlstm_conv_w"], lw["mlstm_conv_b"], cfg)
    hmf, hmb = _mlstm_call(mq, mk, p, ps, lw["mlstm_gate_b"], cfg)

    w2 = lw["gla_gate_w2"]
    w2p = jnp.zeros((HEAD_DIM, 2 * G_QK), F32)
    w2p = w2p.at[16:32, 0:G_QK].set(w2[0]).at[32:48, G_QK:].set(w2[1])
    hgf, hgb = _gla_call(p, ps, w2p, lw["gla_gate_b"].reshape(1, 2 * G_QK), cfg)

    qkv = _rope_call(p, *rope)
    hd = _attn_call(qkv, None, lw["diff_lambda"], lw["diff_subln"], lam_init, cfg, latent=True)
    if need_ctx:
        hd = _attn_call(qkv, hd, lw["diff_lambda"], lw["diff_subln"], lam_init, cfg, latent=False)

    n_rows = cfg.T if need_ctx else cfg.TL
    xt = _out_call(xt, hmf, hmb, hgf, hgb, hd, p, lw["mlstm_norm"].reshape(1, 512),
                   lw["gla_norm"].reshape(1, 512), lw["w_out"].astype(BF16),
                   lw["norm_mix_post"].reshape(1, d), mod, cfg, n_rows)
    xt = _ffn_call(xt, lw["norm_ffn_pre"].reshape(1, d), lw["norm_ffn_post"].reshape(1, d), mod,
                   lw["w_ffn_gate"].astype(BF16), lw["w_ffn_up"].astype(BF16), lw["w_ffn_down"].astype(BF16),
                   cfg, n_rows)
    return xt


_LAYER_KEYS = ("norm_mix_pre", "norm_mix_post", "norm_ffn_pre", "norm_ffn_post", "w_in", "mlstm_conv_w",
               "mlstm_conv_b", "mlstm_gate_b", "mlstm_norm", "gla_gate_w2", "gla_gate_b", "gla_norm",
               "diff_lambda", "diff_subln", "w_out", "w_ffn_gate", "w_ffn_up", "w_ffn_down")


def kernel(x, c, ctx, c_ctx, w_mod, b_mod, norm_mix_pre, norm_mix_post, norm_ffn_pre, norm_ffn_post, w_in, mlstm_conv_w, mlstm_conv_b, mlstm_gate_b, mlstm_norm, gla_gate_w2, gla_gate_b, gla_norm, diff_lambda, diff_subln, w_out, w_ffn_gate, w_ffn_up, w_ffn_down):
    weights = dict(zip(_LAYER_KEYS, (norm_mix_pre, norm_mix_post, norm_ffn_pre, norm_ffn_post, w_in,
                                     mlstm_conv_w, mlstm_conv_b, mlstm_gate_b, mlstm_norm, gla_gate_w2,
                                     gla_gate_b, gla_norm, diff_lambda, diff_subln, w_out, w_ffn_gate,
                                     w_ffn_up, w_ffn_down)))
    b, n_lat, d = x.shape
    cfg = Cfg(B=b, n_ctx=ctx.shape[1], n_lat=n_lat, D=d, F=w_ffn_gate.shape[-1])
    depth = w_mod.shape[0]
    c8 = jnp.zeros((8, d), F32).at[0].set(c_ctx).at[1:1 + b].set(c)
    mods = _mod_call(c8, w_mod, b_mod).reshape(depth, 8 * 6, 1, d)
    rope = _rope_tables(cfg)
    xt = jnp.concatenate([x.reshape(b * n_lat, d), ctx.reshape(b * ctx.shape[1], d)], axis=0)
    for layer in range(depth):
        lw = {k: v[layer] for k, v in weights.items()}
        lam_init = 0.8 - 0.6 * math.exp(-0.3 * layer)
        xt = _layer(xt, mods[layer], lw, lam_init, rope, cfg, need_ctx=layer < depth - 1)
    return xt.reshape(b, n_lat, d)
```

```python
import dataclasses
import functools
import math

import jax
import jax.numpy as jnp
from jax import lax
from jax.experimental import pallas as pl
from jax.experimental.pallas import tpu as pltpu

F32 = jnp.float32
BF16 = jnp.bfloat16
NORM_EPS = 1e-6
CHUNK = 64
GRID_W = 64
ROPE_BASE = 10000.0
GLA_TAU = 16.0
HEAD_DIM = 128
VMEM_LIMIT_BYTES = 56 * 1024 * 1024
HIGHEST = lax.Precision.HIGHEST


@dataclasses.dataclass(frozen=True)
class Cfg:
    B: int = 2
    n_ctx: int = 256
    n_lat: int = 4096
    D: int = 2048
    F: int = 5632

    @property
    def T(self):
        return self.B * (self.n_ctx + self.n_lat)

    @property
    def TL(self):
        return self.B * self.n_lat


def _cparams(n_axes):
    return pltpu.CompilerParams(dimension_semantics=("arbitrary",) * n_axes,
                                vmem_limit_bytes=VMEM_LIMIT_BYTES)


def _mod_row(i, cfg, tm):
    lt = cfg.n_lat // tm
    return jnp.where(i < cfg.B * lt, 1 + i // lt, 0)


def _chunk_block(b, c, cfg):
    ncc, ncl = cfg.n_ctx // CHUNK, cfg.n_lat // CHUNK
    return jnp.where(c < ncc, cfg.B * ncl + b * ncc + c, b * ncl + (c - ncc))


def _bwd_chunk(s, cfg):
    ncc, ncl = cfg.n_ctx // CHUNK, cfg.n_lat // CHUNK
    return jnp.where(s < ncc, ncc - 1 - s, ncc + ncl - 1 - (s - ncc))


def _log_sigmoid(x):
    return jnp.minimum(x, 0.0) - jnp.log1p(jnp.exp(-jnp.abs(x)))


def _silu(x):
    return x * jax.nn.sigmoid(x)


def _mod_kernel(c_ref, w_ref, b_ref, o_ref):
    s = _silu(c_ref[...]).astype(BF16)
    o_ref[0] = jnp.dot(s, w_ref[0].astype(BF16), preferred_element_type=F32) + b_ref[0]


def _mod_call(c8, w_mod, b_mod, tn=1024):
    depth, d, n = w_mod.shape
    return pl.pallas_call(
        _mod_kernel,
        grid=(depth, n // tn),
        in_specs=[pl.BlockSpec((8, d), lambda l, j: (0, 0)),
                  pl.BlockSpec((1, d, tn), lambda l, j: (l, 0, j)),
                  pl.BlockSpec((1, 1, tn), lambda l, j: (l, 0, j))],
        out_specs=pl.BlockSpec((1, 8, tn), lambda l, j: (l, 0, j)),
        out_shape=jax.ShapeDtypeStruct((depth, 8, n), F32),
        compiler_params=_cparams(2),
        name="adaln_mod",
    )(c8, w_mod, b_mod.reshape(depth, 1, n))


def _prenorm(x, nw, shift, scale):
    ms = jnp.mean(x * x, axis=-1, keepdims=True)
    return (x * lax.rsqrt(ms + NORM_EPS) * nw) * (1.0 + scale) + shift


D_HEADS = 8
P_COLS = 3584
QKV_COLS = 3072
IN_TILE = 256


def _in_kernel(x_ref, nw_ref, shift_ref, scale_ref, wa_ref, wb_ref, ws_ref, cos_ref, sin_ref,
               p_ref, qkv_ref, ps_ref):
    h = _prenorm(x_ref[...], nw_ref[...], shift_ref[0], scale_ref[0]).astype(BF16)
    ps_ref[...] = jnp.dot(h, ws_ref[...], preferred_element_type=F32)
    p_ref[...] = jnp.dot(h, wa_ref[...], preferred_element_type=F32)
    qkv = jnp.dot(h, wb_ref[...], preferred_element_type=F32)
    lane = lax.broadcasted_iota(jnp.int32, (IN_TILE, HEAD_DIM), 1)
    low = (lane % 32) < 16
    cos, sin = cos_ref[...], sin_ref[...]
    q_scale = 64 ** -0.5 * math.log2(math.e)
    for s in range(2 * D_HEADS):
        cs = slice(s * HEAD_DIM, (s + 1) * HEAD_DIM)
        x = qkv[:, cs]
        rot = jnp.where(low, -pltpu.roll(x, HEAD_DIM - 16, 1), pltpu.roll(x, 16, 1))
        y = x * cos + rot * sin
        qkv_ref[:, cs] = ((y * q_scale) if s < D_HEADS else y).astype(BF16)
    vs = slice(2 * D_HEADS * HEAD_DIM, QKV_COLS)
    qkv_ref[:, vs] = qkv[:, vs].astype(BF16)


def _in_call(x, nw, mod, w_a, w_b, w_small, cos_t, sin_t, cfg):
    t, d = x.shape
    tm = IN_TILE
    row = functools.partial(_mod_row, cfg=cfg, tm=tm)
    resident = lambda shape: pl.BlockSpec(shape, lambda i: (0, 0), pipeline_mode=pl.Buffered(1))
    return pl.pallas_call(
        _in_kernel,
        grid=(t // tm,),
        in_specs=[pl.BlockSpec((tm, d), lambda i: (i, 0)),
                  resident((1, d)),
                  pl.BlockSpec((1, 1, d), lambda i: (row(i) * 6 + 0, 0, 0)),
                  pl.BlockSpec((1, 1, d), lambda i: (row(i) * 6 + 1, 0, 0)),
                  resident((d, P_COLS)), resident((d, QKV_COLS)), resident((d, HEAD_DIM)),
                  pl.BlockSpec((tm, HEAD_DIM), lambda i: (i, 0)),
                  pl.BlockSpec((tm, HEAD_DIM), lambda i: (i, 0))],
        out_specs=[pl.BlockSpec((tm, P_COLS), lambda i: (i, 0)),
                   pl.BlockSpec((tm, QKV_COLS), lambda i: (i, 0)),
                   pl.BlockSpec((tm, HEAD_DIM), lambda i: (i, 0))],
        out_shape=[jax.ShapeDtypeStruct((t, P_COLS), F32), jax.ShapeDtypeStruct((t, QKV_COLS), BF16),
                   jax.ShapeDtypeStruct((t, HEAD_DIM), F32)],
        compiler_params=_cparams(1),
        name="in_proj",
    )(x, nw, mod, mod, w_a, w_b, w_small, cos_t, sin_t)


CONV_TILE = 256


def _conv_kernel(xq_ref, xk_ref, pq_ref, pk_ref, nq_ref, nk_ref, w_ref, b_ref, q_ref, k_ref, *,
                 starts, ends):
    i = pl.program_id(0)
    is_start = functools.reduce(jnp.logical_or, [i == s for s in starts])
    is_end = functools.reduce(jnp.logical_or, [i == s for s in ends])
    tr = CONV_TILE
    rows = lax.broadcasted_iota(jnp.int32, (tr, 512), 0)

    def conv(x_ref, p_ref, n_ref, half):
        x = x_ref[...]
        prev_row = jnp.where(is_start, 0.0, p_ref[7:8, :])
        next_row = jnp.where(is_end, 0.0, n_ref[0:1, :])
        xp = jnp.where(rows == 0, prev_row, pltpu.roll(x, 1, 0))
        xn = jnp.where(rows == tr - 1, next_row, pltpu.roll(x, tr - 1, 0))
        lo, hi = half * 512, (half + 1) * 512
        y = xp * w_ref[0:1, lo:hi] + x * w_ref[1:2, lo:hi] + xn * w_ref[2:3, lo:hi] + b_ref[:, lo:hi]
        return _silu(y)

    q_ref[...] = conv(xq_ref, pq_ref, nq_ref, 0) * (HEAD_DIM ** -0.5)
    k_ref[...] = conv(xk_ref, pk_ref, nk_ref, 1)


def _conv_call(p, conv_w, conv_b, cfg):
    t = p.shape[0]
    tr = CONV_TILE
    nt = t // tr
    lt, ct = cfg.n_lat // tr, cfg.n_ctx // tr
    seg_first = [b * lt for b in range(cfg.B)] + [cfg.B * lt + b * ct for b in range(cfg.B)]
    seg_last = [b * lt + lt - 1 for b in range(cfg.B)] + [cfg.B * lt + b * ct + ct - 1 for b in range(cfg.B)]
    r8 = tr // 8
    last8 = t // 8 - 1
    kern = functools.partial(_conv_kernel, starts=tuple(seg_first), ends=tuple(seg_last))
    prev = lambda c: (lambda i: (jnp.maximum(i * r8 - 1, 0), c))
    nxt = lambda c: (lambda i: (jnp.minimum((i + 1) * r8, last8), c))
    return pl.pallas_call(
        kern,
        grid=(nt,),
        in_specs=[pl.BlockSpec((tr, 512), lambda i: (i, 0)),
                  pl.BlockSpec((tr, 512), lambda i: (i, 1)),
                  pl.BlockSpec((8, 512), prev(0)), pl.BlockSpec((8, 512), prev(1)),
                  pl.BlockSpec((8, 512), nxt(0)), pl.BlockSpec((8, 512), nxt(1)),
                  pl.BlockSpec((3, 1024), lambda i: (0, 0)),
                  pl.BlockSpec((1, 1024), lambda i: (0, 0))],
        out_specs=[pl.BlockSpec((tr, 512), lambda i: (i, 0)), pl.BlockSpec((tr, 512), lambda i: (i, 0))],
        out_shape=[jax.ShapeDtypeStruct((t, 512), F32)] * 2,
        compiler_params=_cparams(1),
        name="mlstm_conv",
    )(p, p, p, p, p, p, conv_w, conv_b.reshape(1, 1024))


M_HEADS = 4


def _mlstm_kernel(qf, kf, vf, gf, qb, kb, vb, gb, brow, bcol, of, ob, c_scr, n_scr, m_scr):
    @pl.when(pl.program_id(1) == 0)
    def _():
        c_scr[...] = jnp.zeros_like(c_scr)
        n_scr[...] = jnp.zeros_like(n_scr)
        m_scr[...] = jnp.zeros_like(m_scr)

    L = CHUNK
    row = lax.broadcasted_iota(jnp.int32, (L, L), 0)
    col = lax.broadcasted_iota(jnp.int32, (L, L), 1)
    neg_inf = jnp.float32(-jnp.inf)

    for d, (q_ref, k_ref, v_ref, g_ref, o_ref) in enumerate(((qf, kf, vf, gf, of), (qb, kb, vb, gb, ob))):
        valid = (col <= row) if d == 0 else (col >= row)
        tri = valid.astype(F32)
        g = g_ref[...] + brow[...]
        gt = g_ref[...].T + bcol[...]
        cum_c = jnp.dot(tri, _log_sigmoid(g), precision=HIGHEST, preferred_element_type=F32)
        cum_r = lax.dot_general(_log_sigmoid(gt), tri, (((1,), (1,)), ((), ())), precision=HIGHEST,
                                preferred_element_type=F32)
        for h in range(M_HEADS):
            ci, cf = d * 8 + h, d * 8 + 4 + h
            hs = slice(h * HEAD_DIM, (h + 1) * HEAD_DIM)
            q = q_ref[:, hs].astype(BF16)
            k32 = k_ref[:, hs]
            k = k32.astype(BF16)
            v = v_ref[:, hs].astype(BF16)
            cf_c = cum_c[:, cf:cf + 1]
            cf_r = cum_r[cf:cf + 1, :]
            ic_c = g[:, ci:ci + 1]
            ic_r = gt[ci:ci + 1, :]
            c_st = c_scr[d, h]
            n_st = n_scr[d, h]
            m_st = m_scr[d, h][:, 0:1]

            dmat = jnp.where(valid, cf_c - cf_r + ic_r, neg_inf)
            m_inter = cf_c + m_st
            m_t = jnp.maximum(m_inter, jnp.max(dmat, axis=-1, keepdims=True))
            w_inter = jnp.exp(m_inter - m_t)
            qk = lax.dot_general(q, k, (((1,), (1,)), ((), ())), preferred_element_type=F32)
            s = qk * jnp.exp(dmat - m_t)
            num = (w_inter * jnp.dot(q, c_st.astype(BF16), preferred_element_type=F32)
                   + jnp.dot(s.astype(BF16), v, preferred_element_type=F32))
            qn = jnp.sum(q_ref[:, hs] * n_st, axis=-1, keepdims=True)
            den = w_inter * qn + jnp.sum(s, axis=-1, keepdims=True)
            o_ref[:, hs] = num / jnp.maximum(jnp.abs(den), jnp.exp(-m_t))

            f_end = cf_c[L - 1:L] if d == 0 else cf_c[0:1]
            dec = f_end - cf_c + ic_c
            m_new = jnp.maximum(f_end + m_st, jnp.max(dec, axis=0, keepdims=True))
            a_prev = jnp.exp(f_end + m_st - m_new)
            ws = jnp.exp(dec - m_new)
            wk = ws * k32
            c_scr[d, h] = a_prev * c_st + lax.dot_general(wk.astype(BF16), v, (((0,), (0,)), ((), ())),
                                                          preferred_element_type=F32)
            n_scr[d, h] = a_prev * n_st + jnp.sum(wk, axis=0, keepdims=True)
            m_scr[d, h] = jnp.broadcast_to(m_new, (1, HEAD_DIM))


def _mlstm_call(q, k, p, ps, gate_b, cfg):
    t = q.shape[0]
    nc = (cfg.n_ctx + cfg.n_lat) // CHUNK
    fwd = lambda col: (lambda b, s: (_chunk_block(b, s, cfg), col))
    bwd = lambda col: (lambda b, s: (_chunk_block(b, _bwd_chunk(s, cfg), cfg), col))
    blk = lambda w, f: pl.BlockSpec((CHUNK, w), f)
    brow = jnp.zeros((1, HEAD_DIM), F32).at[0, :16].set(gate_b)
    const = lambda b, s: (0, 0)
    return pl.pallas_call(
        _mlstm_kernel,
        grid=(cfg.B, nc),
        in_specs=[blk(512, fwd(0)), blk(512, fwd(0)), blk(512, fwd(2)), blk(HEAD_DIM, fwd(0)),
                  blk(512, bwd(0)), blk(512, bwd(0)), blk(512, bwd(2)), blk(HEAD_DIM, bwd(0)),
                  pl.BlockSpec((1, HEAD_DIM), const), pl.BlockSpec((HEAD_DIM, 1), const)],
        out_specs=[blk(512, fwd(0)), blk(512, bwd(0))],
        out_shape=[jax.ShapeDtypeStruct((t, 512), F32)] * 2,
        scratch_shapes=[pltpu.VMEM((2, M_HEADS, HEAD_DIM, HEAD_DIM), F32),
                        pltpu.VMEM((2, M_HEADS, 1, HEAD_DIM), F32),
                        pltpu.VMEM((2, M_HEADS, 1, HEAD_DIM), F32)],
        compiler_params=_cparams(2),
        name="mlstm_scan",
    )(q, k, p, ps, q, k, p, ps, brow, brow.reshape(HEAD_DIM, 1))


G_QK = 256
G_V = 512


def _gla_kernel(qkf, vf, lf, qkb, vb, lb, w2_ref, b2_ref, of, ob, s_scr, g_scr):
    @pl.when(pl.program_id(1) == 0)
    def _():
        s_scr[...] = jnp.zeros_like(s_scr)

    L = CHUNK
    row = lax.broadcasted_iota(jnp.int32, (L, L), 0)
    col = lax.broadcasted_iota(jnp.int32, (L, L), 1)
    rows8 = lax.broadcasted_iota(jnp.int32, (8, G_QK), 0)
    neg_inf = jnp.float32(-jnp.inf)
    he_r = lax.broadcasted_iota(jnp.int32, (G_QK, G_V), 0) // 64
    he_c = lax.broadcasted_iota(jnp.int32, (G_QK, G_V), 1) // HEAD_DIM
    head_expand = (he_r == he_c).astype(BF16)
    bd_r = lax.broadcasted_iota(jnp.int32, (G_V, G_QK), 0) // HEAD_DIM
    bd_c = lax.broadcasted_iota(jnp.int32, (G_V, G_QK), 1) // 64
    block_diag = (bd_r == bd_c).astype(F32)

    for d, (qk_ref, v_ref, l_ref, o_ref) in enumerate(((qkf, vf, lf, of), (qkb, vb, lb, ob))):
        cs = slice(d * G_QK, (d + 1) * G_QK)
        z = jnp.dot(l_ref[...], w2_ref[:, cs], precision=HIGHEST, preferred_element_type=F32) + b2_ref[:, cs]
        log_a = _log_sigmoid(z) * (1.0 / GLA_TAU)
        tri = ((col <= row) if d == 0 else (col >= row)).astype(F32)
        g = jnp.dot(tri, log_a, precision=HIGHEST, preferred_element_type=F32)
        g_scr[d] = g
        q = qk_ref[:, 0:G_QK] * (64 ** -0.5)
        k = qk_ref[:, G_QK:2 * G_QK]
        v = v_ref[...]
        g_end = g[L - 1:L, :] if d == 0 else g[0:1, :]
        st = s_scr[d]
        inter = lax.dot_general((q * jnp.exp(g)).astype(BF16), st.astype(BF16), (((1,), (1,)), ((), ())),
                                preferred_element_type=F32)

        acc = inter
        for i in range(L):
            lo, hi = ((i // 8) * 8, L) if d == 0 else (0, (i // 8) * 8 + 8)
            edge = (lo, lo + 8) if d == 0 else (hi - 8, hi)
            k_i = qk_ref[i:i + 1, G_QK:2 * G_QK]
            g_i = g_scr[d, i:i + 1, :]
            v_i = v_ref[i:i + 1, :]
            diff = g[lo:hi] - g_i
            vis = (rows8 >= i % 8) if d == 0 else (rows8 <= i % 8)
            diff_edge = jnp.where(vis, g[edge[0]:edge[1]] - g_i, neg_inf)
            if d == 0:
                diff = jnp.concatenate([diff_edge, diff[8:]], axis=0) if hi - lo > 8 else diff_edge
            else:
                diff = jnp.concatenate([diff[:-8], diff_edge], axis=0) if hi - lo > 8 else diff_edge
            p_i = (q[lo:hi] * k_i) * jnp.exp(diff)
            att = jnp.dot(p_i.astype(BF16), head_expand, preferred_element_type=F32)
            upd = acc[lo:hi] + att * v_i
            parts = ([acc[:lo]] if lo > 0 else []) + [upd] + ([acc[hi:]] if hi < L else [])
            acc = jnp.concatenate(parts, axis=0) if len(parts) > 1 else upd
        o_ref[...] = acc

        k_hat = k * jnp.exp(g_end - g)
        upd = lax.dot_general(v.astype(BF16), k_hat.astype(BF16), (((0,), (0,)), ((), ())),
                              preferred_element_type=F32)
        s_scr[d] = st * jnp.exp(g_end) + upd * block_diag


def _gla_call(p, ps, w2p, b2p, cfg):
    t = p.shape[0]
    nc = (cfg.n_ctx + cfg.n_lat) // CHUNK
    fwd = lambda col: (lambda b, s: (_chunk_block(b, s, cfg), col))
    bwd = lambda col: (lambda b, s: (_chunk_block(b, _bwd_chunk(s, cfg), cfg), col))
    blk = lambda w, f: pl.BlockSpec((CHUNK, w), f)
    const = lambda b, s: (0, 0)
    return pl.pallas_call(
        _gla_kernel,
        grid=(cfg.B, nc),
        in_specs=[blk(512, fwd(4)), blk(512, fwd(5)), blk(HEAD_DIM, fwd(0)),
                  blk(512, bwd(4)), blk(512, bwd(5)), blk(HEAD_DIM, bwd(0)),
                  pl.BlockSpec((HEAD_DIM, 2 * G_QK), const), pl.BlockSpec((1, 2 * G_QK), const)],
        out_specs=[blk(512, fwd(0)), blk(512, bwd(0))],
        out_shape=[jax.ShapeDtypeStruct((t, G_V), F32)] * 2,
        scratch_shapes=[pltpu.VMEM((2, G_V, G_QK), F32), pltpu.VMEM((2, CHUNK, G_QK), F32)],
        compiler_params=_cparams(2),
        name="gla_scan",
    )(p, p, ps, p, p, ps, w2p, b2p)


ATT_SUB_ROWS = 256
ATT_TQ = 512
ATT_KEY_CHUNK = 1024


def _attn_kernel(*refs, has_lat, lam_init):
    if has_lat:
        q_ref, kc_ref, vc_ref, kl_ref, vl_ref, dl_ref, sub_ref, _, o_ref, s_scr, vo_scr = refs
        kv = ((kc_ref, vc_ref), (kl_ref, vl_ref))
    else:
        q_ref, kc_ref, vc_ref, dl_ref, sub_ref, _, o_ref, s_scr, vo_scr = refs
        kv = ((kc_ref, vc_ref),)

    @pl.when(pl.program_id(2) == 0)
    def _():
        off = 0
        for _, v_ref in kv:
            n = v_ref.shape[0]
            vo_scr[off:off + n, 0:HEAD_DIM] = v_ref[...]
            vo_scr[off:off + n, HEAD_DIM:2 * HEAD_DIM] = jnp.ones((n, HEAD_DIM), BF16)
            off += n

    n_sub = q_ref.shape[0] // ATT_SUB_ROWS
    lane = lax.broadcasted_iota(jnp.int32, (ATT_SUB_ROWS, HEAD_DIM), 1)
    zero = jnp.zeros((ATT_SUB_ROWS, HEAD_DIM), BF16)
    dl = dl_ref[...]
    lam = (jnp.exp(jnp.sum(dl[0:1] * dl[1:2], axis=-1, keepdims=True))
           - jnp.exp(jnp.sum(dl[2:3] * dl[3:4], axis=-1, keepdims=True)) + lam_init)
    row_maxes = []
    for sb in range(n_sub):
        q = q_ref[sb * ATT_SUB_ROWS:(sb + 1) * ATT_SUB_ROWS, :]
        for m, qm in enumerate((jnp.where(lane < 64, q, zero), jnp.where(lane >= 64, q, zero))):
            off, row_max = 0, None
            for k_ref, _ in kv:
                n = k_ref.shape[0]
                s = lax.dot_general(qm, k_ref[...], (((1,), (1,)), ((), ())), preferred_element_type=F32)
                s_scr[2 * sb + m, :, off:off + n] = s
                smax = jnp.max(s, axis=-1, keepdims=True)
                row_max = smax if row_max is None else jnp.maximum(row_max, smax)
                off += n
            row_maxes.append(row_max)
    n_keys = s_scr.shape[-1]
    for sb in range(n_sub):
        accs = [None, None]
        for off in range(0, n_keys, ATT_KEY_CHUNK):
            n = min(ATT_KEY_CHUNK, n_keys - off)
            for m in range(2):
                p = jnp.exp2(s_scr[2 * sb + m, :, off:off + n] - row_maxes[2 * sb + m]).astype(BF16)
                part = jnp.dot(p, vo_scr[off:off + n, :], preferred_element_type=F32)
                accs[m] = part if accs[m] is None else accs[m] + part
        outs = [a[:, 0:HEAD_DIM] / a[:, HEAD_DIM:HEAD_DIM + 1] for a in accs]
        out = outs[0] - lam * outs[1]
        ms = jnp.mean(out * out, axis=-1, keepdims=True)
        o_ref[sb * ATT_SUB_ROWS:(sb + 1) * ATT_SUB_ROWS, :] = (
            (out * lax.rsqrt(ms + NORM_EPS) * sub_ref[...]) * (1.0 - lam_init))


def _attn_call(qkv, hd_prev, d_lam, d_subln, lam_init, cfg, latent):
    t = qkv.shape[0]
    nlb = cfg.n_lat // ATT_TQ
    ctx_row0 = cfg.B * cfg.n_lat // cfg.n_ctx
    kern = functools.partial(_attn_kernel, has_lat=latent, lam_init=lam_init)
    n_keys = cfg.n_ctx + (cfg.n_lat if latent else 0)
    kc = pl.BlockSpec((cfg.n_ctx, HEAD_DIM), lambda b, h, i: (ctx_row0 + b, D_HEADS + h))
    vc = pl.BlockSpec((cfg.n_ctx, HEAD_DIM), lambda b, h, i: (ctx_row0 + b, 2 * D_HEADS + h))
    small = [pl.BlockSpec((4, 64), lambda b, h, i: (0, 0)), pl.BlockSpec((1, HEAD_DIM), lambda b, h, i: (0, 0))]
    if latent:
        tq = ATT_TQ
        grid = (cfg.B, D_HEADS, nlb)
        q_spec = pl.BlockSpec((tq, HEAD_DIM), lambda b, h, i: (b * nlb + i, h))
        kv = [kc, vc,
              pl.BlockSpec((cfg.n_lat, HEAD_DIM), lambda b, h, i: (b, D_HEADS + h)),
              pl.BlockSpec((cfg.n_lat, HEAD_DIM), lambda b, h, i: (b, 2 * D_HEADS + h))]
        out_spec = pl.BlockSpec((tq, HEAD_DIM), lambda b, h, i: (b * nlb + i, h))
    else:
        tq = cfg.n_ctx
        grid = (cfg.B, D_HEADS, 1)
        q_spec = pl.BlockSpec((tq, HEAD_DIM), lambda b, h, i: (ctx_row0 + b, h))
        kv = [kc, vc]
        out_spec = pl.BlockSpec((tq, HEAD_DIM), lambda b, h, i: (ctx_row0 + b, h))
    if hd_prev is None:
        hd_prev = jnp.zeros((8, HEAD_DIM), F32)
        aliases = {}
    else:
        aliases = {len(kv) + 3: 0}
    args = [qkv] * (1 + len(kv)) + [d_lam, d_subln.reshape(1, HEAD_DIM), hd_prev]
    return pl.pallas_call(
        kern,
        grid=grid,
        in_specs=[q_spec] + kv + small + [pl.BlockSpec(memory_space=pl.ANY)],
        out_specs=out_spec,
        out_shape=jax.ShapeDtypeStruct((t, D_HEADS * HEAD_DIM), F32),
        input_output_aliases=aliases,
        scratch_shapes=[pltpu.VMEM((2 * tq // ATT_SUB_ROWS, ATT_SUB_ROWS, n_keys), F32),
                        pltpu.VMEM((n_keys, 2 * HEAD_DIM), BF16)],
        compiler_params=_cparams(3),
        name="diff_attn_lat" if latent else "diff_attn_ctx",
    )(*args)


OUT_TILE = 256


def _group_rmsnorm(x, w, groups):
    parts = []
    for gi in range(groups):
        xs = x[:, gi * HEAD_DIM:(gi + 1) * HEAD_DIM]
        ms = jnp.mean(xs * xs, axis=-1, keepdims=True)
        parts.append(xs * lax.rsqrt(ms + NORM_EPS) * w[:, gi * HEAD_DIM:(gi + 1) * HEAD_DIM])
    return jnp.concatenate(parts, axis=-1)


def _out_kernel(x_ref, hmf, hmb, hgf, hgb, hd, mo, go, mn, gn, w_ref, nw_ref, gate_ref, o_ref):
    ym = _group_rmsnorm(hmf[...] + hmb[...], mn[...], 4) * jax.nn.sigmoid(mo[...])
    yg = _group_rmsnorm(hgf[...] + hgb[...], gn[...], 4) * _silu(go[...])
    y = jnp.concatenate([ym.astype(BF16), yg.astype(BF16), hd[...].astype(BF16)], axis=-1)
    z = jnp.dot(y, w_ref[...], preferred_element_type=F32)
    ms = jnp.mean(z * z, axis=-1, keepdims=True)
    o_ref[...] = x_ref[...] + gate_ref[0] * (z * lax.rsqrt(ms + NORM_EPS) * nw_ref[...])


def _out_call(x, hmf, hmb, hgf, hgb, hd, p, m_norm, g_norm, w_out, nw, mod, cfg, n_rows):
    d = cfg.D
    tm = OUT_TILE
    row = functools.partial(_mod_row, cfg=cfg, tm=tm)
    rt = lambda w, c: pl.BlockSpec((tm, w), lambda i: (i, c))
    const = lambda i: (0, 0)
    return pl.pallas_call(
        _out_kernel,
        grid=(n_rows // tm,),
        in_specs=[rt(d, 0), rt(512, 0), rt(512, 0), rt(512, 0), rt(512, 0), rt(1024, 0),
                  rt(512, 3), rt(512, 6),
                  pl.BlockSpec((1, 512), const), pl.BlockSpec((1, 512), const),
                  pl.BlockSpec((d, d), const), pl.BlockSpec((1, d), const),
                  pl.BlockSpec((1, 1, d), lambda i: (row(i) * 6 + 2, 0, 0))],
        out_specs=rt(d, 0),
        out_shape=jax.ShapeDtypeStruct((n_rows, d), F32),
        compiler_params=_cparams(1),
        name="out_proj",
    )(x, hmf, hmb, hgf, hgb, hd, p, p, m_norm, g_norm, w_out, nw, mod)


def _ffn_kernel(x_ref, nw_ref, shift_ref, scale_ref, wg_ref, wu_ref, wd_ref, pw_ref, gate_ref, o_ref,
                h_scr, acc_scr):
    j = pl.program_id(1)

    @pl.when(j == 0)
    def _():
        h_scr[...] = _prenorm(x_ref[...], nw_ref[...], shift_ref[0], scale_ref[0]).astype(BF16)
        acc_scr[...] = jnp.zeros_like(acc_scr)

    h = h_scr[...]
    a = jnp.dot(h, wg_ref[...], preferred_element_type=F32)
    u = jnp.dot(h, wu_ref[...], preferred_element_type=F32)
    acc_scr[...] += jnp.dot((_silu(a) * u).astype(BF16), wd_ref[...], preferred_element_type=F32)

    @pl.when(j == pl.num_programs(1) - 1)
    def _():
        z = acc_scr[...]
        ms = jnp.mean(z * z, axis=-1, keepdims=True)
        o_ref[...] = x_ref[...] + gate_ref[0] * (z * lax.rsqrt(ms + NORM_EPS) * pw_ref[...])


def _ffn_call(x, nw_pre, nw_post, mod, wg, wu, wd, cfg, n_rows, tm=512, tf=512):
    d, f = wg.shape
    row = functools.partial(_mod_row, cfg=cfg, tm=tm)
    const = lambda i, j: (0, 0)
    modspec = lambda kk: pl.BlockSpec((1, 1, d), lambda i, j: (row(i) * 6 + kk, 0, 0))
    return pl.pallas_call(
        _ffn_kernel,
        grid=(n_rows // tm, f // tf),
        in_specs=[pl.BlockSpec((tm, d), lambda i, j: (i, 0)),
                  pl.BlockSpec((1, d), const), modspec(3), modspec(4),
                  pl.BlockSpec((d, tf), lambda i, j: (0, j)),
                  pl.BlockSpec((d, tf), lambda i, j: (0, j)),
                  pl.BlockSpec((tf, d), lambda i, j: (j, 0)),
                  pl.BlockSpec((1, d), const), modspec(5)],
        out_specs=pl.BlockSpec((tm, d), lambda i, j: (i, 0)),
        out_shape=jax.ShapeDtypeStruct((n_rows, d), F32),
        scratch_shapes=[pltpu.VMEM((tm, d), BF16), pltpu.VMEM((tm, d), F32)],
        compiler_params=_cparams(2),
        name="ffn",
    )(x, nw_pre, mod, mod, wg, wu, wd, nw_post, mod)


_MIX = {}
_off = 0
for _name, _w in (("m_q", 512), ("m_k", 512), ("m_v", 512), ("m_o", 512), ("m_gates", 16),
                  ("g_q", 256), ("g_k", 256), ("g_v", 512), ("g_out", 512), ("g_lr", 32),
                  ("d_q", 1024), ("d_k", 1024), ("d_v", 1024)):
    _MIX[_name] = (_off, _w)
    _off += _w
_P_ORDER = ("m_q", "m_k", "m_v", "m_o", "g_q", "g_k", "g_v", "g_out")
_QKV_ORDER = ("d_q", "d_k", "d_v")


def _split_w_in(w_in):
    cols = lambda n: w_in[:, _MIX[n][0]:_MIX[n][0] + _MIX[n][1]]
    w_a = jnp.concatenate([cols(n) for n in _P_ORDER], axis=1).astype(BF16)
    w_b = jnp.concatenate([cols(n) for n in _QKV_ORDER], axis=1).astype(BF16)
    w_small = jnp.concatenate([cols("m_gates"), cols("g_lr"),
                               jnp.zeros((w_in.shape[0], HEAD_DIM - 48), w_in.dtype)], axis=1).astype(BF16)
    return w_a, w_b, w_small


def _rope_tables(cfg):
    rows = cfg.n_lat // GRID_W
    r = jnp.repeat(jnp.arange(rows, dtype=F32), GRID_W)
    c = jnp.tile(jnp.arange(GRID_W, dtype=F32), rows)
    half = 16
    inv_freq = ROPE_BASE ** (-jnp.arange(half, dtype=F32) / half)
    ang_r, ang_c = r[:, None] * inv_freq, c[:, None] * inv_freq
    ang = jnp.concatenate([ang_r, ang_r, ang_c, ang_c], axis=-1)
    ang = jnp.tile(ang, (cfg.B, 2))
    n_c = cfg.B * cfg.n_ctx
    cos_t = jnp.concatenate([jnp.cos(ang), jnp.ones((n_c, HEAD_DIM), F32)], axis=0)
    sin_t = jnp.concatenate([jnp.sin(ang), jnp.zeros((n_c, HEAD_DIM), F32)], axis=0)
    return cos_t, sin_t


def _layer(xt, mod, lw, lam_init, rope, cfg, need_ctx):
    d = cfg.D
    p, qkv, ps = _in_call(xt, lw["norm_mix_pre"].reshape(1, d), mod, *_split_w_in(lw["w_in"]), *rope, cfg)

    mq, mk = _conv_call(p, lw["mlstm_conv_w"], lw["mlstm_conv_b"], cfg)
    hmf, hmb = _mlstm_call(mq, mk, p, ps, lw["mlstm_gate_b"], cfg)

    w2 = lw["gla_gate_w2"]
    w2p = jnp.zeros((HEAD_DIM, 2 * G_QK), F32)
    w2p = w2p.at[16:32, 0:G_QK].set(w2[0]).at[32:48, G_QK:].set(w2[1])
    hgf, hgb = _gla_call(p, ps, w2p, lw["gla_gate_b"].reshape(1, 2 * G_QK), cfg)

    hd = _attn_call(qkv, None, lw["diff_lambda"], lw["diff_subln"], lam_init, cfg, latent=True)
    if need_ctx:
        hd = _attn_call(qkv, hd, lw["diff_lambda"], lw["diff_subln"], lam_init, cfg, latent=False)

    n_rows = cfg.T if need_ctx else cfg.TL
    xt = _out_call(xt, hmf, hmb, hgf, hgb, hd, p, lw["mlstm_norm"].reshape(1, 512),
                   lw["gla_norm"].reshape(1, 512), lw["w_out"].astype(BF16),
                   lw["norm_mix_post"].reshape(1, d), mod, cfg, n_rows)
    xt = _ffn_call(xt, lw["norm_ffn_pre"].reshape(1, d), lw["norm_ffn_post"].reshape(1, d), mod,
                   lw["w_ffn_gate"].astype(BF16), lw["w_ffn_up"].astype(BF16), lw["w_ffn_down"].astype(BF16),
                   cfg, n_rows)
    return xt


_LAYER_KEYS = ("norm_mix_pre", "norm_mix_post", "norm_ffn_pre", "norm_ffn_post", "w_in", "mlstm_conv_w",
               "mlstm_conv_b", "mlstm_gate_b", "mlstm_norm", "gla_gate_w2", "gla_gate_b", "gla_norm",
               "diff_lambda", "diff_subln", "w_out", "w_ffn_gate", "w_ffn_up", "w_ffn_down")


def kernel(x, c, ctx, c_ctx, w_mod, b_mod, norm_mix_pre, norm_mix_post, norm_ffn_pre, norm_ffn_post, w_in, mlstm_conv_w, mlstm_conv_b, mlstm_gate_b, mlstm_norm, gla_gate_w2, gla_gate_b, gla_norm, diff_lambda, diff_subln, w_out, w_ffn_gate, w_ffn_up, w_ffn_down):
    weights = dict(zip(_LAYER_KEYS, (norm_mix_pre, norm_mix_post, norm_ffn_pre, norm_ffn_post, w_in,
                                     mlstm_conv_w, mlstm_conv_b, mlstm_gate_b, mlstm_norm, gla_gate_w2,
                                     gla_gate_b, gla_norm, diff_lambda, diff_subln, w_out, w_ffn_gate,
                                     w_ffn_up, w_ffn_down)))
    b, n_lat, d = x.shape
    cfg = Cfg(B=b, n_ctx=ctx.shape[1], n_lat=n_lat, D=d, F=w_ffn_gate.shape[-1])
    depth = w_mod.shape[0]
    c8 = jnp.zeros((8, d), F32).at[0].set(c_ctx).at[1:1 + b].set(c)
    mods = _mod_call(c8, w_mod, b_mod).reshape(depth, 8 * 6, 1, d)
    rope = _rope_tables(cfg)
    xt = jnp.concatenate([x.reshape(b * n_lat, d), ctx.reshape(b * ctx.shape[1], d)], axis=0)
    for layer in range(depth):
        lw = {k: v[layer] for k, v in weights.items()}
        lam_init = 0.8 - 0.6 * math.exp(-0.3 * layer)
        xt = _layer(xt, mods[layer], lw, lam_init, rope, cfg, need_ctx=layer < depth - 1)
    return xt.reshape(b, n_lat, d)
```

```python
import dataclasses
import functools
import math

import jax
import jax.numpy as jnp
from jax import lax
from jax.experimental import pallas as pl
from jax.experimental.pallas import tpu as pltpu

F32 = jnp.float32
BF16 = jnp.bfloat16
NORM_EPS = 1e-6
CHUNK = 64
GRID_W = 64
ROPE_BASE = 10000.0
GLA_TAU = 16.0
HEAD_DIM = 128
VMEM_LIMIT_BYTES = 56 * 1024 * 1024
HIGHEST = lax.Precision.HIGHEST


@dataclasses.dataclass(frozen=True)
class Cfg:
    B: int = 2
    n_ctx: int = 256
    n_lat: int = 4096
    D: int = 2048
    F: int = 5632

    @property
    def T(self):
        return self.B * (self.n_ctx + self.n_lat)

    @property
    def TL(self):
        return self.B * self.n_lat


def _cparams(n_axes):
    return pltpu.CompilerParams(dimension_semantics=("arbitrary",) * n_axes,
                                vmem_limit_bytes=VMEM_LIMIT_BYTES)


def _mod_row(i, cfg, tm):
    lt = cfg.n_lat // tm
    return jnp.where(i < cfg.B * lt, 1 + i // lt, 0)


def _chunk_block(b, c, cfg):
    ncc, ncl = cfg.n_ctx // CHUNK, cfg.n_lat // CHUNK
    return jnp.where(c < ncc, cfg.B * ncl + b * ncc + c, b * ncl + (c - ncc))


def _bwd_chunk(s, cfg):
    ncc, ncl = cfg.n_ctx // CHUNK, cfg.n_lat // CHUNK
    return jnp.where(s < ncc, ncc - 1 - s, ncc + ncl - 1 - (s - ncc))


def _log_sigmoid(x):
    return jnp.minimum(x, 0.0) - jnp.log1p(jnp.exp(-jnp.abs(x)))


def _silu(x):
    return x * jax.nn.sigmoid(x)


def _mod_kernel(c_ref, w_ref, b_ref, o_ref):
    s = _silu(c_ref[...]).astype(BF16)
    o_ref[0] = jnp.dot(s, w_ref[0].astype(BF16), preferred_element_type=F32) + b_ref[0]


def _mod_call(c8, w_mod, b_mod, tn=1024):
    depth, d, n = w_mod.shape
    return pl.pallas_call(
        _mod_kernel,
        grid=(depth, n // tn),
        in_specs=[pl.BlockSpec((8, d), lambda l, j: (0, 0)),
                  pl.BlockSpec((1, d, tn), lambda l, j: (l, 0, j)),
                  pl.BlockSpec((1, 1, tn), lambda l, j: (l, 0, j))],
        out_specs=pl.BlockSpec((1, 8, tn), lambda l, j: (l, 0, j)),
        out_shape=jax.ShapeDtypeStruct((depth, 8, n), F32),
        compiler_params=_cparams(2),
        name="adaln_mod",
    )(c8, w_mod, b_mod.reshape(depth, 1, n))


def _prenorm(x, nw, shift, scale):
    ms = jnp.mean(x * x, axis=-1, keepdims=True)
    return (x * lax.rsqrt(ms + NORM_EPS) * nw) * (1.0 + scale) + shift


D_HEADS = 8
P_COLS = 3584
QKV_COLS = 3072
IN_TILE = 256


def _in_kernel(x_ref, nw_ref, shift_ref, scale_ref, wa_ref, wb_ref, ws_ref, cos_ref, sin_ref,
               p_ref, qkv_ref, ps_ref):
    h = _prenorm(x_ref[...], nw_ref[...], shift_ref[0], scale_ref[0]).astype(BF16)
    ps_ref[...] = jnp.dot(h, ws_ref[...], preferred_element_type=F32)
    p_ref[...] = jnp.dot(h, wa_ref[...], preferred_element_type=F32)
    qkv = jnp.dot(h, wb_ref[...], preferred_element_type=F32)
    lane = lax.broadcasted_iota(jnp.int32, (IN_TILE, HEAD_DIM), 1)
    low = (lane % 32) < 16
    cos, sin = cos_ref[...], sin_ref[...]
    q_scale = 64 ** -0.5 * math.log2(math.e)
    for s in range(2 * D_HEADS):
        cs = slice(s * HEAD_DIM, (s + 1) * HEAD_DIM)
        x = qkv[:, cs]
        rot = jnp.where(low, -pltpu.roll(x, HEAD_DIM - 16, 1), pltpu.roll(x, 16, 1))
        y = x * cos + rot * sin
        qkv_ref[:, cs] = ((y * q_scale) if s < D_HEADS else y).astype(BF16)
    vs = slice(2 * D_HEADS * HEAD_DIM, QKV_COLS)
    qkv_ref[:, vs] = qkv[:, vs].astype(BF16)


def _in_call(x, nw, mod, w_a, w_b, w_small, cos_t, sin_t, cfg):
    t, d = x.shape
    tm = IN_TILE
    row = functools.partial(_mod_row, cfg=cfg, tm=tm)
    resident = lambda shape: pl.BlockSpec(shape, lambda i: (0, 0), pipeline_mode=pl.Buffered(1))
    return pl.pallas_call(
        _in_kernel,
        grid=(t // tm,),
        in_specs=[pl.BlockSpec((tm, d), lambda i: (i, 0)),
                  resident((1, d)),
                  pl.BlockSpec((1, 1, d), lambda i: (row(i) * 6 + 0, 0, 0)),
                  pl.BlockSpec((1, 1, d), lambda i: (row(i) * 6 + 1, 0, 0)),
                  resident((d, P_COLS)), resident((d, QKV_COLS)), resident((d, HEAD_DIM)),
                  pl.BlockSpec((tm, HEAD_DIM), lambda i: (i, 0)),
                  pl.BlockSpec((tm, HEAD_DIM), lambda i: (i, 0))],
        out_specs=[pl.BlockSpec((tm, P_COLS), lambda i: (i, 0)),
                   pl.BlockSpec((tm, QKV_COLS), lambda i: (i, 0)),
                   pl.BlockSpec((tm, HEAD_DIM), lambda i: (i, 0))],
        out_shape=[jax.ShapeDtypeStruct((t, P_COLS), F32), jax.ShapeDtypeStruct((t, QKV_COLS), BF16),
                   jax.ShapeDtypeStruct((t, HEAD_DIM), F32)],
        compiler_params=_cparams(1),
        name="in_proj",
    )(x, nw, mod, mod, w_a, w_b, w_small, cos_t, sin_t)


CONV_TILE = 256


def _conv_kernel(xq_ref, xk_ref, pq_ref, pk_ref, nq_ref, nk_ref, w_ref, b_ref, q_ref, k_ref, *,
                 starts, ends):
    i = pl.program_id(0)
    is_start = functools.reduce(jnp.logical_or, [i == s for s in starts])
    is_end = functools.reduce(jnp.logical_or, [i == s for s in ends])
    tr = CONV_TILE
    rows = lax.broadcasted_iota(jnp.int32, (tr, 512), 0)

    def conv(x_ref, p_ref, n_ref, half):
        x = x_ref[...]
        prev_row = jnp.where(is_start, 0.0, p_ref[7:8, :])
        next_row = jnp.where(is_end, 0.0, n_ref[0:1, :])
        xp = jnp.where(rows == 0, prev_row, pltpu.roll(x, 1, 0))
        xn = jnp.where(rows == tr - 1, next_row, pltpu.roll(x, tr - 1, 0))
        lo, hi = half * 512, (half + 1) * 512
        y = xp * w_ref[0:1, lo:hi] + x * w_ref[1:2, lo:hi] + xn * w_ref[2:3, lo:hi] + b_ref[:, lo:hi]
        return _silu(y)

    q_ref[...] = conv(xq_ref, pq_ref, nq_ref, 0) * (HEAD_DIM ** -0.5)
    k_ref[...] = conv(xk_ref, pk_ref, nk_ref, 1)


def _conv_call(p, conv_w, conv_b, cfg):
    t = p.shape[0]
    tr = CONV_TILE
    nt = t // tr
    lt, ct = cfg.n_lat // tr, cfg.n_ctx // tr
    seg_first = [b * lt for b in range(cfg.B)] + [cfg.B * lt + b * ct for b in range(cfg.B)]
    seg_last = [b * lt + lt - 1 for b in range(cfg.B)] + [cfg.B * lt + b * ct + ct - 1 for b in range(cfg.B)]
    r8 = tr // 8
    last8 = t // 8 - 1
    kern = functools.partial(_conv_kernel, starts=tuple(seg_first), ends=tuple(seg_last))
    prev = lambda c: (lambda i: (jnp.maximum(i * r8 - 1, 0), c))
    nxt = lambda c: (lambda i: (jnp.minimum((i + 1) * r8, last8), c))
    return pl.pallas_call(
        kern,
        grid=(nt,),
        in_specs=[pl.BlockSpec((tr, 512), lambda i: (i, 0)),
                  pl.BlockSpec((tr, 512), lambda i: (i, 1)),
                  pl.BlockSpec((8, 512), prev(0)), pl.BlockSpec((8, 512), prev(1)),
                  pl.BlockSpec((8, 512), nxt(0)), pl.BlockSpec((8, 512), nxt(1)),
                  pl.BlockSpec((3, 1024), lambda i: (0, 0)),
                  pl.BlockSpec((1, 1024), lambda i: (0, 0))],
        out_specs=[pl.BlockSpec((tr, 512), lambda i: (i, 0)), pl.BlockSpec((tr, 512), lambda i: (i, 0))],
        out_shape=[jax.ShapeDtypeStruct((t, 512), F32)] * 2,
        compiler_params=_cparams(1),
        name="mlstm_conv",
    )(p, p, p, p, p, p, conv_w, conv_b.reshape(1, 1024))


M_HEADS = 4


def _mlstm_kernel(qf, kf, vf, gf, qb, kb, vb, gb, brow, bcol, of, ob, c_scr, n_scr, m_scr):
    @pl.when(pl.program_id(1) == 0)
    def _():
        c_scr[...] = jnp.zeros_like(c_scr)
        n_scr[...] = jnp.zeros_like(n_scr)
        m_scr[...] = jnp.zeros_like(m_scr)

    L = CHUNK
    row = lax.broadcasted_iota(jnp.int32, (L, L), 0)
    col = lax.broadcasted_iota(jnp.int32, (L, L), 1)
    neg_inf = jnp.float32(-jnp.inf)

    for d, (q_ref, k_ref, v_ref, g_ref, o_ref) in enumerate(((qf, kf, vf, gf, of), (qb, kb, vb, gb, ob))):
        valid = (col <= row) if d == 0 else (col >= row)
        tri = valid.astype(F32)
        g = g_ref[...] + brow[...]
        gt = g_ref[...].T + bcol[...]
        cum_c = jnp.dot(tri, _log_sigmoid(g), precision=HIGHEST, preferred_element_type=F32)
        cum_r = lax.dot_general(_log_sigmoid(gt), tri, (((1,), (1,)), ((), ())), precision=HIGHEST,
                                preferred_element_type=F32)
        for h in range(M_HEADS):
            ci, cf = d * 8 + h, d * 8 + 4 + h
            hs = slice(h * HEAD_DIM, (h + 1) * HEAD_DIM)
            q = q_ref[:, hs].astype(BF16)
            k32 = k_ref[:, hs]
            k = k32.astype(BF16)
            v = v_ref[:, hs].astype(BF16)
            cf_c = cum_c[:, cf:cf + 1]
            cf_r = cum_r[cf:cf + 1, :]
            ic_c = g[:, ci:ci + 1]
            ic_r = gt[ci:ci + 1, :]
            c_st = c_scr[d, h]
            n_st = n_scr[d, h]
            m_st = m_scr[d, h][:, 0:1]

            dmat = jnp.where(valid, cf_c - cf_r + ic_r, neg_inf)
            m_inter = cf_c + m_st
            m_t = jnp.maximum(m_inter, jnp.max(dmat, axis=-1, keepdims=True))
            w_inter = jnp.exp(m_inter - m_t)
            qk = lax.dot_general(q, k, (((1,), (1,)), ((), ())), preferred_element_type=F32)
            s = qk * jnp.exp(dmat - m_t)
            num = (w_inter * jnp.dot(q, c_st.astype(BF16), preferred_element_type=F32)
                   + jnp.dot(s.astype(BF16), v, preferred_element_type=F32))
            qn = jnp.sum(q_ref[:, hs] * n_st, axis=-1, keepdims=True)
            den = w_inter * qn + jnp.sum(s, axis=-1, keepdims=True)
            o_ref[:, hs] = num / jnp.maximum(jnp.abs(den), jnp.exp(-m_t))

            f_end = cf_c[L - 1:L] if d == 0 else cf_c[0:1]
            dec = f_end - cf_c + ic_c
            m_new = jnp.maximum(f_end + m_st, jnp.max(dec, axis=0, keepdims=True))
            a_prev = jnp.exp(f_end + m_st - m_new)
            ws = jnp.exp(dec - m_new)
            wk = ws * k32
            c_scr[d, h] = a_prev * c_st + lax.dot_general(wk.astype(BF16), v, (((0,), (0,)), ((), ())),
                                                          preferred_element_type=F32)
            n_scr[d, h] = a_prev * n_st + jnp.sum(wk, axis=0, keepdims=True)
            m_scr[d, h] = jnp.broadcast_to(m_new, (1, HEAD_DIM))


def _mlstm_call(q, k, p, ps, gate_b, cfg):
    t = q.shape[0]
    nc = (cfg.n_ctx + cfg.n_lat) // CHUNK
    fwd = lambda col: (lambda b, s: (_chunk_block(b, s, cfg), col))
    bwd = lambda col: (lambda b, s: (_chunk_block(b, _bwd_chunk(s, cfg), cfg), col))
    blk = lambda w, f: pl.BlockSpec((CHUNK, w), f)
    brow = jnp.zeros((1, HEAD_DIM), F32).at[0, :16].set(gate_b)
    const = lambda b, s: (0, 0)
    return pl.pallas_call(
        _mlstm_kernel,
        grid=(cfg.B, nc),
        in_specs=[blk(512, fwd(0)), blk(512, fwd(0)), blk(512, fwd(2)), blk(HEAD_DIM, fwd(0)),
                  blk(512, bwd(0)), blk(512, bwd(0)), blk(512, bwd(2)), blk(HEAD_DIM, bwd(0)),
                  pl.BlockSpec((1, HEAD_DIM), const), pl.BlockSpec((HEAD_DIM, 1), const)],
        out_specs=[blk(512, fwd(0)), blk(512, bwd(0))],
        out_shape=[jax.ShapeDtypeStruct((t, 512), F32)] * 2,
        scratch_shapes=[pltpu.VMEM((2, M_HEADS, HEAD_DIM, HEAD_DIM), F32),
                        pltpu.VMEM((2, M_HEADS, 1, HEAD_DIM), F32),
                        pltpu.VMEM((2, M_HEADS, 1, HEAD_DIM), F32)],
        compiler_params=_cparams(2),
        name="mlstm_scan",
    )(q, k, p, ps, q, k, p, ps, brow, brow.reshape(HEAD_DIM, 1))


G_QK = 256
G_V = 512


G_HEADS = 4
G_DK = 64
GLA_BLK = 16


def _cumsum_matmul(mask, x):
    hi = x.astype(BF16)
    r1 = x - hi.astype(F32)
    mid = r1.astype(BF16)
    lo = (r1 - mid.astype(F32)).astype(BF16)
    mm = lambda a: jnp.dot(mask, a, preferred_element_type=F32)
    return mm(hi) + mm(mid) + mm(lo)


def _gla_kernel(qkf, vf, lf, qkb, vb, lb, w2_ref, b2_ref, of, ob, s_scr, gl_scr):
    @pl.when(pl.program_id(1) == 0)
    def _():
        s_scr[...] = jnp.zeros_like(s_scr)

    L, C = CHUNK, GLA_BLK
    nb = L // C
    row = lax.broadcasted_iota(jnp.int32, (L, L), 0)
    col = lax.broadcasted_iota(jnp.int32, (L, L), 1)
    same_blk = (row // C) == (col // C)
    rows8 = lax.broadcasted_iota(jnp.int32, (8, G_QK), 0)
    neg_inf = jnp.float32(-jnp.inf)
    he_r = lax.broadcasted_iota(jnp.int32, (G_QK, G_V), 0) // G_DK
    he_c = lax.broadcasted_iota(jnp.int32, (G_QK, G_V), 1) // HEAD_DIM
    head_expand = (he_r == he_c).astype(BF16)

    def per_head_rows(x):
        lane_head = lax.broadcasted_iota(jnp.int32, x.shape, 1) // G_DK
        return jnp.concatenate([jnp.where(lane_head == h, x, 0.0) for h in range(G_HEADS)], axis=0).astype(BF16)

    def head_blocks(r, n):
        return jnp.concatenate([r[h * n:(h + 1) * n, h * HEAD_DIM:(h + 1) * HEAD_DIM] for h in range(G_HEADS)],
                               axis=1)

    for d, (qk_ref, v_ref, l_ref, o_ref) in enumerate(((qkf, vf, lf, of), (qkb, vb, lb, ob))):
        cs = slice(d * G_QK, (d + 1) * G_QK)
        z = jnp.dot(l_ref[...], w2_ref[:, cs], precision=HIGHEST, preferred_element_type=F32) + b2_ref[:, cs]
        log_a = _log_sigmoid(z) * (1.0 / GLA_TAU)
        tri = (col <= row) if d == 0 else (col >= row)
        g = _cumsum_matmul(tri.astype(BF16), log_a)
        gl = _cumsum_matmul(jnp.logical_and(tri, same_blk).astype(BF16), log_a)
        gl_scr[d] = gl
        q = qk_ref[:, 0:G_QK] * (G_DK ** -0.5)
        k = qk_ref[:, G_QK:2 * G_QK]
        v = v_ref[...]
        vb16 = v.astype(BF16)
        g_end = g[L - 1:L, :] if d == 0 else g[0:1, :]
        st = s_scr[d]

        r = lax.dot_general(per_head_rows(q * jnp.exp(g)), st.astype(BF16), (((1,), (1,)), ((), ())),
                            preferred_element_type=F32)
        inter = jnp.concatenate([r[h * L:(h + 1) * L] for h in range(G_HEADS)], axis=1)

        ql = q * jnp.exp(gl)
        first = lambda b: b * C + (C - 1 if d == 0 else 0)
        tot = [gl[first(b):first(b) + 1] for b in range(nb)]
        k_end = [k[b * C:(b + 1) * C] * jnp.exp(tot[b] - gl[b * C:(b + 1) * C]) for b in range(nb)]

        pieces, spans = [], []
        for i in range(L):
            bi, il = divmod(i, C)
            lo, hi = ((il // 8) * 8, C) if d == 0 else (0, (il // 8) * 8 + 8)
            edge = (lo, lo + 8) if d == 0 else (hi - 8, hi)
            k_i = qk_ref[i:i + 1, G_QK:2 * G_QK]
            g_i = gl_scr[d, i:i + 1, :]
            vis = (rows8 >= il % 8) if d == 0 else (rows8 <= il % 8)
            diff = jnp.where(vis, gl[bi * C + edge[0]:bi * C + edge[1]] - g_i, neg_inf)
            if hi - lo > 8:
                other = gl[bi * C + 8:bi * C + 16] if d == 0 else gl[bi * C:bi * C + 8]
                diff = jnp.concatenate([diff, other - g_i] if d == 0 else [other - g_i, diff], axis=0)
            pieces.append((q[bi * C + lo:bi * C + hi] * k_i) * jnp.exp(diff))
            spans.append((lo, hi))
        w_all = jnp.dot(jnp.concatenate(pieces, axis=0).astype(BF16), head_expand, preferred_element_type=F32)

        blocks = []
        w_off = 0
        for bi in range(nb):
            acc = inter[bi * C:(bi + 1) * C]
            earlier = list(range(bi)) if d == 0 else list(range(bi + 1, nb))
            if earlier:
                ks, vs = [], []
                for bj in earlier:
                    between = range(bj + 1, bi) if d == 0 else range(bi + 1, bj)
                    kj = k_end[bj]
                    if len(between):
                        kj = kj * jnp.exp(functools.reduce(jnp.add, [tot[m] for m in between]))
                    ks.append(kj)
                    vs.append(vb16[bj * C:(bj + 1) * C])
                k_cat = jnp.concatenate(ks, axis=0).astype(BF16)
                v_cat = jnp.concatenate(vs, axis=0)
                att = lax.dot_general(per_head_rows(ql[bi * C:(bi + 1) * C]), k_cat, (((1,), (1,)), ((), ())),
                                      preferred_element_type=F32)
                acc = acc + head_blocks(jnp.dot(att.astype(BF16), v_cat, preferred_element_type=F32), C)
            for il in range(C):
                i = bi * C + il
                lo, hi = spans[i]
                upd = acc[lo:hi] + w_all[w_off:w_off + hi - lo] * v_ref[i:i + 1, :]
                w_off += hi - lo
                parts = ([acc[:lo]] if lo > 0 else []) + [upd] + ([acc[hi:]] if hi < C else [])
                acc = jnp.concatenate(parts, axis=0) if len(parts) > 1 else upd
            blocks.append(acc)
        o_ref[...] = jnp.concatenate(blocks, axis=0)

        k_hat = per_head_rows(k * jnp.exp(g_end - g))
        v_rows = jnp.concatenate([vb16[:, h * HEAD_DIM:(h + 1) * HEAD_DIM] for h in range(G_HEADS)], axis=0)
        upd = lax.dot_general(v_rows, k_hat, (((0,), (0,)), ((), ())), preferred_element_type=F32)
        s_scr[d] = st * jnp.exp(g_end) + upd


def _gla_call(p, ps, w2p, b2p, cfg):
    t = p.shape[0]
    nc = (cfg.n_ctx + cfg.n_lat) // CHUNK
    fwd = lambda col: (lambda b, s: (_chunk_block(b, s, cfg), col))
    bwd = lambda col: (lambda b, s: (_chunk_block(b, _bwd_chunk(s, cfg), cfg), col))
    blk = lambda w, f: pl.BlockSpec((CHUNK, w), f)
    const = lambda b, s: (0, 0)
    return pl.pallas_call(
        _gla_kernel,
        grid=(cfg.B, nc),
        in_specs=[blk(512, fwd(4)), blk(512, fwd(5)), blk(HEAD_DIM, fwd(0)),
                  blk(512, bwd(4)), blk(512, bwd(5)), blk(HEAD_DIM, bwd(0)),
                  pl.BlockSpec((HEAD_DIM, 2 * G_QK), const), pl.BlockSpec((1, 2 * G_QK), const)],
        out_specs=[blk(512, fwd(0)), blk(512, bwd(0))],
        out_shape=[jax.ShapeDtypeStruct((t, G_V), F32)] * 2,
        scratch_shapes=[pltpu.VMEM((2, HEAD_DIM, G_QK), F32),
                        pltpu.VMEM((2, CHUNK, G_QK), F32)],
        compiler_params=_cparams(2),
        name="gla_scan",
    )(p, p, ps, p, p, ps, w2p, b2p)


ATT_SUB_ROWS = 256
ATT_TQ = 512
ATT_KEY_CHUNK = 1024


def _attn_kernel(*refs, has_lat, lam_init):
    if has_lat:
        q_ref, kc_ref, vc_ref, kl_ref, vl_ref, dl_ref, sub_ref, _, o_ref, s_scr, vo_scr = refs
        kv = ((kc_ref, vc_ref), (kl_ref, vl_ref))
    else:
        q_ref, kc_ref, vc_ref, dl_ref, sub_ref, _, o_ref, s_scr, vo_scr = refs
        kv = ((kc_ref, vc_ref),)

    @pl.when(pl.program_id(2) == 0)
    def _():
        off = 0
        for _, v_ref in kv:
            n = v_ref.shape[0]
            vo_scr[off:off + n, 0:HEAD_DIM] = v_ref[...]
            vo_scr[off:off + n, HEAD_DIM:2 * HEAD_DIM] = jnp.ones((n, HEAD_DIM), BF16)
            off += n

    n_sub = q_ref.shape[0] // ATT_SUB_ROWS
    lane = lax.broadcasted_iota(jnp.int32, (ATT_SUB_ROWS, HEAD_DIM), 1)
    zero = jnp.zeros((ATT_SUB_ROWS, HEAD_DIM), BF16)
    dl = dl_ref[...]
    lam = (jnp.exp(jnp.sum(dl[0:1] * dl[1:2], axis=-1, keepdims=True))
           - jnp.exp(jnp.sum(dl[2:3] * dl[3:4], axis=-1, keepdims=True)) + lam_init)
    row_maxes = []
    for sb in range(n_sub):
        q = q_ref[sb * ATT_SUB_ROWS:(sb + 1) * ATT_SUB_ROWS, :]
        for m, qm in enumerate((jnp.where(lane < 64, q, zero), jnp.where(lane >= 64, q, zero))):
            off, row_max = 0, None
            for k_ref, _ in kv:
                n = k_ref.shape[0]
                s = lax.dot_general(qm, k_ref[...], (((1,), (1,)), ((), ())), preferred_element_type=F32)
                s_scr[2 * sb + m, :, off:off + n] = s
                smax = jnp.max(s, axis=-1, keepdims=True)
                row_max = smax if row_max is None else jnp.maximum(row_max, smax)
                off += n
            row_maxes.append(row_max)
    n_keys = s_scr.shape[-1]
    for sb in range(n_sub):
        accs = [None, None]
        for off in range(0, n_keys, ATT_KEY_CHUNK):
            n = min(ATT_KEY_CHUNK, n_keys - off)
            for m in range(2):
                p = jnp.exp2(s_scr[2 * sb + m, :, off:off + n] - row_maxes[2 * sb + m]).astype(BF16)
                part = jnp.dot(p, vo_scr[off:off + n, :], preferred_element_type=F32)
                accs[m] = part if accs[m] is None else accs[m] + part
        outs = [a[:, 0:HEAD_DIM] / a[:, HEAD_DIM:HEAD_DIM + 1] for a in accs]
        out = outs[0] - lam * outs[1]
        ms = jnp.mean(out * out, axis=-1, keepdims=True)
        o_ref[sb * ATT_SUB_ROWS:(sb + 1) * ATT_SUB_ROWS, :] = (
            (out * lax.rsqrt(ms + NORM_EPS) * sub_ref[...]) * (1.0 - lam_init))


def _attn_call(qkv, hd_prev, d_lam, d_subln, lam_init, cfg, latent):
    t = qkv.shape[0]
    nlb = cfg.n_lat // ATT_TQ
    ctx_row0 = cfg.B * cfg.n_lat // cfg.n_ctx
    kern = functools.partial(_attn_kernel, has_lat=latent, lam_init=lam_init)
    n_keys = cfg.n_ctx + (cfg.n_lat if latent else 0)
    kc = pl.BlockSpec((cfg.n_ctx, HEAD_DIM), lambda b, h, i: (ctx_row0 + b, D_HEADS + h))
    vc = pl.BlockSpec((cfg.n_ctx, HEAD_DIM), lambda b, h, i: (ctx_row0 + b, 2 * D_HEADS + h))
    small = [pl.BlockSpec((4, 64), lambda b, h, i: (0, 0)), pl.BlockSpec((1, HEAD_DIM), lambda b, h, i: (0, 0))]
    if latent:
        tq = ATT_TQ
        grid = (cfg.B, D_HEADS, nlb)
        q_spec = pl.BlockSpec((tq, HEAD_DIM), lambda b, h, i: (b * nlb + i, h))
        kv = [kc, vc,
              pl.BlockSpec((cfg.n_lat, HEAD_DIM), lambda b, h, i: (b, D_HEADS + h)),
              pl.BlockSpec((cfg.n_lat, HEAD_DIM), lambda b, h, i: (b, 2 * D_HEADS + h))]
        out_spec = pl.BlockSpec((tq, HEAD_DIM), lambda b, h, i: (b * nlb + i, h))
    else:
        tq = cfg.n_ctx
        grid = (cfg.B, D_HEADS, 1)
        q_spec = pl.BlockSpec((tq, HEAD_DIM), lambda b, h, i: (ctx_row0 + b, h))
        kv = [kc, vc]
        out_spec = pl.BlockSpec((tq, HEAD_DIM), lambda b, h, i: (ctx_row0 + b, h))
    if hd_prev is None:
        hd_prev = jnp.zeros((8, HEAD_DIM), F32)
        aliases = {}
    else:
        aliases = {len(kv) + 3: 0}
    args = [qkv] * (1 + len(kv)) + [d_lam, d_subln.reshape(1, HEAD_DIM), hd_prev]
    return pl.pallas_call(
        kern,
        grid=grid,
        in_specs=[q_spec] + kv + small + [pl.BlockSpec(memory_space=pl.ANY)],
        out_specs=out_spec,
        out_shape=jax.ShapeDtypeStruct((t, D_HEADS * HEAD_DIM), F32),
        input_output_aliases=aliases,
        scratch_shapes=[pltpu.VMEM((2 * tq // ATT_SUB_ROWS, ATT_SUB_ROWS, n_keys), F32),
                        pltpu.VMEM((n_keys, 2 * HEAD_DIM), BF16)],
        compiler_params=_cparams(3),
        name="diff_attn_lat" if latent else "diff_attn_ctx",
    )(*args)


OUT_TILE = 256


def _group_rmsnorm(x, w, groups):
    parts = []
    for gi in range(groups):
        xs = x[:, gi * HEAD_DIM:(gi + 1) * HEAD_DIM]
        ms = jnp.mean(xs * xs, axis=-1, keepdims=True)
        parts.append(xs * lax.rsqrt(ms + NORM_EPS) * w[:, gi * HEAD_DIM:(gi + 1) * HEAD_DIM])
    return jnp.concatenate(parts, axis=-1)


def _out_kernel(x_ref, hmf, hmb, hgf, hgb, hd, mo, go, mn, gn, w_ref, nw_ref, gate_ref, o_ref):
    ym = _group_rmsnorm(hmf[...] + hmb[...], mn[...], 4) * jax.nn.sigmoid(mo[...])
    yg = _group_rmsnorm(hgf[...] + hgb[...], gn[...], 4) * _silu(go[...])
    y = jnp.concatenate([ym.astype(BF16), yg.astype(BF16), hd[...].astype(BF16)], axis=-1)
    z = jnp.dot(y, w_ref[...], preferred_element_type=F32)
    ms = jnp.mean(z * z, axis=-1, keepdims=True)
    o_ref[...] = x_ref[...] + gate_ref[0] * (z * lax.rsqrt(ms + NORM_EPS) * nw_ref[...])


def _out_call(x, hmf, hmb, hgf, hgb, hd, p, m_norm, g_norm, w_out, nw, mod, cfg, n_rows):
    d = cfg.D
    tm = OUT_TILE
    row = functools.partial(_mod_row, cfg=cfg, tm=tm)
    rt = lambda w, c: pl.BlockSpec((tm, w), lambda i: (i, c))
    const = lambda i: (0, 0)
    return pl.pallas_call(
        _out_kernel,
        grid=(n_rows // tm,),
        in_specs=[rt(d, 0), rt(512, 0), rt(512, 0), rt(512, 0), rt(512, 0), rt(1024, 0),
                  rt(512, 3), rt(512, 6),
                  pl.BlockSpec((1, 512), const), pl.BlockSpec((1, 512), const),
                  pl.BlockSpec((d, d), const), pl.BlockSpec((1, d), const),
                  pl.BlockSpec((1, 1, d), lambda i: (row(i) * 6 + 2, 0, 0))],
        out_specs=rt(d, 0),
        out_shape=jax.ShapeDtypeStruct((n_rows, d), F32),
        compiler_params=_cparams(1),
        name="out_proj",
    )(x, hmf, hmb, hgf, hgb, hd, p, p, m_norm, g_norm, w_out, nw, mod)


def _ffn_kernel(x_ref, nw_ref, shift_ref, scale_ref, wg_ref, wu_ref, wd_ref, pw_ref, gate_ref, o_ref,
                h_scr, acc_scr):
    j = pl.program_id(1)

    @pl.when(j == 0)
    def _():
        h_scr[...] = _prenorm(x_ref[...], nw_ref[...], shift_ref[0], scale_ref[0]).astype(BF16)
        acc_scr[...] = jnp.zeros_like(acc_scr)

    h = h_scr[...]
    a = jnp.dot(h, wg_ref[...], preferred_element_type=F32)
    u = jnp.dot(h, wu_ref[...], preferred_element_type=F32)
    acc_scr[...] += jnp.dot((_silu(a) * u).astype(BF16), wd_ref[...], preferred_element_type=F32)

    @pl.when(j == pl.num_programs(1) - 1)
    def _():
        z = acc_scr[...]
        ms = jnp.mean(z * z, axis=-1, keepdims=True)
        o_ref[...] = x_ref[...] + gate_ref[0] * (z * lax.rsqrt(ms + NORM_EPS) * pw_ref[...])


def _ffn_call(x, nw_pre, nw_post, mod, wg, wu, wd, cfg, n_rows, tm=512, tf=512):
    d, f = wg.shape
    row = functools.partial(_mod_row, cfg=cfg, tm=tm)
    const = lambda i, j: (0, 0)
    modspec = lambda kk: pl.BlockSpec((1, 1, d), lambda i, j: (row(i) * 6 + kk, 0, 0))
    return pl.pallas_call(
        _ffn_kernel,
        grid=(n_rows // tm, f // tf),
        in_specs=[pl.BlockSpec((tm, d), lambda i, j: (i, 0)),
                  pl.BlockSpec((1, d), const), modspec(3), modspec(4),
                  pl.BlockSpec((d, tf), lambda i, j: (0, j)),
                  pl.BlockSpec((d, tf), lambda i, j: (0, j)),
                  pl.BlockSpec((tf, d), lambda i, j: (j, 0)),
                  pl.BlockSpec((1, d), const), modspec(5)],
        out_specs=pl.BlockSpec((tm, d), lambda i, j: (i, 0)),
        out_shape=jax.ShapeDtypeStruct((n_rows, d), F32),
        scratch_shapes=[pltpu.VMEM((tm, d), BF16), pltpu.VMEM((tm, d), F32)],
        compiler_params=_cparams(2),
        name="ffn",
    )(x, nw_pre, mod, mod, wg, wu, wd, nw_post, mod)


_MIX = {}
_off = 0
for _name, _w in (("m_q", 512), ("m_k", 512), ("m_v", 512), ("m_o", 512), ("m_gates", 16),
                  ("g_q", 256), ("g_k", 256), ("g_v", 512), ("g_out", 512), ("g_lr", 32),
                  ("d_q", 1024), ("d_k", 1024), ("d_v", 1024)):
    _MIX[_name] = (_off, _w)
    _off += _w
_P_ORDER = ("m_q", "m_k", "m_v", "m_o", "g_q", "g_k", "g_v", "g_out")
_QKV_ORDER = ("d_q", "d_k", "d_v")


def _split_w_in(w_in):
    cols = lambda n: w_in[:, _MIX[n][0]:_MIX[n][0] + _MIX[n][1]]
    w_a = jnp.concatenate([cols(n) for n in _P_ORDER], axis=1).astype(BF16)
    w_b = jnp.concatenate([cols(n) for n in _QKV_ORDER], axis=1).astype(BF16)
    w_small = jnp.concatenate([cols("m_gates"), cols("g_lr"),
                               jnp.zeros((w_in.shape[0], HEAD_DIM - 48), w_in.dtype)], axis=1).astype(BF16)
    return w_a, w_b, w_small


def _rope_tables(cfg):
    rows = cfg.n_lat // GRID_W
    r = jnp.repeat(jnp.arange(rows, dtype=F32), GRID_W)
    c = jnp.tile(jnp.arange(GRID_W, dtype=F32), rows)
    half = 16
    inv_freq = ROPE_BASE ** (-jnp.arange(half, dtype=F32) / half)
    ang_r, ang_c = r[:, None] * inv_freq, c[:, None] * inv_freq
    ang = jnp.concatenate([ang_r, ang_r, ang_c, ang_c], axis=-1)
    ang = jnp.tile(ang, (cfg.B, 2))
    n_c = cfg.B * cfg.n_ctx
    cos_t = jnp.concatenate([jnp.cos(ang), jnp.ones((n_c, HEAD_DIM), F32)], axis=0)
    sin_t = jnp.concatenate([jnp.sin(ang), jnp.zeros((n_c, HEAD_DIM), F32)], axis=0)
    return cos_t, sin_t


def _layer(xt, mod, lw, lam_init, rope, cfg, need_ctx):
    d = cfg.D
    p, qkv, ps = _in_call(xt, lw["norm_mix_pre"].reshape(1, d), mod, *_split_w_in(lw["w_in"]), *rope, cfg)

    mq, mk = _conv_call(p, lw["mlstm_conv_w"], lw["mlstm_conv_b"], cfg)
    hmf, hmb = _mlstm_call(mq, mk, p, ps, lw["mlstm_gate_b"], cfg)

    w2 = lw["gla_gate_w2"]
    w2p = jnp.zeros((HEAD_DIM, 2 * G_QK), F32)
    w2p = w2p.at[16:32, 0:G_QK].set(w2[0]).at[32:48, G_QK:].set(w2[1])
    hgf, hgb = _gla_call(p, ps, w2p, lw["gla_gate_b"].reshape(1, 2 * G_QK), cfg)

    hd = _attn_call(qkv, None, lw["diff_lambda"], lw["diff_subln"], lam_init, cfg, latent=True)
    if need_ctx:
        hd = _attn_call(qkv, hd, lw["diff_lambda"], lw["diff_subln"], lam_init, cfg, latent=False)

    n_rows = cfg.T if need_ctx else cfg.TL
    xt = _out_call(xt, hmf, hmb, hgf, hgb, hd, p, lw["mlstm_norm"].reshape(1, 512),
                   lw["gla_norm"].reshape(1, 512), lw["w_out"].astype(BF16),
                   lw["norm_mix_post"].reshape(1, d), mod, cfg, n_rows)
    xt = _ffn_call(xt, lw["norm_ffn_pre"].reshape(1, d), lw["norm_ffn_post"].reshape(1, d), mod,
                   lw["w_ffn_gate"].astype(BF16), lw["w_ffn_up"].astype(BF16), lw["w_ffn_down"].astype(BF16),
                   cfg, n_rows)
    return xt


_LAYER_KEYS = ("norm_mix_pre", "norm_mix_post", "norm_ffn_pre", "norm_ffn_post", "w_in", "mlstm_conv_w",
               "mlstm_conv_b", "mlstm_gate_b", "mlstm_norm", "gla_gate_w2", "gla_gate_b", "gla_norm",
               "diff_lambda", "diff_subln", "w_out", "w_ffn_gate", "w_ffn_up", "w_ffn_down")


def kernel(x, c, ctx, c_ctx, w_mod, b_mod, norm_mix_pre, norm_mix_post, norm_ffn_pre, norm_ffn_post, w_in, mlstm_conv_w, mlstm_conv_b, mlstm_gate_b, mlstm_norm, gla_gate_w2, gla_gate_b, gla_norm, diff_lambda, diff_subln, w_out, w_ffn_gate, w_ffn_up, w_ffn_down):
    weights = dict(zip(_LAYER_KEYS, (norm_mix_pre, norm_mix_post, norm_ffn_pre, norm_ffn_post, w_in,
                                     mlstm_conv_w, mlstm_conv_b, mlstm_gate_b, mlstm_norm, gla_gate_w2,
                                     gla_gate_b, gla_norm, diff_lambda, diff_subln, w_out, w_ffn_gate,
                                     w_ffn_up, w_ffn_down)))
    b, n_lat, d = x.shape
    cfg = Cfg(B=b, n_ctx=ctx.shape[1], n_lat=n_lat, D=d, F=w_ffn_gate.shape[-1])
    depth = w_mod.shape[0]
    c8 = jnp.zeros((8, d), F32).at[0].set(c_ctx).at[1:1 + b].set(c)
    mods = _mod_call(c8, w_mod, b_mod).reshape(depth, 8 * 6, 1, d)
    rope = _rope_tables(cfg)
    xt = jnp.concatenate([x.reshape(b * n_lat, d), ctx.reshape(b * ctx.shape[1], d)], axis=0)
    for layer in range(depth):
        lw = {k: v[layer] for k, v in weights.items()}
        lam_init = 0.8 - 0.6 * math.exp(-0.3 * layer)
        xt = _layer(xt, mods[layer], lw, lam_init, rope, cfg, need_ctx=layer < depth - 1)
    return xt.reshape(b, n_lat, d)
```

```python
import dataclasses
import functools
import math

import jax
import jax.numpy as jnp
from jax import lax
from jax.experimental import pallas as pl
from jax.experimental.pallas import tpu as pltpu

F32 = jnp.float32
BF16 = jnp.bfloat16
NORM_EPS = 1e-6
CHUNK = 64
GRID_W = 64
ROPE_BASE = 10000.0
GLA_TAU = 16.0
HEAD_DIM = 128
VMEM_LIMIT_BYTES = 56 * 1024 * 1024
HIGHEST = lax.Precision.HIGHEST


@dataclasses.dataclass(frozen=True)
class Cfg:
    B: int = 2
    n_ctx: int = 256
    n_lat: int = 4096
    D: int = 2048
    F: int = 5632

    @property
    def T(self):
        return self.B * (self.n_ctx + self.n_lat)

    @property
    def TL(self):
        return self.B * self.n_lat


def _cparams(n_axes):
    return pltpu.CompilerParams(dimension_semantics=("arbitrary",) * n_axes,
                                vmem_limit_bytes=VMEM_LIMIT_BYTES)


def _mod_row(i, cfg, tm):
    lt = cfg.n_lat // tm
    return jnp.where(i < cfg.B * lt, 1 + i // lt, 0)


def _chunk_block(b, c, cfg):
    ncc, ncl = cfg.n_ctx // CHUNK, cfg.n_lat // CHUNK
    return jnp.where(c < ncc, cfg.B * ncl + b * ncc + c, b * ncl + (c - ncc))


def _bwd_chunk(s, cfg):
    ncc, ncl = cfg.n_ctx // CHUNK, cfg.n_lat // CHUNK
    return jnp.where(s < ncc, ncc - 1 - s, ncc + ncl - 1 - (s - ncc))


def _log_sigmoid(x):
    return jnp.minimum(x, 0.0) - jnp.log1p(jnp.exp(-jnp.abs(x)))


def _silu(x):
    return x * jax.nn.sigmoid(x)


def _mod_kernel(c_ref, w_ref, b_ref, o_ref):
    s = _silu(c_ref[...]).astype(BF16)
    o_ref[0] = jnp.dot(s, w_ref[0].astype(BF16), preferred_element_type=F32) + b_ref[0]


def _mod_call(c8, w_mod, b_mod, tn=1024):
    depth, d, n = w_mod.shape
    return pl.pallas_call(
        _mod_kernel,
        grid=(depth, n // tn),
        in_specs=[pl.BlockSpec((8, d), lambda l, j: (0, 0)),
                  pl.BlockSpec((1, d, tn), lambda l, j: (l, 0, j)),
                  pl.BlockSpec((1, 1, tn), lambda l, j: (l, 0, j))],
        out_specs=pl.BlockSpec((1, 8, tn), lambda l, j: (l, 0, j)),
        out_shape=jax.ShapeDtypeStruct((depth, 8, n), F32),
        compiler_params=_cparams(2),
        name="adaln_mod",
    )(c8, w_mod, b_mod.reshape(depth, 1, n))


def _prenorm(x, nw, shift, scale):
    ms = jnp.mean(x * x, axis=-1, keepdims=True)
    return (x * lax.rsqrt(ms + NORM_EPS) * nw) * (1.0 + scale) + shift


D_HEADS = 8
P_COLS = 3584
QKV_COLS = 3072
IN_TILE = 256


def _in_kernel(x_ref, nw_ref, shift_ref, scale_ref, wa_ref, wb_ref, ws_ref, cos_ref, sin_ref,
               p_ref, qkv_ref, ps_ref):
    h = _prenorm(x_ref[...], nw_ref[...], shift_ref[0], scale_ref[0]).astype(BF16)
    ps_ref[...] = jnp.dot(h, ws_ref[...], preferred_element_type=F32)
    p_ref[...] = jnp.dot(h, wa_ref[...], preferred_element_type=F32)
    qkv = jnp.dot(h, wb_ref[...], preferred_element_type=F32)
    lane = lax.broadcasted_iota(jnp.int32, (IN_TILE, HEAD_DIM), 1)
    low = (lane % 32) < 16
    cos, sin = cos_ref[...], sin_ref[...]
    q_scale = 64 ** -0.5 * math.log2(math.e)
    for s in range(2 * D_HEADS):
        cs = slice(s * HEAD_DIM, (s + 1) * HEAD_DIM)
        x = qkv[:, cs]
        rot = jnp.where(low, -pltpu.roll(x, HEAD_DIM - 16, 1), pltpu.roll(x, 16, 1))
        y = x * cos + rot * sin
        qkv_ref[:, cs] = ((y * q_scale) if s < D_HEADS else y).astype(BF16)
    vs = slice(2 * D_HEADS * HEAD_DIM, QKV_COLS)
    qkv_ref[:, vs] = qkv[:, vs].astype(BF16)


def _in_call(x, nw, mod, w_a, w_b, w_small, cos_t, sin_t, cfg):
    t, d = x.shape
    tm = IN_TILE
    row = functools.partial(_mod_row, cfg=cfg, tm=tm)
    resident = lambda shape: pl.BlockSpec(shape, lambda i: (0, 0), pipeline_mode=pl.Buffered(1))
    return pl.pallas_call(
        _in_kernel,
        grid=(t // tm,),
        in_specs=[pl.BlockSpec((tm, d), lambda i: (i, 0)),
                  resident((1, d)),
                  pl.BlockSpec((1, 1, d), lambda i: (row(i) * 6 + 0, 0, 0)),
                  pl.BlockSpec((1, 1, d), lambda i: (row(i) * 6 + 1, 0, 0)),
                  resident((d, P_COLS)), resident((d, QKV_COLS)), resident((d, HEAD_DIM)),
                  pl.BlockSpec((tm, HEAD_DIM), lambda i: (i, 0)),
                  pl.BlockSpec((tm, HEAD_DIM), lambda i: (i, 0))],
        out_specs=[pl.BlockSpec((tm, P_COLS), lambda i: (i, 0)),
                   pl.BlockSpec((tm, QKV_COLS), lambda i: (i, 0)),
                   pl.BlockSpec((tm, HEAD_DIM), lambda i: (i, 0))],
        out_shape=[jax.ShapeDtypeStruct((t, P_COLS), F32), jax.ShapeDtypeStruct((t, QKV_COLS), BF16),
                   jax.ShapeDtypeStruct((t, HEAD_DIM), F32)],
        compiler_params=_cparams(1),
        name="in_proj",
    )(x, nw, mod, mod, w_a, w_b, w_small, cos_t, sin_t)


CONV_TILE = 256


def _conv_kernel(xq_ref, xk_ref, pq_ref, pk_ref, nq_ref, nk_ref, w_ref, b_ref, q_ref, k_ref, *,
                 starts, ends):
    i = pl.program_id(0)
    is_start = functools.reduce(jnp.logical_or, [i == s for s in starts])
    is_end = functools.reduce(jnp.logical_or, [i == s for s in ends])
    tr = CONV_TILE
    rows = lax.broadcasted_iota(jnp.int32, (tr, 512), 0)

    def conv(x_ref, p_ref, n_ref, half):
        x = x_ref[...]
        prev_row = jnp.where(is_start, 0.0, p_ref[7:8, :])
        next_row = jnp.where(is_end, 0.0, n_ref[0:1, :])
        xp = jnp.where(rows == 0, prev_row, pltpu.roll(x, 1, 0))
        xn = jnp.where(rows == tr - 1, next_row, pltpu.roll(x, tr - 1, 0))
        lo, hi = half * 512, (half + 1) * 512
        y = xp * w_ref[0:1, lo:hi] + x * w_ref[1:2, lo:hi] + xn * w_ref[2:3, lo:hi] + b_ref[:, lo:hi]
        return _silu(y)

    q_ref[...] = conv(xq_ref, pq_ref, nq_ref, 0) * (HEAD_DIM ** -0.5)
    k_ref[...] = conv(xk_ref, pk_ref, nk_ref, 1)


def _conv_call(p, conv_w, conv_b, cfg):
    t = p.shape[0]
    tr = CONV_TILE
    nt = t // tr
    lt, ct = cfg.n_lat // tr, cfg.n_ctx // tr
    seg_first = [b * lt for b in range(cfg.B)] + [cfg.B * lt + b * ct for b in range(cfg.B)]
    seg_last = [b * lt + lt - 1 for b in range(cfg.B)] + [cfg.B * lt + b * ct + ct - 1 for b in range(cfg.B)]
    r8 = tr // 8
    last8 = t // 8 - 1
    kern = functools.partial(_conv_kernel, starts=tuple(seg_first), ends=tuple(seg_last))
    prev = lambda c: (lambda i: (jnp.maximum(i * r8 - 1, 0), c))
    nxt = lambda c: (lambda i: (jnp.minimum((i + 1) * r8, last8), c))
    return pl.pallas_call(
        kern,
        grid=(nt,),
        in_specs=[pl.BlockSpec((tr, 512), lambda i: (i, 0)),
                  pl.BlockSpec((tr, 512), lambda i: (i, 1)),
                  pl.BlockSpec((8, 512), prev(0)), pl.BlockSpec((8, 512), prev(1)),
                  pl.BlockSpec((8, 512), nxt(0)), pl.BlockSpec((8, 512), nxt(1)),
                  pl.BlockSpec((3, 1024), lambda i: (0, 0)),
                  pl.BlockSpec((1, 1024), lambda i: (0, 0))],
        out_specs=[pl.BlockSpec((tr, 512), lambda i: (i, 0)), pl.BlockSpec((tr, 512), lambda i: (i, 0))],
        out_shape=[jax.ShapeDtypeStruct((t, 512), F32)] * 2,
        compiler_params=_cparams(1),
        name="mlstm_conv",
    )(p, p, p, p, p, p, conv_w, conv_b.reshape(1, 1024))


M_HEADS = 4


def _split3(x):
    hi = x.astype(BF16)
    r1 = x - hi.astype(F32)
    mid = r1.astype(BF16)
    return hi, mid, (r1 - mid.astype(F32)).astype(BF16)


def _cumsum_matmul(mask, x):
    return sum(jnp.dot(mask, part, preferred_element_type=F32) for part in _split3(x))


def _mlstm_kernel(qf, kf, vf, gf, qb, kb, vb, gb, brow, bcol, of, ob, c_scr, m_scr):
    @pl.when(pl.program_id(1) == 0)
    def _():
        c_scr[...] = jnp.zeros_like(c_scr)
        m_scr[...] = jnp.zeros_like(m_scr)

    L = CHUNK
    row = lax.broadcasted_iota(jnp.int32, (L, L), 0)
    col = lax.broadcasted_iota(jnp.int32, (L, L), 1)
    neg_inf = jnp.float32(-jnp.inf)

    gcol = jnp.concatenate([gf[...], gb[...]], axis=0) + brow[...]
    grow = jnp.concatenate([gf[...], gb[...]], axis=0).T + bcol[...]
    r8 = lax.broadcasted_iota(jnp.int32, (8, 2 * L), 0)
    l8 = lax.broadcasted_iota(jnp.int32, (8, 2 * L), 1)
    fwd_row = r8 < M_HEADS
    own = (r8 // M_HEADS) == (l8 // L)
    ig = jnp.concatenate([grow[0:4], grow[8:12]], axis=0)
    logf = _log_sigmoid(jnp.concatenate([grow[4:8], grow[12:16]], axis=0))
    tl = lax.broadcasted_iota(jnp.int32, (2 * L, 2 * L), 0)
    ti = lax.broadcasted_iota(jnp.int32, (2 * L, 2 * L), 1)
    same_half = (tl // L) == (ti // L)
    scan_rows = jnp.logical_and(same_half, jnp.where(ti < L, tl - ti, ti - tl) <= 0)
    scan_cols = jnp.logical_and(same_half, jnp.where(tl < L, ti - tl, tl - ti) <= 0)
    cf = sum(jnp.dot(part, scan_rows.astype(BF16), preferred_element_type=F32) for part in _split3(logf))
    u = jnp.where(own, ig - cf, neg_inf)
    m_st = m_scr[...]
    end_lane = jnp.where(fwd_row, L - 1, L)
    f_end = jnp.broadcast_to(jnp.sum(jnp.where(l8 == end_lane, cf, 0.0), axis=-1, keepdims=True),
                             (8, 2 * L))
    dec = jnp.where(own, f_end - cf + ig, neg_inf)
    m_new = jnp.maximum(f_end + m_st, jnp.max(dec, axis=-1, keepdims=True))
    a_prev = jnp.exp(f_end + m_st - m_new)
    m_scr[...] = m_new
    u_keys = jnp.where(fwd_row, u, pltpu.roll(u, L, 1))
    cum_col = _cumsum_matmul(scan_cols.astype(BF16), _log_sigmoid(gcol))

    rep = lambda colv: jnp.broadcast_to(colv, (L, HEAD_DIM))
    dir_refs = ((qf, kf, vf, of), (qb, kb, vb, ob))
    pairs = [(d, h) for d in range(2) for h in range(M_HEADS)]
    work = []
    for d, h in pairs:
        q_ref, k_ref, v_ref, _ = dir_refs[d]
        hs = slice(h * HEAD_DIM, (h + 1) * HEAD_DIM)
        q = q_ref[:, hs].astype(BF16)
        k32 = k_ref[:, hs]
        v1 = jnp.concatenate([v_ref[:, hs].astype(BF16), jnp.ones((L, HEAD_DIM), BF16)], axis=1)
        cn = c_scr[d, h]
        qk = lax.dot_general(q, k32.astype(BF16), (((1,), (1,)), ((), ())), preferred_element_type=F32)
        r_state = jnp.dot(q, cn.astype(BF16), preferred_element_type=F32)
        work.append((k32, v1, cn, qk, r_state))
    for (d, h), (k32, v1, cn, qk, r_state) in zip(pairs, work):
        o_ref = dir_refs[d][3]
        valid = (col <= row) if d == 0 else (col >= row)
        ts = slice(d * L, (d + 1) * L)
        r = d * M_HEADS + h
        ci, cfc = d * 8 + h, d * 8 + 4 + h
        hs = slice(h * HEAD_DIM, (h + 1) * HEAD_DIM)
        u_tile = jnp.where(valid, u_keys[r:r + 1, 0:L], neg_inf)
        m_rep = rep(jnp.maximum(jnp.max(u_tile, axis=-1, keepdims=True), m_st[r:r + 1, 0:1]))
        cf_rep = rep(cum_col[ts, cfc:cfc + 1])
        ig_rep = rep(gcol[ts, ci:ci + 1])
        w_inter = jnp.exp(m_st[r:r + 1] - m_rep)
        e = jnp.exp(u_tile - m_rep[:, 0:L])
        r_chunk = jnp.dot((qk * e).astype(BF16), v1, preferred_element_type=F32)
        num = w_inter * r_state[:, 0:HEAD_DIM] + r_chunk[:, 0:HEAD_DIM]
        den = w_inter * r_state[:, HEAD_DIM:] + r_chunk[:, HEAD_DIM:]
        o_ref[:, hs] = num / jnp.maximum(jnp.abs(den), jnp.exp(-(cf_rep + m_rep)))

        ws = jnp.exp(f_end[r:r + 1] - cf_rep + ig_rep - m_new[r:r + 1])
        wk = (ws * k32).astype(BF16)
        decay = jnp.concatenate([a_prev[r:r + 1], a_prev[r:r + 1]], axis=1)
        c_scr[d, h] = decay * cn + lax.dot_general(wk, v1, (((0,), (0,)), ((), ())),
                                                   preferred_element_type=F32)


def _mlstm_call(q, k, p, ps, gate_b, cfg):
    t = q.shape[0]
    nc = (cfg.n_ctx + cfg.n_lat) // CHUNK
    fwd = lambda col: (lambda b, s: (_chunk_block(b, s, cfg), col))
    bwd = lambda col: (lambda b, s: (_chunk_block(b, _bwd_chunk(s, cfg), cfg), col))
    blk = lambda w, f: pl.BlockSpec((CHUNK, w), f)
    brow = jnp.zeros((1, HEAD_DIM), F32).at[0, :16].set(gate_b)
    const = lambda b, s: (0, 0)
    return pl.pallas_call(
        _mlstm_kernel,
        grid=(cfg.B, nc),
        in_specs=[blk(512, fwd(0)), blk(512, fwd(0)), blk(512, fwd(2)), blk(HEAD_DIM, fwd(0)),
                  blk(512, bwd(0)), blk(512, bwd(0)), blk(512, bwd(2)), blk(HEAD_DIM, bwd(0)),
                  pl.BlockSpec((1, HEAD_DIM), const), pl.BlockSpec((HEAD_DIM, 1), const)],
        out_specs=[blk(512, fwd(0)), blk(512, bwd(0))],
        out_shape=[jax.ShapeDtypeStruct((t, 512), F32)] * 2,
        scratch_shapes=[pltpu.VMEM((2, M_HEADS, HEAD_DIM, 2 * HEAD_DIM), F32),
                        pltpu.VMEM((2 * M_HEADS, 2 * CHUNK), F32)],
        compiler_params=_cparams(2),
        name="mlstm_scan",
    )(q, k, p, ps, q, k, p, ps, brow, brow.reshape(HEAD_DIM, 1))


G_QK = 256
G_V = 512


G_HEADS = 4
G_DK = 64
GLA_BLK = 16


def _gla_kernel(qkf, vf, lf, qkb, vb, lb, w2_ref, b2_ref, of, ob, s_scr, gl_scr):
    @pl.when(pl.program_id(1) == 0)
    def _():
        s_scr[...] = jnp.zeros_like(s_scr)

    L, C = CHUNK, GLA_BLK
    nb = L // C
    row = lax.broadcasted_iota(jnp.int32, (L, L), 0)
    col = lax.broadcasted_iota(jnp.int32, (L, L), 1)
    same_blk = (row // C) == (col // C)
    rows8 = lax.broadcasted_iota(jnp.int32, (8, G_QK), 0)
    neg_inf = jnp.float32(-jnp.inf)
    he_r = lax.broadcasted_iota(jnp.int32, (G_QK, G_V), 0) // G_DK
    he_c = lax.broadcasted_iota(jnp.int32, (G_QK, G_V), 1) // HEAD_DIM
    head_expand = (he_r == he_c).astype(BF16)

    def per_head_rows(x):
        lane_head = lax.broadcasted_iota(jnp.int32, x.shape, 1) // G_DK
        return jnp.concatenate([jnp.where(lane_head == h, x, 0.0) for h in range(G_HEADS)], axis=0).astype(BF16)

    def head_blocks(r, n):
        return jnp.concatenate([r[h * n:(h + 1) * n, h * HEAD_DIM:(h + 1) * HEAD_DIM] for h in range(G_HEADS)],
                               axis=1)

    def direction(d, qk_ref, v_ref, l_ref, o_ref):
        cs = slice(d * G_QK, (d + 1) * G_QK)
        z = jnp.dot(l_ref[...], w2_ref[:, cs], precision=HIGHEST, preferred_element_type=F32) + b2_ref[:, cs]
        log_a = _log_sigmoid(z) * (1.0 / GLA_TAU)
        tri = (col <= row) if d == 0 else (col >= row)
        g = _cumsum_matmul(tri.astype(BF16), log_a)
        gl = _cumsum_matmul(jnp.logical_and(tri, same_blk).astype(BF16), log_a)
        gl_scr[d] = gl
        q = qk_ref[:, 0:G_QK] * (G_DK ** -0.5)
        k = qk_ref[:, G_QK:2 * G_QK]
        v = v_ref[...]
        vb16 = v.astype(BF16)
        g_end = g[L - 1:L, :] if d == 0 else g[0:1, :]
        st = s_scr[d]

        r = lax.dot_general(per_head_rows(q * jnp.exp(g)), st.astype(BF16), (((1,), (1,)), ((), ())),
                            preferred_element_type=F32)
        inter = jnp.concatenate([r[h * L:(h + 1) * L] for h in range(G_HEADS)], axis=1)
        yield

        ql = q * jnp.exp(gl)
        first = lambda b: b * C + (C - 1 if d == 0 else 0)
        tot = [gl[first(b):first(b) + 1] for b in range(nb)]
        k_end = [k[b * C:(b + 1) * C] * jnp.exp(tot[b] - gl[b * C:(b + 1) * C]) for b in range(nb)]

        atts, v_cats = {}, {}
        for bi in range(nb):
            earlier = list(range(bi)) if d == 0 else list(range(bi + 1, nb))
            if not earlier:
                continue
            ks, vs = [], []
            for bj in earlier:
                between = range(bj + 1, bi) if d == 0 else range(bi + 1, bj)
                kj = k_end[bj]
                if len(between):
                    kj = kj * jnp.exp(functools.reduce(jnp.add, [tot[m] for m in between]))
                ks.append(kj)
                vs.append(vb16[bj * C:(bj + 1) * C])
            v_cats[bi] = jnp.concatenate(vs, axis=0)
            atts[bi] = lax.dot_general(per_head_rows(ql[bi * C:(bi + 1) * C]),
                                       jnp.concatenate(ks, axis=0).astype(BF16), (((1,), (1,)), ((), ())),
                                       preferred_element_type=F32)

        k_hat = per_head_rows(k * jnp.exp(g_end - g))
        v_rows = jnp.concatenate([vb16[:, h * HEAD_DIM:(h + 1) * HEAD_DIM] for h in range(G_HEADS)], axis=0)
        s_new = st * jnp.exp(g_end) + lax.dot_general(v_rows, k_hat, (((0,), (0,)), ((), ())),
                                                      preferred_element_type=F32)
        yield

        pieces, spans = [], []
        for i in range(L):
            bi, il = divmod(i, C)
            lo, hi = ((il // 8) * 8, C) if d == 0 else (0, (il // 8) * 8 + 8)
            edge = (lo, lo + 8) if d == 0 else (hi - 8, hi)
            k_i = qk_ref[i:i + 1, G_QK:2 * G_QK]
            g_i = gl_scr[d, i:i + 1, :]
            vis = (rows8 >= il % 8) if d == 0 else (rows8 <= il % 8)
            diff = jnp.where(vis, gl[bi * C + edge[0]:bi * C + edge[1]] - g_i, neg_inf)
            if hi - lo > 8:
                other = gl[bi * C + 8:bi * C + 16] if d == 0 else gl[bi * C:bi * C + 8]
                diff = jnp.concatenate([diff, other - g_i] if d == 0 else [other - g_i, diff], axis=0)
            pieces.append((q[bi * C + lo:bi * C + hi] * k_i) * jnp.exp(diff))
            spans.append((lo, hi))
        w_all = jnp.dot(jnp.concatenate(pieces, axis=0).astype(BF16), head_expand, preferred_element_type=F32)
        yield
        cross = {bi: head_blocks(jnp.dot(att.astype(BF16), v_cats[bi], preferred_element_type=F32), C)
                 for bi, att in atts.items()}
        yield

        blocks = []
        w_off = 0
        for bi in range(nb):
            acc = inter[bi * C:(bi + 1) * C]
            if bi in cross:
                acc = acc + cross[bi]
            for il in range(C):
                i = bi * C + il
                lo, hi = spans[i]
                upd = acc[lo:hi] + w_all[w_off:w_off + hi - lo] * v_ref[i:i + 1, :]
                w_off += hi - lo
                parts = ([acc[:lo]] if lo > 0 else []) + [upd] + ([acc[hi:]] if hi < C else [])
                acc = jnp.concatenate(parts, axis=0) if len(parts) > 1 else upd
            blocks.append(acc)
        o_ref[...] = jnp.concatenate(blocks, axis=0)
        s_scr[d] = s_new

    stages = [direction(0, qkf, vf, lf, of), direction(1, qkb, vb, lb, ob)]
    for _ in range(5):
        for stage in stages:
            next(stage, None)


def _gla_call(p, ps, w2p, b2p, cfg):
    t = p.shape[0]
    nc = (cfg.n_ctx + cfg.n_lat) // CHUNK
    fwd = lambda col: (lambda b, s: (_chunk_block(b, s, cfg), col))
    bwd = lambda col: (lambda b, s: (_chunk_block(b, _bwd_chunk(s, cfg), cfg), col))
    blk = lambda w, f: pl.BlockSpec((CHUNK, w), f)
    const = lambda b, s: (0, 0)
    return pl.pallas_call(
        _gla_kernel,
        grid=(cfg.B, nc),
        in_specs=[blk(512, fwd(4)), blk(512, fwd(5)), blk(HEAD_DIM, fwd(0)),
                  blk(512, bwd(4)), blk(512, bwd(5)), blk(HEAD_DIM, bwd(0)),
                  pl.BlockSpec((HEAD_DIM, 2 * G_QK), const), pl.BlockSpec((1, 2 * G_QK), const)],
        out_specs=[blk(512, fwd(0)), blk(512, bwd(0))],
        out_shape=[jax.ShapeDtypeStruct((t, G_V), F32)] * 2,
        scratch_shapes=[pltpu.VMEM((2, HEAD_DIM, G_QK), F32),
                        pltpu.VMEM((2, CHUNK, G_QK), F32)],
        compiler_params=_cparams(2),
        name="gla_scan",
    )(p, p, ps, p, p, ps, w2p, b2p)


ATT_SUB_ROWS = 256
ATT_TQ = 512
ATT_KEY_CHUNK = 1024


def _attn_kernel(*refs, has_lat, lam_init):
    if has_lat:
        q_ref, kc_ref, vc_ref, kl_ref, vl_ref, dl_ref, sub_ref, _, o_ref, s_scr, vo_scr = refs
        kv = ((kc_ref, vc_ref), (kl_ref, vl_ref))
    else:
        q_ref, kc_ref, vc_ref, dl_ref, sub_ref, _, o_ref, s_scr, vo_scr = refs
        kv = ((kc_ref, vc_ref),)

    @pl.when(pl.program_id(2) == 0)
    def _():
        off = 0
        for _, v_ref in kv:
            n = v_ref.shape[0]
            vo_scr[off:off + n, 0:HEAD_DIM] = v_ref[...]
            vo_scr[off:off + n, HEAD_DIM:2 * HEAD_DIM] = jnp.ones((n, HEAD_DIM), BF16)
            off += n

    n_sub = q_ref.shape[0] // ATT_SUB_ROWS
    lane = lax.broadcasted_iota(jnp.int32, (ATT_SUB_ROWS, HEAD_DIM), 1)
    zero = jnp.zeros((ATT_SUB_ROWS, HEAD_DIM), BF16)
    dl = dl_ref[...]
    lam = (jnp.exp(jnp.sum(dl[0:1] * dl[1:2], axis=-1, keepdims=True))
           - jnp.exp(jnp.sum(dl[2:3] * dl[3:4], axis=-1, keepdims=True)) + lam_init)
    row_maxes = []
    for sb in range(n_sub):
        q = q_ref[sb * ATT_SUB_ROWS:(sb + 1) * ATT_SUB_ROWS, :]
        for m, qm in enumerate((jnp.where(lane < 64, q, zero), jnp.where(lane >= 64, q, zero))):
            off, row_max = 0, None
            for k_ref, _ in kv:
                n = k_ref.shape[0]
                s = lax.dot_general(qm, k_ref[...], (((1,), (1,)), ((), ())), preferred_element_type=F32)
                s_scr[2 * sb + m, :, off:off + n] = s
                smax = jnp.max(s, axis=-1, keepdims=True)
                row_max = smax if row_max is None else jnp.maximum(row_max, smax)
                off += n
            row_maxes.append(row_max)
    n_keys = s_scr.shape[-1]
    for sb in range(n_sub):
        accs = [None, None]
        for off in range(0, n_keys, ATT_KEY_CHUNK):
            n = min(ATT_KEY_CHUNK, n_keys - off)
            for m in range(2):
                p = jnp.exp2(s_scr[2 * sb + m, :, off:off + n] - row_maxes[2 * sb + m]).astype(BF16)
                part = jnp.dot(p, vo_scr[off:off + n, :], preferred_element_type=F32)
                accs[m] = part if accs[m] is None else accs[m] + part
        outs = [a[:, 0:HEAD_DIM] / a[:, HEAD_DIM:HEAD_DIM + 1] for a in accs]
        out = outs[0] - lam * outs[1]
        ms = jnp.mean(out * out, axis=-1, keepdims=True)
        o_ref[sb * ATT_SUB_ROWS:(sb + 1) * ATT_SUB_ROWS, :] = (
            (out * lax.rsqrt(ms + NORM_EPS) * sub_ref[...]) * (1.0 - lam_init))


def _attn_call(qkv, hd_prev, d_lam, d_subln, lam_init, cfg, latent):
    t = qkv.shape[0]
    nlb = cfg.n_lat // ATT_TQ
    ctx_row0 = cfg.B * cfg.n_lat // cfg.n_ctx
    kern = functools.partial(_attn_kernel, has_lat=latent, lam_init=lam_init)
    n_keys = cfg.n_ctx + (cfg.n_lat if latent else 0)
    kc = pl.BlockSpec((cfg.n_ctx, HEAD_DIM), lambda b, h, i: (ctx_row0 + b, D_HEADS + h))
    vc = pl.BlockSpec((cfg.n_ctx, HEAD_DIM), lambda b, h, i: (ctx_row0 + b, 2 * D_HEADS + h))
    small = [pl.BlockSpec((4, 64), lambda b, h, i: (0, 0)), pl.BlockSpec((1, HEAD_DIM), lambda b, h, i: (0, 0))]
    if latent:
        tq = ATT_TQ
        grid = (cfg.B, D_HEADS, nlb)
        q_spec = pl.BlockSpec((tq, HEAD_DIM), lambda b, h, i: (b * nlb + i, h))
        kv = [kc, vc,
              pl.BlockSpec((cfg.n_lat, HEAD_DIM), lambda b, h, i: (b, D_HEADS + h)),
              pl.BlockSpec((cfg.n_lat, HEAD_DIM), lambda b, h, i: (b, 2 * D_HEADS + h))]
        out_spec = pl.BlockSpec((tq, HEAD_DIM), lambda b, h, i: (b * nlb + i, h))
    else:
        tq = cfg.n_ctx
        grid = (cfg.B, D_HEADS, 1)
        q_spec = pl.BlockSpec((tq, HEAD_DIM), lambda b, h, i: (ctx_row0 + b, h))
        kv = [kc, vc]
        out_spec = pl.BlockSpec((tq, HEAD_DIM), lambda b, h, i: (ctx_row0 + b, h))
    if hd_prev is None:
        hd_prev = jnp.zeros((8, HEAD_DIM), F32)
        aliases = {}
    else:
        aliases = {len(kv) + 3: 0}
    args = [qkv] * (1 + len(kv)) + [d_lam, d_subln.reshape(1, HEAD_DIM), hd_prev]
    return pl.pallas_call(
        kern,
        grid=grid,
        in_specs=[q_spec] + kv + small + [pl.BlockSpec(memory_space=pl.ANY)],
        out_specs=out_spec,
        out_shape=jax.ShapeDtypeStruct((t, D_HEADS * HEAD_DIM), F32),
        input_output_aliases=aliases,
        scratch_shapes=[pltpu.VMEM((2 * tq // ATT_SUB_ROWS, ATT_SUB_ROWS, n_keys), F32),
                        pltpu.VMEM((n_keys, 2 * HEAD_DIM), BF16)],
        compiler_params=_cparams(3),
        name="diff_attn_lat" if latent else "diff_attn_ctx",
    )(*args)


OUT_TILE = 256


def _group_rmsnorm(x, w, groups):
    parts = []
    for gi in range(groups):
        xs = x[:, gi * HEAD_DIM:(gi + 1) * HEAD_DIM]
        ms = jnp.mean(xs * xs, axis=-1, keepdims=True)
        parts.append(xs * lax.rsqrt(ms + NORM_EPS) * w[:, gi * HEAD_DIM:(gi + 1) * HEAD_DIM])
    return jnp.concatenate(parts, axis=-1)


def _out_kernel(x_ref, hmf, hmb, hgf, hgb, hd, mo, go, mn, gn, w_ref, nw_ref, gate_ref, o_ref):
    ym = _group_rmsnorm(hmf[...] + hmb[...], mn[...], 4) * jax.nn.sigmoid(mo[...])
    yg = _group_rmsnorm(hgf[...] + hgb[...], gn[...], 4) * _silu(go[...])
    y = jnp.concatenate([ym.astype(BF16), yg.astype(BF16), hd[...].astype(BF16)], axis=-1)
    z = jnp.dot(y, w_ref[...], preferred_element_type=F32)
    ms = jnp.mean(z * z, axis=-1, keepdims=True)
    o_ref[...] = x_ref[...] + gate_ref[0] * (z * lax.rsqrt(ms + NORM_EPS) * nw_ref[...])


def _out_call(x, hmf, hmb, hgf, hgb, hd, p, m_norm, g_norm, w_out, nw, mod, cfg, n_rows):
    d = cfg.D
    tm = OUT_TILE
    row = functools.partial(_mod_row, cfg=cfg, tm=tm)
    rt = lambda w, c: pl.BlockSpec((tm, w), lambda i: (i, c))
    const = lambda i: (0, 0)
    return pl.pallas_call(
        _out_kernel,
        grid=(n_rows // tm,),
        in_specs=[rt(d, 0), rt(512, 0), rt(512, 0), rt(512, 0), rt(512, 0), rt(1024, 0),
                  rt(512, 3), rt(512, 6),
                  pl.BlockSpec((1, 512), const), pl.BlockSpec((1, 512), const),
                  pl.BlockSpec((d, d), const), pl.BlockSpec((1, d), const),
                  pl.BlockSpec((1, 1, d), lambda i: (row(i) * 6 + 2, 0, 0))],
        out_specs=rt(d, 0),
        out_shape=jax.ShapeDtypeStruct((n_rows, d), F32),
        compiler_params=_cparams(1),
        name="out_proj",
    )(x, hmf, hmb, hgf, hgb, hd, p, p, m_norm, g_norm, w_out, nw, mod)


def _ffn_kernel(x_ref, nw_ref, shift_ref, scale_ref, wg_ref, wu_ref, wd_ref, pw_ref, gate_ref, o_ref,
                h_scr, acc_scr):
    j = pl.program_id(1)

    @pl.when(j == 0)
    def _():
        h_scr[...] = _prenorm(x_ref[...], nw_ref[...], shift_ref[0], scale_ref[0]).astype(BF16)
        acc_scr[...] = jnp.zeros_like(acc_scr)

    h = h_scr[...]
    a = jnp.dot(h, wg_ref[...], preferred_element_type=F32)
    u = jnp.dot(h, wu_ref[...], preferred_element_type=F32)
    acc_scr[...] += jnp.dot((_silu(a) * u).astype(BF16), wd_ref[...], preferred_element_type=F32)

    @pl.when(j == pl.num_programs(1) - 1)
    def _():
        z = acc_scr[...]
        ms = jnp.mean(z * z, axis=-1, keepdims=True)
        o_ref[...] = x_ref[...] + gate_ref[0] * (z * lax.rsqrt(ms + NORM_EPS) * pw_ref[...])


def _ffn_call(x, nw_pre, nw_post, mod, wg, wu, wd, cfg, n_rows, tm=512, tf=512):
    d, f = wg.shape
    row = functools.partial(_mod_row, cfg=cfg, tm=tm)
    const = lambda i, j: (0, 0)
    modspec = lambda kk: pl.BlockSpec((1, 1, d), lambda i, j: (row(i) * 6 + kk, 0, 0))
    return pl.pallas_call(
        _ffn_kernel,
        grid=(n_rows // tm, f // tf),
        in_specs=[pl.BlockSpec((tm, d), lambda i, j: (i, 0)),
                  pl.BlockSpec((1, d), const), modspec(3), modspec(4),
                  pl.BlockSpec((d, tf), lambda i, j: (0, j)),
                  pl.BlockSpec((d, tf), lambda i, j: (0, j)),
                  pl.BlockSpec((tf, d), lambda i, j: (j, 0)),
                  pl.BlockSpec((1, d), const), modspec(5)],
        out_specs=pl.BlockSpec((tm, d), lambda i, j: (i, 0)),
        out_shape=jax.ShapeDtypeStruct((n_rows, d), F32),
        scratch_shapes=[pltpu.VMEM((tm, d), BF16), pltpu.VMEM((tm, d), F32)],
        compiler_params=_cparams(2),
        name="ffn",
    )(x, nw_pre, mod, mod, wg, wu, wd, nw_post, mod)


_MIX = {}
_off = 0
for _name, _w in (("m_q", 512), ("m_k", 512), ("m_v", 512), ("m_o", 512), ("m_gates", 16),
                  ("g_q", 256), ("g_k", 256), ("g_v", 512), ("g_out", 512), ("g_lr", 32),
                  ("d_q", 1024), ("d_k", 1024), ("d_v", 1024)):
    _MIX[_name] = (_off, _w)
    _off += _w
_P_ORDER = ("m_q", "m_k", "m_v", "m_o", "g_q", "g_k", "g_v", "g_out")
_QKV_ORDER = ("d_q", "d_k", "d_v")


def _split_w_in(w_in):
    cols = lambda n: w_in[:, _MIX[n][0]:_MIX[n][0] + _MIX[n][1]]
    w_a = jnp.concatenate([cols(n) for n in _P_ORDER], axis=1).astype(BF16)
    w_b = jnp.concatenate([cols(n) for n in _QKV_ORDER], axis=1).astype(BF16)
    w_small = jnp.concatenate([cols("m_gates"), cols("g_lr"),
                               jnp.zeros((w_in.shape[0], HEAD_DIM - 48), w_in.dtype)], axis=1).astype(BF16)
    return w_a, w_b, w_small


def _rope_tables(cfg):
    rows = cfg.n_lat // GRID_W
    r = jnp.repeat(jnp.arange(rows, dtype=F32), GRID_W)
    c = jnp.tile(jnp.arange(GRID_W, dtype=F32), rows)
    half = 16
    inv_freq = ROPE_BASE ** (-jnp.arange(half, dtype=F32) / half)
    ang_r, ang_c = r[:, None] * inv_freq, c[:, None] * inv_freq
    ang = jnp.concatenate([ang_r, ang_r, ang_c, ang_c], axis=-1)
    ang = jnp.tile(ang, (cfg.B, 2))
    n_c = cfg.B * cfg.n_ctx
    cos_t = jnp.concatenate([jnp.cos(ang), jnp.ones((n_c, HEAD_DIM), F32)], axis=0)
    sin_t = jnp.concatenate([jnp.sin(ang), jnp.zeros((n_c, HEAD_DIM), F32)], axis=0)
    return cos_t, sin_t


def _layer(xt, mod, lw, lam_init, rope, cfg, need_ctx):
    d = cfg.D
    p, qkv, ps = _in_call(xt, lw["norm_mix_pre"].reshape(1, d), mod, *_split_w_in(lw["w_in"]), *rope, cfg)

    mq, mk = _conv_call(p, lw["mlstm_conv_w"], lw["mlstm_conv_b"], cfg)
    hmf, hmb = _mlstm_call(mq, mk, p, ps, lw["mlstm_gate_b"], cfg)

    w2 = lw["gla_gate_w2"]
    w2p = jnp.zeros((HEAD_DIM, 2 * G_QK), F32)
    w2p = w2p.at[16:32, 0:G_QK].set(w2[0]).at[32:48, G_QK:].set(w2[1])
    hgf, hgb = _gla_call(p, ps, w2p, lw["gla_gate_b"].reshape(1, 2 * G_QK), cfg)

    hd = _attn_call(qkv, None, lw["diff_lambda"], lw["diff_subln"], lam_init, cfg, latent=True)
    if need_ctx:
        hd = _attn_call(qkv, hd, lw["diff_lambda"], lw["diff_subln"], lam_init, cfg, latent=False)

    n_rows = cfg.T if need_ctx else cfg.TL
    xt = _out_call(xt, hmf, hmb, hgf, hgb, hd, p, lw["mlstm_norm"].reshape(1, 512),
                   lw["gla_norm"].reshape(1, 512), lw["w_out"].astype(BF16),
                   lw["norm_mix_post"].reshape(1, d), mod, cfg, n_rows)
    xt = _ffn_call(xt, lw["norm_ffn_pre"].reshape(1, d), lw["norm_ffn_post"].reshape(1, d), mod,
                   lw["w_ffn_gate"].astype(BF16), lw["w_ffn_up"].astype(BF16), lw["w_ffn_down"].astype(BF16),
                   cfg, n_rows)
    return xt


_LAYER_KEYS = ("norm_mix_pre", "norm_mix_post", "norm_ffn_pre", "norm_ffn_post", "w_in", "mlstm_conv_w",
               "mlstm_conv_b", "mlstm_gate_b", "mlstm_norm", "gla_gate_w2", "gla_gate_b", "gla_norm",
               "diff_lambda", "diff_subln", "w_out", "w_ffn_gate", "w_ffn_up", "w_ffn_down")


def kernel(x, c, ctx, c_ctx, w_mod, b_mod, norm_mix_pre, norm_mix_post, norm_ffn_pre, norm_ffn_post, w_in, mlstm_conv_w, mlstm_conv_b, mlstm_gate_b, mlstm_norm, gla_gate_w2, gla_gate_b, gla_norm, diff_lambda, diff_subln, w_out, w_ffn_gate, w_ffn_up, w_ffn_down):
    weights = dict(zip(_LAYER_KEYS, (norm_mix_pre, norm_mix_post, norm_ffn_pre, norm_ffn_post, w_in,
                                     mlstm_conv_w, mlstm_conv_b, mlstm_gate_b, mlstm_norm, gla_gate_w2,
                                     gla_gate_b, gla_norm, diff_lambda, diff_subln, w_out, w_ffn_gate,
                                     w_ffn_up, w_ffn_down)))
    b, n_lat, d = x.shape
    cfg = Cfg(B=b, n_ctx=ctx.shape[1], n_lat=n_lat, D=d, F=w_ffn_gate.shape[-1])
    depth = w_mod.shape[0]
    c8 = jnp.zeros((8, d), F32).at[0].set(c_ctx).at[1:1 + b].set(c)
    mods = _mod_call(c8, w_mod, b_mod).reshape(depth, 8 * 6, 1, d)
    rope = _rope_tables(cfg)
    xt = jnp.concatenate([x.reshape(b * n_lat, d), ctx.reshape(b * ctx.shape[1], d)], axis=0)
    for layer in range(depth):
        lw = {k: v[layer] for k, v in weights.items()}
        lam_init = 0.8 - 0.6 * math.exp(-0.3 * layer)
        xt = _layer(xt, mods[layer], lw, lam_init, rope, cfg, need_ctx=layer < depth - 1)
    return xt.reshape(b, n_lat, d)
```

```python
import dataclasses
import functools
import math

import jax
import jax.numpy as jnp
import numpy as np
from jax import lax
from jax.experimental import pallas as pl
from jax.experimental.pallas import tpu as pltpu

F32 = jnp.float32
BF16 = jnp.bfloat16
NORM_EPS = 1e-6
CHUNK = 64
GRID_W = 64
ROPE_BASE = 10000.0
GLA_TAU = 16.0
HEAD_DIM = 128
VMEM_LIMIT_BYTES = 56 * 1024 * 1024
HIGHEST = lax.Precision.HIGHEST


@dataclasses.dataclass(frozen=True)
class Cfg:
    B: int = 2
    n_ctx: int = 256
    n_lat: int = 4096
    D: int = 2048
    F: int = 5632

    @property
    def T(self):
        return self.B * (self.n_ctx + self.n_lat)

    @property
    def TL(self):
        return self.B * self.n_lat


def _cparams(n_axes):
    return pltpu.CompilerParams(dimension_semantics=("arbitrary",) * n_axes,
                                vmem_limit_bytes=VMEM_LIMIT_BYTES)


def _mod_row(i, cfg, tm):
    lt = cfg.n_lat // tm
    return jnp.where(i < cfg.B * lt, 1 + i // lt, 0)


def _chunk_block(b, c, cfg):
    ncc, ncl = cfg.n_ctx // CHUNK, cfg.n_lat // CHUNK
    return jnp.where(c < ncc, cfg.B * ncl + b * ncc + c, b * ncl + (c - ncc))


def _bwd_chunk(s, cfg):
    ncc, ncl = cfg.n_ctx // CHUNK, cfg.n_lat // CHUNK
    return jnp.where(s < ncc, ncc - 1 - s, ncc + ncl - 1 - (s - ncc))


def _log_sigmoid(x):
    return jnp.minimum(x, 0.0) - jnp.log1p(jnp.exp(-jnp.abs(x)))


def _silu(x):
    return x * jax.nn.sigmoid(x)


def _mod_kernel(c_ref, w_ref, b_ref, o_ref):
    s = _silu(c_ref[...]).astype(BF16)
    o_ref[0] = jnp.dot(s, w_ref[0].astype(BF16), preferred_element_type=F32) + b_ref[0]


def _mod_call(c8, w_mod, b_mod, tn=2048):
    depth, d, n = w_mod.shape
    return pl.pallas_call(
        _mod_kernel,
        grid=(depth, n // tn),
        in_specs=[pl.BlockSpec((8, d), lambda l, j: (0, 0)),
                  pl.BlockSpec((1, d, tn), lambda l, j: (l, 0, j)),
                  pl.BlockSpec((1, 1, tn), lambda l, j: (l, 0, j))],
        out_specs=pl.BlockSpec((1, 8, tn), lambda l, j: (l, 0, j)),
        out_shape=jax.ShapeDtypeStruct((depth, 8, n), F32),
        compiler_params=_cparams(2),
        name="adaln_mod",
    )(c8, w_mod, b_mod.reshape(depth, 1, n))


def _prenorm(x, nw, shift, scale):
    ms = jnp.mean(x * x, axis=-1, keepdims=True)
    return (x * lax.rsqrt(ms + NORM_EPS) * nw) * (1.0 + scale) + shift


D_HEADS = 8
P_COLS = 3584
QKV_COLS = 3072
IN_TILE = 256


def _in_kernel(x_ref, nw_ref, shift_ref, scale_ref, wa_ref, wb_ref, ws_ref, cos_ref, sin_ref,
               p_ref, qkv_ref, ps_ref):
    h = _prenorm(x_ref[...], nw_ref[...], shift_ref[0], scale_ref[0]).astype(BF16)
    ps_ref[...] = jnp.dot(h, ws_ref[...], preferred_element_type=F32)
    p_ref[...] = jnp.dot(h, wa_ref[...], preferred_element_type=F32)
    qkv = jnp.dot(h, wb_ref[...], preferred_element_type=F32)
    lane = lax.broadcasted_iota(jnp.int32, (IN_TILE, HEAD_DIM), 1)
    low = (lane % 32) < 16
    cos, sin = cos_ref[...], sin_ref[...]
    q_scale = 64 ** -0.5 * math.log2(math.e)
    for s in range(2 * D_HEADS):
        cs = slice(s * HEAD_DIM, (s + 1) * HEAD_DIM)
        x = qkv[:, cs]
        rot = jnp.where(low, -pltpu.roll(x, HEAD_DIM - 16, 1), pltpu.roll(x, 16, 1))
        y = x * cos + rot * sin
        qkv_ref[:, cs] = ((y * q_scale) if s < D_HEADS else y).astype(BF16)
    vs = slice(2 * D_HEADS * HEAD_DIM, QKV_COLS)
    qkv_ref[:, vs] = qkv[:, vs].astype(BF16)


def _in_call(x, nw, mod, w_a, w_b, w_small, cos_t, sin_t, cfg):
    t, d = x.shape
    tm = IN_TILE
    row = functools.partial(_mod_row, cfg=cfg, tm=tm)
    resident = lambda shape: pl.BlockSpec(shape, lambda i: (0, 0), pipeline_mode=pl.Buffered(1))
    return pl.pallas_call(
        _in_kernel,
        grid=(t // tm,),
        in_specs=[pl.BlockSpec((tm, d), lambda i: (i, 0)),
                  resident((1, d)),
                  pl.BlockSpec((1, 1, d), lambda i: (row(i) * 6 + 0, 0, 0)),
                  pl.BlockSpec((1, 1, d), lambda i: (row(i) * 6 + 1, 0, 0)),
                  resident((d, P_COLS)), resident((d, QKV_COLS)), resident((d, HEAD_DIM)),
                  pl.BlockSpec((tm, HEAD_DIM), lambda i: (i, 0)),
                  pl.BlockSpec((tm, HEAD_DIM), lambda i: (i, 0))],
        out_specs=[pl.BlockSpec((tm, P_COLS), lambda i: (i, 0)),
                   pl.BlockSpec((tm, QKV_COLS), lambda i: (i, 0)),
                   pl.BlockSpec((tm, HEAD_DIM), lambda i: (i, 0))],
        out_shape=[jax.ShapeDtypeStruct((t, P_COLS), F32), jax.ShapeDtypeStruct((t, QKV_COLS), BF16),
                   jax.ShapeDtypeStruct((t, HEAD_DIM), F32)],
        compiler_params=_cparams(1),
        name="in_proj",
    )(x, nw, mod, mod, w_a, w_b, w_small, cos_t, sin_t)


CONV_TILE = 256


def _conv_kernel(xq_ref, xk_ref, pq_ref, pk_ref, nq_ref, nk_ref, w_ref, b_ref, q_ref, k_ref, *,
                 starts, ends):
    i = pl.program_id(0)
    is_start = functools.reduce(jnp.logical_or, [i == s for s in starts])
    is_end = functools.reduce(jnp.logical_or, [i == s for s in ends])
    tr = CONV_TILE
    rows = lax.broadcasted_iota(jnp.int32, (tr, 512), 0)

    def conv(x_ref, p_ref, n_ref, half):
        x = x_ref[...]
        prev_row = jnp.where(is_start, 0.0, p_ref[7:8, :])
        next_row = jnp.where(is_end, 0.0, n_ref[0:1, :])
        xp = jnp.where(rows == 0, prev_row, pltpu.roll(x, 1, 0))
        xn = jnp.where(rows == tr - 1, next_row, pltpu.roll(x, tr - 1, 0))
        lo, hi = half * 512, (half + 1) * 512
        y = xp * w_ref[0:1, lo:hi] + x * w_ref[1:2, lo:hi] + xn * w_ref[2:3, lo:hi] + b_ref[:, lo:hi]
        return _silu(y)

    q_ref[...] = conv(xq_ref, pq_ref, nq_ref, 0) * (HEAD_DIM ** -0.5)
    k_ref[...] = conv(xk_ref, pk_ref, nk_ref, 1)


def _conv_call(p, conv_w, conv_b, cfg):
    t = p.shape[0]
    tr = CONV_TILE
    nt = t // tr
    lt, ct = cfg.n_lat // tr, cfg.n_ctx // tr
    seg_first = [b * lt for b in range(cfg.B)] + [cfg.B * lt + b * ct for b in range(cfg.B)]
    seg_last = [b * lt + lt - 1 for b in range(cfg.B)] + [cfg.B * lt + b * ct + ct - 1 for b in range(cfg.B)]
    r8 = tr // 8
    last8 = t // 8 - 1
    kern = functools.partial(_conv_kernel, starts=tuple(seg_first), ends=tuple(seg_last))
    prev = lambda c: (lambda i: (jnp.maximum(i * r8 - 1, 0), c))
    nxt = lambda c: (lambda i: (jnp.minimum((i + 1) * r8, last8), c))
    return pl.pallas_call(
        kern,
        grid=(nt,),
        in_specs=[pl.BlockSpec((tr, 512), lambda i: (i, 0)),
                  pl.BlockSpec((tr, 512), lambda i: (i, 1)),
                  pl.BlockSpec((8, 512), prev(0)), pl.BlockSpec((8, 512), prev(1)),
                  pl.BlockSpec((8, 512), nxt(0)), pl.BlockSpec((8, 512), nxt(1)),
                  pl.BlockSpec((3, 1024), lambda i: (0, 0)),
                  pl.BlockSpec((1, 1024), lambda i: (0, 0))],
        out_specs=[pl.BlockSpec((tr, 512), lambda i: (i, 0)), pl.BlockSpec((tr, 512), lambda i: (i, 0))],
        out_shape=[jax.ShapeDtypeStruct((t, 512), F32)] * 2,
        compiler_params=_cparams(1),
        name="mlstm_conv",
    )(p, p, p, p, p, p, conv_w, conv_b.reshape(1, 1024))


M_HEADS = 4


def _split3(x):
    hi = x.astype(BF16)
    r1 = x - hi.astype(F32)
    mid = r1.astype(BF16)
    return hi, mid, (r1 - mid.astype(F32)).astype(BF16)


def _cumsum_matmul(mask, x):
    return sum(jnp.dot(mask, part, preferred_element_type=F32) for part in _split3(x))


def _mlstm_kernel(qf, kf, vf, gf, qb, kb, vb, gb, brow, bcol, of, ob, c_scr, m_scr):
    @pl.when(pl.program_id(1) == 0)
    def _():
        c_scr[...] = jnp.zeros_like(c_scr)
        m_scr[...] = jnp.zeros_like(m_scr)

    L = CHUNK
    row = lax.broadcasted_iota(jnp.int32, (L, L), 0)
    col = lax.broadcasted_iota(jnp.int32, (L, L), 1)
    neg_inf = jnp.float32(-jnp.inf)

    gcol = jnp.concatenate([gf[...], gb[...]], axis=0) + brow[...]
    grow = jnp.concatenate([gf[...], gb[...]], axis=0).T + bcol[...]
    r8 = lax.broadcasted_iota(jnp.int32, (8, 2 * L), 0)
    l8 = lax.broadcasted_iota(jnp.int32, (8, 2 * L), 1)
    fwd_row = r8 < M_HEADS
    own = (r8 // M_HEADS) == (l8 // L)
    ig = jnp.concatenate([grow[0:4], grow[8:12]], axis=0)
    logf = _log_sigmoid(jnp.concatenate([grow[4:8], grow[12:16]], axis=0))
    tl = lax.broadcasted_iota(jnp.int32, (2 * L, 2 * L), 0)
    ti = lax.broadcasted_iota(jnp.int32, (2 * L, 2 * L), 1)
    same_half = (tl // L) == (ti // L)
    scan_rows = jnp.logical_and(same_half, jnp.where(ti < L, tl - ti, ti - tl) <= 0)
    scan_cols = jnp.logical_and(same_half, jnp.where(tl < L, ti - tl, tl - ti) <= 0)
    cf = sum(jnp.dot(part, scan_rows.astype(BF16), preferred_element_type=F32) for part in _split3(logf))
    u = jnp.where(own, ig - cf, neg_inf)
    m_st = m_scr[...]
    end_lane = jnp.where(fwd_row, L - 1, L)
    f_end = jnp.broadcast_to(jnp.sum(jnp.where(l8 == end_lane, cf, 0.0), axis=-1, keepdims=True),
                             (8, 2 * L))
    dec = jnp.where(own, f_end - cf + ig, neg_inf)
    m_new = jnp.maximum(f_end + m_st, jnp.max(dec, axis=-1, keepdims=True))
    a_prev = jnp.exp(f_end + m_st - m_new)
    m_scr[...] = m_new
    u_keys = jnp.where(fwd_row, u, pltpu.roll(u, L, 1))
    cum_col = _cumsum_matmul(scan_cols.astype(BF16), _log_sigmoid(gcol))

    rep = lambda colv: jnp.broadcast_to(colv, (L, HEAD_DIM))
    dir_refs = ((qf, kf, vf, of), (qb, kb, vb, ob))
    pairs = [(d, h) for d in range(2) for h in range(M_HEADS)]
    work = []
    for d, h in pairs:
        q_ref, k_ref, v_ref, _ = dir_refs[d]
        hs = slice(h * HEAD_DIM, (h + 1) * HEAD_DIM)
        q = q_ref[:, hs].astype(BF16)
        k32 = k_ref[:, hs]
        v1 = jnp.concatenate([v_ref[:, hs].astype(BF16), jnp.ones((L, HEAD_DIM), BF16)], axis=1)
        cn = c_scr[d, h]
        qk = lax.dot_general(q, k32.astype(BF16), (((1,), (1,)), ((), ())), preferred_element_type=F32)
        r_state = jnp.dot(q, cn.astype(BF16), preferred_element_type=F32)
        work.append((k32, v1, cn, qk, r_state))
    for (d, h), (k32, v1, cn, qk, r_state) in zip(pairs, work):
        o_ref = dir_refs[d][3]
        valid = (col <= row) if d == 0 else (col >= row)
        ts = slice(d * L, (d + 1) * L)
        r = d * M_HEADS + h
        ci, cfc = d * 8 + h, d * 8 + 4 + h
        hs = slice(h * HEAD_DIM, (h + 1) * HEAD_DIM)
        u_tile = jnp.where(valid, u_keys[r:r + 1, 0:L], neg_inf)
        m_rep = rep(jnp.maximum(jnp.max(u_tile, axis=-1, keepdims=True), m_st[r:r + 1, 0:1]))
        cf_rep = rep(cum_col[ts, cfc:cfc + 1])
        ig_rep = rep(gcol[ts, ci:ci + 1])
        w_inter = jnp.exp(m_st[r:r + 1] - m_rep)
        e = jnp.exp(u_tile - m_rep[:, 0:L])
        r_chunk = jnp.dot((qk * e).astype(BF16), v1, preferred_element_type=F32)
        num = w_inter * r_state[:, 0:HEAD_DIM] + r_chunk[:, 0:HEAD_DIM]
        den = w_inter * r_state[:, HEAD_DIM:] + r_chunk[:, HEAD_DIM:]
        o_ref[:, hs] = num / jnp.maximum(jnp.abs(den), jnp.exp(-(cf_rep + m_rep)))

        ws = jnp.exp(f_end[r:r + 1] - cf_rep + ig_rep - m_new[r:r + 1])
        wk = (ws * k32).astype(BF16)
        decay = jnp.concatenate([a_prev[r:r + 1], a_prev[r:r + 1]], axis=1)
        c_scr[d, h] = decay * cn + lax.dot_general(wk, v1, (((0,), (0,)), ((), ())),
                                                   preferred_element_type=F32)


def _mlstm_call(q, k, p, ps, gate_b, cfg):
    t = q.shape[0]
    nc = (cfg.n_ctx + cfg.n_lat) // CHUNK
    fwd = lambda col: (lambda b, s: (_chunk_block(b, s, cfg), col))
    bwd = lambda col: (lambda b, s: (_chunk_block(b, _bwd_chunk(s, cfg), cfg), col))
    blk = lambda w, f: pl.BlockSpec((CHUNK, w), f)
    brow = jnp.zeros((1, HEAD_DIM), F32).at[0, :16].set(gate_b)
    const = lambda b, s: (0, 0)
    return pl.pallas_call(
        _mlstm_kernel,
        grid=(cfg.B, nc),
        in_specs=[blk(512, fwd(0)), blk(512, fwd(0)), blk(512, fwd(2)), blk(HEAD_DIM, fwd(0)),
                  blk(512, bwd(0)), blk(512, bwd(0)), blk(512, bwd(2)), blk(HEAD_DIM, bwd(0)),
                  pl.BlockSpec((1, HEAD_DIM), const), pl.BlockSpec((HEAD_DIM, 1), const)],
        out_specs=[blk(512, fwd(0)), blk(512, bwd(0))],
        out_shape=[jax.ShapeDtypeStruct((t, 512), F32)] * 2,
        scratch_shapes=[pltpu.VMEM((2, M_HEADS, HEAD_DIM, 2 * HEAD_DIM), F32),
                        pltpu.VMEM((2 * M_HEADS, 2 * CHUNK), F32)],
        compiler_params=_cparams(2),
        name="mlstm_scan",
    )(q, k, p, ps, q, k, p, ps, brow, brow.reshape(HEAD_DIM, 1))


G_QK = 256
G_V = 512


G_HEADS = 4
G_DK = 64
GLA_BLK = 16


def _gla_kernel(qkf, vf, lf, qkb, vb, lb, w2_ref, b2_ref, of, ob, s_scr, gl_scr):
    @pl.when(pl.program_id(1) == 0)
    def _():
        s_scr[...] = jnp.zeros_like(s_scr)

    L, C = CHUNK, GLA_BLK
    nb = L // C
    row = lax.broadcasted_iota(jnp.int32, (L, L), 0)
    col = lax.broadcasted_iota(jnp.int32, (L, L), 1)
    same_blk = (row // C) == (col // C)
    rows8 = lax.broadcasted_iota(jnp.int32, (8, G_QK), 0)
    neg_inf = jnp.float32(-jnp.inf)
    he_r = lax.broadcasted_iota(jnp.int32, (G_QK, G_V), 0) // G_DK
    he_c = lax.broadcasted_iota(jnp.int32, (G_QK, G_V), 1) // HEAD_DIM
    head_expand = (he_r == he_c).astype(BF16)

    def per_head_rows(x):
        lane_head = lax.broadcasted_iota(jnp.int32, x.shape, 1) // G_DK
        return jnp.concatenate([jnp.where(lane_head == h, x, 0.0) for h in range(G_HEADS)], axis=0).astype(BF16)

    def head_blocks(r, n):
        return jnp.concatenate([r[h * n:(h + 1) * n, h * HEAD_DIM:(h + 1) * HEAD_DIM] for h in range(G_HEADS)],
                               axis=1)

    def direction(d, qk_ref, v_ref, l_ref, o_ref):
        cs = slice(d * G_QK, (d + 1) * G_QK)
        z = jnp.dot(l_ref[...], w2_ref[:, cs], precision=HIGHEST, preferred_element_type=F32) + b2_ref[:, cs]
        log_a = _log_sigmoid(z) * (1.0 / GLA_TAU)
        tri = (col <= row) if d == 0 else (col >= row)
        g = _cumsum_matmul(tri.astype(BF16), log_a)
        gl = _cumsum_matmul(jnp.logical_and(tri, same_blk).astype(BF16), log_a)
        gl_scr[d] = gl
        q = qk_ref[:, 0:G_QK] * (G_DK ** -0.5)
        k = qk_ref[:, G_QK:2 * G_QK]
        v = v_ref[...]
        vb16 = v.astype(BF16)
        g_end = g[L - 1:L, :] if d == 0 else g[0:1, :]
        st = s_scr[d]

        r = lax.dot_general(per_head_rows(q * jnp.exp(g)), st.astype(BF16), (((1,), (1,)), ((), ())),
                            preferred_element_type=F32)
        inter = jnp.concatenate([r[h * L:(h + 1) * L] for h in range(G_HEADS)], axis=1)
        yield

        ql = q * jnp.exp(gl)
        first = lambda b: b * C + (C - 1 if d == 0 else 0)
        tot = [gl[first(b):first(b) + 1] for b in range(nb)]
        k_end = [k[b * C:(b + 1) * C] * jnp.exp(tot[b] - gl[b * C:(b + 1) * C]) for b in range(nb)]

        atts, v_cats = {}, {}
        for bi in range(nb):
            earlier = list(range(bi)) if d == 0 else list(range(bi + 1, nb))
            if not earlier:
                continue
            ks, vs = [], []
            for bj in earlier:
                between = range(bj + 1, bi) if d == 0 else range(bi + 1, bj)
                kj = k_end[bj]
                if len(between):
                    kj = kj * jnp.exp(functools.reduce(jnp.add, [tot[m] for m in between]))
                ks.append(kj)
                vs.append(vb16[bj * C:(bj + 1) * C])
            v_cats[bi] = jnp.concatenate(vs, axis=0)
            atts[bi] = lax.dot_general(per_head_rows(ql[bi * C:(bi + 1) * C]),
                                       jnp.concatenate(ks, axis=0).astype(BF16), (((1,), (1,)), ((), ())),
                                       preferred_element_type=F32)

        k_hat = per_head_rows(k * jnp.exp(g_end - g))
        v_rows = jnp.concatenate([vb16[:, h * HEAD_DIM:(h + 1) * HEAD_DIM] for h in range(G_HEADS)], axis=0)
        s_new = st * jnp.exp(g_end) + lax.dot_general(v_rows, k_hat, (((0,), (0,)), ((), ())),
                                                      preferred_element_type=F32)
        yield

        pieces, spans = [], []
        for i in range(L):
            bi, il = divmod(i, C)
            lo, hi = ((il // 8) * 8, C) if d == 0 else (0, (il // 8) * 8 + 8)
            edge = (lo, lo + 8) if d == 0 else (hi - 8, hi)
            k_i = qk_ref[i:i + 1, G_QK:2 * G_QK]
            g_i = gl_scr[d, i:i + 1, :]
            vis = (rows8 >= il % 8) if d == 0 else (rows8 <= il % 8)
            diff = jnp.where(vis, gl[bi * C + edge[0]:bi * C + edge[1]] - g_i, neg_inf)
            if hi - lo > 8:
                other = gl[bi * C + 8:bi * C + 16] if d == 0 else gl[bi * C:bi * C + 8]
                diff = jnp.concatenate([diff, other - g_i] if d == 0 else [other - g_i, diff], axis=0)
            pieces.append((q[bi * C + lo:bi * C + hi] * k_i) * jnp.exp(diff))
            spans.append((lo, hi))
        w_all = jnp.dot(jnp.concatenate(pieces, axis=0).astype(BF16), head_expand, preferred_element_type=F32)
        yield
        cross = {bi: head_blocks(jnp.dot(att.astype(BF16), v_cats[bi], preferred_element_type=F32), C)
                 for bi, att in atts.items()}
        yield

        blocks = []
        w_off = 0
        for bi in range(nb):
            acc = inter[bi * C:(bi + 1) * C]
            if bi in cross:
                acc = acc + cross[bi]
            for il in range(C):
                i = bi * C + il
                lo, hi = spans[i]
                upd = acc[lo:hi] + w_all[w_off:w_off + hi - lo] * v_ref[i:i + 1, :]
                w_off += hi - lo
                parts = ([acc[:lo]] if lo > 0 else []) + [upd] + ([acc[hi:]] if hi < C else [])
                acc = jnp.concatenate(parts, axis=0) if len(parts) > 1 else upd
            blocks.append(acc)
        o_ref[...] = jnp.concatenate(blocks, axis=0)
        s_scr[d] = s_new

    stages = [direction(0, qkf, vf, lf, of), direction(1, qkb, vb, lb, ob)]
    for _ in range(5):
        for stage in stages:
            next(stage, None)


def _gla_call(p, ps, w2p, b2p, cfg):
    t = p.shape[0]
    nc = (cfg.n_ctx + cfg.n_lat) // CHUNK
    fwd = lambda col: (lambda b, s: (_chunk_block(b, s, cfg), col))
    bwd = lambda col: (lambda b, s: (_chunk_block(b, _bwd_chunk(s, cfg), cfg), col))
    blk = lambda w, f: pl.BlockSpec((CHUNK, w), f)
    const = lambda b, s: (0, 0)
    return pl.pallas_call(
        _gla_kernel,
        grid=(cfg.B, nc),
        in_specs=[blk(512, fwd(4)), blk(512, fwd(5)), blk(HEAD_DIM, fwd(0)),
                  blk(512, bwd(4)), blk(512, bwd(5)), blk(HEAD_DIM, bwd(0)),
                  pl.BlockSpec((HEAD_DIM, 2 * G_QK), const), pl.BlockSpec((1, 2 * G_QK), const)],
        out_specs=[blk(512, fwd(0)), blk(512, bwd(0))],
        out_shape=[jax.ShapeDtypeStruct((t, G_V), F32)] * 2,
        scratch_shapes=[pltpu.VMEM((2, HEAD_DIM, G_QK), F32),
                        pltpu.VMEM((2, CHUNK, G_QK), F32)],
        compiler_params=_cparams(2),
        name="gla_scan",
    )(p, p, ps, p, p, ps, w2p, b2p)


ATT_SUB_ROWS = 256
ATT_TQ = 1024
ATT_KEY_CHUNK = 1024


def _attn_kernel(*refs, has_lat, lam_init):
    if has_lat:
        q_ref, kc_ref, vc_ref, kl_ref, vl_ref, dl_ref, sub_ref, o_ref, s_scr, vo_scr = refs
        kv = ((kc_ref, vc_ref), (kl_ref, vl_ref))
    else:
        q_ref, kc_ref, vc_ref, dl_ref, sub_ref, o_ref, s_scr, vo_scr = refs
        kv = ((kc_ref, vc_ref),)

    @pl.when(pl.program_id(2) == 0)
    def _():
        off = 0
        for _, v_ref in kv:
            n = v_ref.shape[0]
            vo_scr[off:off + n, 0:HEAD_DIM] = v_ref[...]
            vo_scr[off:off + n, HEAD_DIM:2 * HEAD_DIM] = jnp.ones((n, HEAD_DIM), BF16)
            off += n

    n_sub = q_ref.shape[0] // ATT_SUB_ROWS
    lane = lax.broadcasted_iota(jnp.int32, (ATT_SUB_ROWS, HEAD_DIM), 1)
    zero = jnp.zeros((ATT_SUB_ROWS, HEAD_DIM), BF16)
    dl = dl_ref[...]
    lam = (jnp.exp(jnp.sum(dl[0:1] * dl[1:2], axis=-1, keepdims=True))
           - jnp.exp(jnp.sum(dl[2:3] * dl[3:4], axis=-1, keepdims=True)) + lam_init)
    key_chunks, off = [], 0
    for k_ref, _ in kv:
        for c0 in range(0, k_ref.shape[0], ATT_KEY_CHUNK):
            n = min(ATT_KEY_CHUNK, k_ref.shape[0] - c0)
            key_chunks.append((k_ref, c0, off + c0, n))
        off += k_ref.shape[0]

    prev = None
    for sb in range(n_sub + 1):
        if sb < n_sub:
            q = q_ref[sb * ATT_SUB_ROWS:(sb + 1) * ATT_SUB_ROWS, :]
            qms = (jnp.where(lane < 64, q, zero), jnp.where(lane >= 64, q, zero))
            part_max = [None, None]
        accs = [None, None]
        for k_ref, c0, off, n in key_chunks:
            for m in range(2):
                if sb < n_sub:
                    s = lax.dot_general(qms[m], k_ref[c0:c0 + n, :], (((1,), (1,)), ((), ())),
                                        preferred_element_type=F32)
                    s_scr[2 * sb + m, :, off:off + n] = s
                    folded = functools.reduce(jnp.maximum, [s[:, j:j + HEAD_DIM] for j in range(0, n, HEAD_DIM)])
                    part_max[m] = folded if part_max[m] is None else jnp.maximum(part_max[m], folded)
                if prev is not None:
                    p = jnp.exp2(s_scr[2 * (sb - 1) + m, :, off:off + n] - prev[m]).astype(BF16)
                    part = jnp.dot(p, vo_scr[off:off + n, :], preferred_element_type=F32)
                    accs[m] = part if accs[m] is None else accs[m] + part
        if prev is not None:
            outs = [a[:, 0:HEAD_DIM] / a[:, HEAD_DIM:HEAD_DIM + 1] for a in accs]
            out = outs[0] - lam * outs[1]
            ms = jnp.mean(out * out, axis=-1, keepdims=True)
            o_ref[(sb - 1) * ATT_SUB_ROWS:sb * ATT_SUB_ROWS, :] = (
                (out * lax.rsqrt(ms + NORM_EPS) * sub_ref[...]) * (1.0 - lam_init))
        prev = [jnp.max(pm, axis=-1, keepdims=True) for pm in part_max] if sb < n_sub else None


def _attn_call(qkv, d_lam, d_subln, lam_init, cfg, latent):
    nlb = cfg.n_lat // ATT_TQ
    ctx_row0 = cfg.B * cfg.n_lat // cfg.n_ctx
    kern = functools.partial(_attn_kernel, has_lat=latent, lam_init=lam_init)
    n_keys = cfg.n_ctx + (cfg.n_lat if latent else 0)
    kc = pl.BlockSpec((cfg.n_ctx, HEAD_DIM), lambda b, h, i: (ctx_row0 + b, D_HEADS + h))
    vc = pl.BlockSpec((cfg.n_ctx, HEAD_DIM), lambda b, h, i: (ctx_row0 + b, 2 * D_HEADS + h))
    small = [pl.BlockSpec((4, 64), lambda b, h, i: (0, 0)), pl.BlockSpec((1, HEAD_DIM), lambda b, h, i: (0, 0))]
    if latent:
        tq = ATT_TQ
        grid = (cfg.B, D_HEADS, nlb)
        q_spec = pl.BlockSpec((tq, HEAD_DIM), lambda b, h, i: (b * nlb + i, h))
        kv = [kc, vc,
              pl.BlockSpec((cfg.n_lat, HEAD_DIM), lambda b, h, i: (b, D_HEADS + h)),
              pl.BlockSpec((cfg.n_lat, HEAD_DIM), lambda b, h, i: (b, 2 * D_HEADS + h))]
        out_spec = pl.BlockSpec((tq, HEAD_DIM), lambda b, h, i: (b * nlb + i, h))
        n_rows = cfg.TL
    else:
        tq = cfg.n_ctx
        grid = (cfg.B, D_HEADS, 1)
        q_spec = pl.BlockSpec((tq, HEAD_DIM), lambda b, h, i: (ctx_row0 + b, h))
        kv = [kc, vc]
        out_spec = pl.BlockSpec((tq, HEAD_DIM), lambda b, h, i: (b, h))
        n_rows = cfg.B * cfg.n_ctx
    args = [qkv] * (1 + len(kv)) + [d_lam, d_subln.reshape(1, HEAD_DIM)]
    return pl.pallas_call(
        kern,
        grid=grid,
        in_specs=[q_spec] + kv + small,
        out_specs=out_spec,
        out_shape=jax.ShapeDtypeStruct((n_rows, D_HEADS * HEAD_DIM), F32),
        scratch_shapes=[pltpu.VMEM((2 * tq // ATT_SUB_ROWS, ATT_SUB_ROWS, n_keys), F32),
                        pltpu.VMEM((n_keys, 2 * HEAD_DIM), BF16)],
        compiler_params=_cparams(3),
        name="diff_attn_lat" if latent else "diff_attn_ctx",
    )(*args)


OUT_TILE = 256


def _group_rmsnorm(x, w, groups):
    parts = []
    for gi in range(groups):
        xs = x[:, gi * HEAD_DIM:(gi + 1) * HEAD_DIM]
        ms = jnp.mean(xs * xs, axis=-1, keepdims=True)
        parts.append(xs * lax.rsqrt(ms + NORM_EPS) * w[:, gi * HEAD_DIM:(gi + 1) * HEAD_DIM])
    return jnp.concatenate(parts, axis=-1)


def _out_kernel(*refs, n_lat_tiles, has_ctx):
    if has_ctx:
        x_ref, hmf, hmb, hgf, hgb, hdl, hdc, mo, go, mn, gn, w_ref, nw_ref, gate_ref, o_ref = refs
        hd = jnp.where(pl.program_id(0) < n_lat_tiles, hdl[...], hdc[...])
    else:
        x_ref, hmf, hmb, hgf, hgb, hdl, mo, go, mn, gn, w_ref, nw_ref, gate_ref, o_ref = refs
        hd = hdl[...]
    ym = _group_rmsnorm(hmf[...] + hmb[...], mn[...], 4) * jax.nn.sigmoid(mo[...])
    yg = _group_rmsnorm(hgf[...] + hgb[...], gn[...], 4) * _silu(go[...])
    y = jnp.concatenate([ym.astype(BF16), yg.astype(BF16), hd.astype(BF16)], axis=-1)
    z = jnp.dot(y, w_ref[...], preferred_element_type=F32)
    ms = jnp.mean(z * z, axis=-1, keepdims=True)
    o_ref[...] = x_ref[...] + gate_ref[0] * (z * lax.rsqrt(ms + NORM_EPS) * nw_ref[...])


def _out_call(x, hmf, hmb, hgf, hgb, hd_lat, hd_ctx, p, m_norm, g_norm, w_out, nw, mod, cfg, n_rows):
    d = cfg.D
    tm = OUT_TILE
    nl = cfg.TL // tm
    row = functools.partial(_mod_row, cfg=cfg, tm=tm)
    rt = lambda w, c: pl.BlockSpec((tm, w), lambda i: (i, c))
    const = lambda i: (0, 0)
    hd_specs = [pl.BlockSpec((tm, 1024), lambda i: (jnp.minimum(i, nl - 1), 0))]
    hd_args = [hd_lat]
    if hd_ctx is not None:
        hd_specs.append(pl.BlockSpec((tm, 1024), lambda i: (jnp.maximum(i - nl, 0), 0)))
        hd_args.append(hd_ctx)
    return pl.pallas_call(
        functools.partial(_out_kernel, n_lat_tiles=nl, has_ctx=hd_ctx is not None),
        grid=(n_rows // tm,),
        in_specs=[rt(d, 0), rt(512, 0), rt(512, 0), rt(512, 0), rt(512, 0)] + hd_specs + [
                  rt(512, 3), rt(512, 6),
                  pl.BlockSpec((1, 512), const), pl.BlockSpec((1, 512), const),
                  pl.BlockSpec((d, d), const), pl.BlockSpec((1, d), const),
                  pl.BlockSpec((1, 1, d), lambda i: (row(i) * 6 + 2, 0, 0))],
        out_specs=rt(d, 0),
        out_shape=jax.ShapeDtypeStruct((n_rows, d), F32),
        compiler_params=_cparams(1),
        name="out_proj",
    )(x, hmf, hmb, hgf, hgb, *hd_args, p, p, m_norm, g_norm, w_out, nw, mod)


def _ffn_kernel(x_ref, nw_ref, shift_ref, scale_ref, wg_ref, wu_ref, wd_ref, pw_ref, gate_ref, o_ref,
                h_scr, acc_scr):
    j = pl.program_id(1)

    @pl.when(j == 0)
    def _():
        h_scr[...] = _prenorm(x_ref[...], nw_ref[...], shift_ref[0], scale_ref[0]).astype(BF16)
        acc_scr[...] = jnp.zeros_like(acc_scr)

    h = h_scr[...]
    a = jnp.dot(h, wg_ref[...], preferred_element_type=F32)
    u = jnp.dot(h, wu_ref[...], preferred_element_type=F32)
    acc_scr[...] += jnp.dot((_silu(a) * u).astype(BF16), wd_ref[...], preferred_element_type=F32)

    @pl.when(j == pl.num_programs(1) - 1)
    def _():
        z = acc_scr[...]
        ms = jnp.mean(z * z, axis=-1, keepdims=True)
        o_ref[...] = x_ref[...] + gate_ref[0] * (z * lax.rsqrt(ms + NORM_EPS) * pw_ref[...])


def _ffn_call(x, nw_pre, nw_post, mod, wg, wu, wd, cfg, n_rows, tm=512, tf=512):
    d, f = wg.shape
    row = functools.partial(_mod_row, cfg=cfg, tm=tm)
    const = lambda i, j: (0, 0)
    modspec = lambda kk: pl.BlockSpec((1, 1, d), lambda i, j: (row(i) * 6 + kk, 0, 0))
    return pl.pallas_call(
        _ffn_kernel,
        grid=(n_rows // tm, f // tf),
        in_specs=[pl.BlockSpec((tm, d), lambda i, j: (i, 0)),
                  pl.BlockSpec((1, d), const), modspec(3), modspec(4),
                  pl.BlockSpec((d, tf), lambda i, j: (0, j)),
                  pl.BlockSpec((d, tf), lambda i, j: (0, j)),
                  pl.BlockSpec((tf, d), lambda i, j: (j, 0)),
                  pl.BlockSpec((1, d), const), modspec(5)],
        out_specs=pl.BlockSpec((tm, d), lambda i, j: (i, 0)),
        out_shape=jax.ShapeDtypeStruct((n_rows, d), F32),
        scratch_shapes=[pltpu.VMEM((tm, d), BF16), pltpu.VMEM((tm, d), F32)],
        compiler_params=_cparams(2),
        name="ffn",
    )(x, nw_pre, mod, mod, wg, wu, wd, nw_post, mod)


_MIX = {}
_off = 0
for _name, _w in (("m_q", 512), ("m_k", 512), ("m_v", 512), ("m_o", 512), ("m_gates", 16),
                  ("g_q", 256), ("g_k", 256), ("g_v", 512), ("g_out", 512), ("g_lr", 32),
                  ("d_q", 1024), ("d_k", 1024), ("d_v", 1024)):
    _MIX[_name] = (_off, _w)
    _off += _w
_P_ORDER = ("m_q", "m_k", "m_v", "m_o", "g_q", "g_k", "g_v", "g_out")
_QKV_ORDER = ("d_q", "d_k", "d_v")


def _split_w_in(w_in):
    cols = lambda n: w_in[:, _MIX[n][0]:_MIX[n][0] + _MIX[n][1]]
    w_a = jnp.concatenate([cols(n) for n in _P_ORDER], axis=1).astype(BF16)
    w_b = jnp.concatenate([cols(n) for n in _QKV_ORDER], axis=1).astype(BF16)
    w_small = jnp.concatenate([cols("m_gates"), cols("g_lr"),
                               jnp.zeros((w_in.shape[0], HEAD_DIM - 48), w_in.dtype)], axis=1).astype(BF16)
    return w_a, w_b, w_small


def _rope_tables(cfg):
    rows = cfg.n_lat // GRID_W
    r = np.repeat(np.arange(rows, dtype=np.float64), GRID_W)
    c = np.tile(np.arange(GRID_W, dtype=np.float64), rows)
    half = 16
    inv_freq = ROPE_BASE ** (-np.arange(half, dtype=np.float64) / half)
    ang_r, ang_c = r[:, None] * inv_freq, c[:, None] * inv_freq
    ang = np.concatenate([ang_r, ang_r, ang_c, ang_c], axis=-1)
    ang = np.tile(ang, (cfg.B, 2))
    n_c = cfg.B * cfg.n_ctx
    cos_t = np.concatenate([np.cos(ang), np.ones((n_c, HEAD_DIM))], axis=0).astype(np.float32)
    sin_t = np.concatenate([np.sin(ang), np.zeros((n_c, HEAD_DIM))], axis=0).astype(np.float32)
    return jnp.asarray(cos_t), jnp.asarray(sin_t)


def _layer(xt, mod, lw, lam_init, rope, cfg, need_ctx):
    d = cfg.D
    p, qkv, ps = _in_call(xt, lw["norm_mix_pre"].reshape(1, d), mod, *_split_w_in(lw["w_in"]), *rope, cfg)

    mq, mk = _conv_call(p, lw["mlstm_conv_w"], lw["mlstm_conv_b"], cfg)
    hmf, hmb = _mlstm_call(mq, mk, p, ps, lw["mlstm_gate_b"], cfg)

    w2 = lw["gla_gate_w2"]
    w2p = jnp.zeros((HEAD_DIM, 2 * G_QK), F32)
    w2p = w2p.at[16:32, 0:G_QK].set(w2[0]).at[32:48, G_QK:].set(w2[1])
    hgf, hgb = _gla_call(p, ps, w2p, lw["gla_gate_b"].reshape(1, 2 * G_QK), cfg)

    hd_lat = _attn_call(qkv, lw["diff_lambda"], lw["diff_subln"], lam_init, cfg, latent=True)
    hd_ctx = _attn_call(qkv, lw["diff_lambda"], lw["diff_subln"], lam_init, cfg, latent=False) if need_ctx else None

    n_rows = cfg.T if need_ctx else cfg.TL
    xt = _out_call(xt, hmf, hmb, hgf, hgb, hd_lat, hd_ctx, p, lw["mlstm_norm"].reshape(1, 512),
                   lw["gla_norm"].reshape(1, 512), lw["w_out"].astype(BF16),
                   lw["norm_mix_post"].reshape(1, d), mod, cfg, n_rows)
    xt = _ffn_call(xt, lw["norm_ffn_pre"].reshape(1, d), lw["norm_ffn_post"].reshape(1, d), mod,
                   lw["w_ffn_gate"].astype(BF16), lw["w_ffn_up"].astype(BF16), lw["w_ffn_down"].astype(BF16),
                   cfg, n_rows)
    return xt


_LAYER_KEYS = ("norm_mix_pre", "norm_mix_post", "norm_ffn_pre", "norm_ffn_post", "w_in", "mlstm_conv_w",
               "mlstm_conv_b", "mlstm_gate_b", "mlstm_norm", "gla_gate_w2", "gla_gate_b", "gla_norm",
               "diff_lambda", "diff_subln", "w_out", "w_ffn_gate", "w_ffn_up", "w_ffn_down")


def kernel(x, c, ctx, c_ctx, w_mod, b_mod, norm_mix_pre, norm_mix_post, norm_ffn_pre, norm_ffn_post, w_in, mlstm_conv_w, mlstm_conv_b, mlstm_gate_b, mlstm_norm, gla_gate_w2, gla_gate_b, gla_norm, diff_lambda, diff_subln, w_out, w_ffn_gate, w_ffn_up, w_ffn_down):
    weights = dict(zip(_LAYER_KEYS, (norm_mix_pre, norm_mix_post, norm_ffn_pre, norm_ffn_post, w_in,
                                     mlstm_conv_w, mlstm_conv_b, mlstm_gate_b, mlstm_norm, gla_gate_w2,
                                     gla_gate_b, gla_norm, diff_lambda, diff_subln, w_out, w_ffn_gate,
                                     w_ffn_up, w_ffn_down)))
    b, n_lat, d = x.shape
    cfg = Cfg(B=b, n_ctx=ctx.shape[1], n_lat=n_lat, D=d, F=w_ffn_gate.shape[-1])
    depth = w_mod.shape[0]
    c8 = jnp.zeros((8, d), F32).at[0].set(c_ctx).at[1:1 + b].set(c)
    mods = _mod_call(c8, w_mod, b_mod).reshape(depth, 8 * 6, 1, d)
    rope = _rope_tables(cfg)
    xt = jnp.concatenate([x.reshape(b * n_lat, d), ctx.reshape(b * ctx.shape[1], d)], axis=0)
    for layer in range(depth):
        lw = {k: v[layer] for k, v in weights.items()}
        lam_init = 0.8 - 0.6 * math.exp(-0.3 * layer)
        xt = _layer(xt, mods[layer], lw, lam_init, rope, cfg, need_ctx=layer < depth - 1)
    return xt.reshape(b, n_lat, d)
```

```python
import dataclasses
import functools
import math

import jax
import jax.numpy as jnp
import numpy as np
from jax import lax
from jax.experimental import pallas as pl
from jax.experimental.pallas import tpu as pltpu

F32 = jnp.float32
BF16 = jnp.bfloat16
NORM_EPS = 1e-6
CHUNK = 64
GRID_W = 64
ROPE_BASE = 10000.0
GLA_TAU = 16.0
HEAD_DIM = 128
VMEM_LIMIT_BYTES = 56 * 1024 * 1024
HIGHEST = lax.Precision.HIGHEST


@dataclasses.dataclass(frozen=True)
class Cfg:
    B: int = 2
    n_ctx: int = 256
    n_lat: int = 4096
    D: int = 2048
    F: int = 5632

    @property
    def T(self):
        return self.B * (self.n_ctx + self.n_lat)

    @property
    def TL(self):
        return self.B * self.n_lat


def _cparams(n_axes):
    return pltpu.CompilerParams(dimension_semantics=("arbitrary",) * n_axes,
                                vmem_limit_bytes=VMEM_LIMIT_BYTES)


def _mod_row(i, cfg, tm):
    lt = cfg.n_lat // tm
    return jnp.where(i < cfg.B * lt, 1 + i // lt, 0)


def _chunk_block(b, c, cfg):
    ncc, ncl = cfg.n_ctx // CHUNK, cfg.n_lat // CHUNK
    return jnp.where(c < ncc, cfg.B * ncl + b * ncc + c, b * ncl + (c - ncc))


def _bwd_chunk(s, cfg):
    ncc, ncl = cfg.n_ctx // CHUNK, cfg.n_lat // CHUNK
    return jnp.where(s < ncc, ncc - 1 - s, ncc + ncl - 1 - (s - ncc))


def _log_sigmoid(x):
    return jnp.minimum(x, 0.0) - jnp.log1p(jnp.exp(-jnp.abs(x)))


def _silu(x):
    return x * jax.nn.sigmoid(x)


def _mod_kernel(c_ref, w_ref, b_ref, o_ref):
    s = _silu(c_ref[...]).astype(BF16)
    o_ref[0] = jnp.dot(s, w_ref[0].astype(BF16), preferred_element_type=F32) + b_ref[0]


def _mod_call(c8, w_mod, b_mod, tn=2048):
    depth, d, n = w_mod.shape
    return pl.pallas_call(
        _mod_kernel,
        grid=(depth, n // tn),
        in_specs=[pl.BlockSpec((8, d), lambda l, j: (0, 0)),
                  pl.BlockSpec((1, d, tn), lambda l, j: (l, 0, j)),
                  pl.BlockSpec((1, 1, tn), lambda l, j: (l, 0, j))],
        out_specs=pl.BlockSpec((1, 8, tn), lambda l, j: (l, 0, j)),
        out_shape=jax.ShapeDtypeStruct((depth, 8, n), F32),
        compiler_params=_cparams(2),
        name="adaln_mod",
    )(c8, w_mod, b_mod.reshape(depth, 1, n))


def _prenorm(x, nw, shift, scale):
    ms = jnp.mean(x * x, axis=-1, keepdims=True)
    return (x * lax.rsqrt(ms + NORM_EPS) * nw) * (1.0 + scale) + shift


D_HEADS = 8
P_COLS = 3584
QKV_COLS = 3072
IN_TILE = 256


def _in_kernel(x_ref, nw_ref, shift_ref, scale_ref, w_ref, cos_ref, sin_ref, p_ref, qkv_ref, ps_ref):
    h = _prenorm(x_ref[...], nw_ref[...], shift_ref[0], scale_ref[0]).astype(BF16)
    ps_ref[...] = jnp.dot(h, w_ref[:, P_COLS + QKV_COLS:], preferred_element_type=F32)
    p_ref[...] = jnp.dot(h, w_ref[:, 0:P_COLS], preferred_element_type=F32)
    qkv = jnp.dot(h, w_ref[:, P_COLS:P_COLS + QKV_COLS], preferred_element_type=F32)
    lane = lax.broadcasted_iota(jnp.int32, (IN_TILE, HEAD_DIM), 1)
    low = (lane % 32) < 16
    cos, sin = cos_ref[...], sin_ref[...]
    q_scale = 64 ** -0.5 * math.log2(math.e)
    for s in range(2 * D_HEADS):
        cs = slice(s * HEAD_DIM, (s + 1) * HEAD_DIM)
        x = qkv[:, cs]
        rot = jnp.where(low, -pltpu.roll(x, HEAD_DIM - 16, 1), pltpu.roll(x, 16, 1))
        y = x * cos + rot * sin
        qkv_ref[:, cs] = ((y * q_scale) if s < D_HEADS else y).astype(BF16)
    vs = slice(2 * D_HEADS * HEAD_DIM, QKV_COLS)
    qkv_ref[:, vs] = qkv[:, vs].astype(BF16)


def _in_call(x, nw, mod, w_all, cos_t, sin_t, cfg):
    t, d = x.shape
    tm = IN_TILE
    row = functools.partial(_mod_row, cfg=cfg, tm=tm)
    resident = lambda shape: pl.BlockSpec(shape, lambda i: (0, 0), pipeline_mode=pl.Buffered(1))
    return pl.pallas_call(
        _in_kernel,
        grid=(t // tm,),
        in_specs=[pl.BlockSpec((tm, d), lambda i: (i, 0)),
                  resident((1, d)),
                  pl.BlockSpec((1, 1, d), lambda i: (row(i) * 6 + 0, 0, 0)),
                  pl.BlockSpec((1, 1, d), lambda i: (row(i) * 6 + 1, 0, 0)),
                  resident((d, P_COLS + QKV_COLS + HEAD_DIM)),
                  pl.BlockSpec((tm, HEAD_DIM), lambda i: (i, 0)),
                  pl.BlockSpec((tm, HEAD_DIM), lambda i: (i, 0))],
        out_specs=[pl.BlockSpec((tm, P_COLS), lambda i: (i, 0)),
                   pl.BlockSpec((tm, QKV_COLS), lambda i: (i, 0)),
                   pl.BlockSpec((tm, HEAD_DIM), lambda i: (i, 0))],
        out_shape=[jax.ShapeDtypeStruct((t, P_COLS), F32), jax.ShapeDtypeStruct((t, QKV_COLS), BF16),
                   jax.ShapeDtypeStruct((t, HEAD_DIM), F32)],
        compiler_params=_cparams(1),
        name="in_proj",
    )(x, nw, mod, mod, w_all, cos_t, sin_t)


CONV_TILE = 256


def _conv_kernel(xq_ref, xk_ref, pq_ref, pk_ref, nq_ref, nk_ref, w_ref, b_ref, q_ref, k_ref, *,
                 starts, ends):
    i = pl.program_id(0)
    is_start = functools.reduce(jnp.logical_or, [i == s for s in starts])
    is_end = functools.reduce(jnp.logical_or, [i == s for s in ends])
    tr = CONV_TILE
    rows = lax.broadcasted_iota(jnp.int32, (tr, 512), 0)

    def conv(x_ref, p_ref, n_ref, half):
        x = x_ref[...]
        prev_row = jnp.where(is_start, 0.0, p_ref[7:8, :])
        next_row = jnp.where(is_end, 0.0, n_ref[0:1, :])
        xp = jnp.where(rows == 0, prev_row, pltpu.roll(x, 1, 0))
        xn = jnp.where(rows == tr - 1, next_row, pltpu.roll(x, tr - 1, 0))
        lo, hi = half * 512, (half + 1) * 512
        y = xp * w_ref[0:1, lo:hi] + x * w_ref[1:2, lo:hi] + xn * w_ref[2:3, lo:hi] + b_ref[:, lo:hi]
        return _silu(y)

    q_ref[...] = conv(xq_ref, pq_ref, nq_ref, 0) * (HEAD_DIM ** -0.5)
    k_ref[...] = conv(xk_ref, pk_ref, nk_ref, 1)


def _conv_call(p, conv_w, conv_b, cfg):
    t = p.shape[0]
    tr = CONV_TILE
    nt = t // tr
    lt, ct = cfg.n_lat // tr, cfg.n_ctx // tr
    seg_first = [b * lt for b in range(cfg.B)] + [cfg.B * lt + b * ct for b in range(cfg.B)]
    seg_last = [b * lt + lt - 1 for b in range(cfg.B)] + [cfg.B * lt + b * ct + ct - 1 for b in range(cfg.B)]
    r8 = tr // 8
    last8 = t // 8 - 1
    kern = functools.partial(_conv_kernel, starts=tuple(seg_first), ends=tuple(seg_last))
    prev = lambda c: (lambda i: (jnp.maximum(i * r8 - 1, 0), c))
    nxt = lambda c: (lambda i: (jnp.minimum((i + 1) * r8, last8), c))
    return pl.pallas_call(
        kern,
        grid=(nt,),
        in_specs=[pl.BlockSpec((tr, 512), lambda i: (i, 0)),
                  pl.BlockSpec((tr, 512), lambda i: (i, 1)),
                  pl.BlockSpec((8, 512), prev(0)), pl.BlockSpec((8, 512), prev(1)),
                  pl.BlockSpec((8, 512), nxt(0)), pl.BlockSpec((8, 512), nxt(1)),
                  pl.BlockSpec((3, 1024), lambda i: (0, 0)),
                  pl.BlockSpec((1, 1024), lambda i: (0, 0))],
        out_specs=[pl.BlockSpec((tr, 512), lambda i: (i, 0)), pl.BlockSpec((tr, 512), lambda i: (i, 0))],
        out_shape=[jax.ShapeDtypeStruct((t, 512), F32)] * 2,
        compiler_params=_cparams(1),
        name="mlstm_conv",
    )(p, p, p, p, p, p, conv_w, conv_b.reshape(1, 1024))


M_HEADS = 4


def _split3(x):
    hi = x.astype(BF16)
    r1 = x - hi.astype(F32)
    mid = r1.astype(BF16)
    return hi, mid, (r1 - mid.astype(F32)).astype(BF16)


def _cumsum_matmul(mask, x):
    return sum(jnp.dot(mask, part, preferred_element_type=F32) for part in _split3(x))


def _mlstm_body(qf, kf, vf, gf, qb, kb, vb, gb, brow, bcol, of, ob, c_scr, m_scr):
    L = CHUNK
    row = lax.broadcasted_iota(jnp.int32, (L, L), 0)
    col = lax.broadcasted_iota(jnp.int32, (L, L), 1)
    neg_inf = jnp.float32(-jnp.inf)

    gcol = jnp.concatenate([gf[...], gb[...]], axis=0) + brow[...]
    grow = jnp.concatenate([gf[...], gb[...]], axis=0).T + bcol[...]
    r8 = lax.broadcasted_iota(jnp.int32, (8, 2 * L), 0)
    l8 = lax.broadcasted_iota(jnp.int32, (8, 2 * L), 1)
    fwd_row = r8 < M_HEADS
    own = (r8 // M_HEADS) == (l8 // L)
    ig = jnp.concatenate([grow[0:4], grow[8:12]], axis=0)
    logf = _log_sigmoid(jnp.concatenate([grow[4:8], grow[12:16]], axis=0))
    tl = lax.broadcasted_iota(jnp.int32, (2 * L, 2 * L), 0)
    ti = lax.broadcasted_iota(jnp.int32, (2 * L, 2 * L), 1)
    same_half = (tl // L) == (ti // L)
    scan_rows = jnp.logical_and(same_half, jnp.where(ti < L, tl - ti, ti - tl) <= 0)
    scan_cols = jnp.logical_and(same_half, jnp.where(tl < L, ti - tl, tl - ti) <= 0)
    cf = sum(jnp.dot(part, scan_rows.astype(BF16), preferred_element_type=F32) for part in _split3(logf))
    u = jnp.where(own, ig - cf, neg_inf)
    m_st = m_scr[...]
    end_lane = jnp.where(fwd_row, L - 1, L)
    f_end = jnp.broadcast_to(jnp.sum(jnp.where(l8 == end_lane, cf, 0.0), axis=-1, keepdims=True),
                             (8, 2 * L))
    dec = jnp.where(own, f_end - cf + ig, neg_inf)
    m_new = jnp.maximum(f_end + m_st, jnp.max(dec, axis=-1, keepdims=True))
    a_prev = jnp.exp(f_end + m_st - m_new)
    m_scr[...] = m_new
    u_keys = jnp.where(fwd_row, u, pltpu.roll(u, L, 1))
    cum_col = _cumsum_matmul(scan_cols.astype(BF16), _log_sigmoid(gcol))

    rep = lambda colv: jnp.broadcast_to(colv, (L, HEAD_DIM))
    dir_refs = ((qf, kf, vf, of), (qb, kb, vb, ob))
    pairs = [(d, h) for d in range(2) for h in range(M_HEADS)]
    work = []
    for d, h in pairs:
        q_ref, k_ref, v_ref, _ = dir_refs[d]
        hs = slice(h * HEAD_DIM, (h + 1) * HEAD_DIM)
        q = q_ref[:, hs].astype(BF16)
        k32 = k_ref[:, hs]
        v1 = jnp.concatenate([v_ref[:, hs].astype(BF16), jnp.ones((L, HEAD_DIM), BF16)], axis=1)
        cn = c_scr[d, h]
        qk = lax.dot_general(q, k32.astype(BF16), (((1,), (1,)), ((), ())), preferred_element_type=F32)
        r_state = jnp.dot(q, cn.astype(BF16), preferred_element_type=F32)
        work.append((k32, v1, cn, qk, r_state))
    yield
    for (d, h), (k32, v1, cn, qk, r_state) in zip(pairs, work):
        o_ref = dir_refs[d][3]
        valid = (col <= row) if d == 0 else (col >= row)
        ts = slice(d * L, (d + 1) * L)
        r = d * M_HEADS + h
        ci, cfc = d * 8 + h, d * 8 + 4 + h
        hs = slice(h * HEAD_DIM, (h + 1) * HEAD_DIM)
        u_tile = jnp.where(valid, u_keys[r:r + 1, 0:L], neg_inf)
        m_rep = rep(jnp.maximum(jnp.max(u_tile, axis=-1, keepdims=True), m_st[r:r + 1, 0:1]))
        cf_rep = rep(cum_col[ts, cfc:cfc + 1])
        ig_rep = rep(gcol[ts, ci:ci + 1])
        w_inter = jnp.exp(m_st[r:r + 1] - m_rep)
        e = jnp.exp(u_tile - m_rep[:, 0:L])
        r_chunk = jnp.dot((qk * e).astype(BF16), v1, preferred_element_type=F32)
        num = w_inter * r_state[:, 0:HEAD_DIM] + r_chunk[:, 0:HEAD_DIM]
        den = w_inter * r_state[:, HEAD_DIM:] + r_chunk[:, HEAD_DIM:]
        o_ref[:, hs] = num / jnp.maximum(jnp.abs(den), jnp.exp(-(cf_rep + m_rep)))

        ws = jnp.exp(f_end[r:r + 1] - cf_rep + ig_rep - m_new[r:r + 1])
        wk = (ws * k32).astype(BF16)
        decay = jnp.concatenate([a_prev[r:r + 1], a_prev[r:r + 1]], axis=1)
        c_scr[d, h] = decay * cn + lax.dot_general(wk, v1, (((0,), (0,)), ((), ())),
                                                   preferred_element_type=F32)


N_MLSTM_REFS = 8
N_GLA_REFS = 6


def _scan_kernel(*refs, n_b):
    it = iter(refs)
    take = lambda n: [next(it) for _ in range(n)]
    m_in = [take(N_MLSTM_REFS) for _ in range(n_b)]
    brow, bcol = take(2)
    g_in = [take(N_GLA_REFS) for _ in range(n_b)]
    w2_ref, b2_ref = take(2)
    m_of, m_ob, g_of, g_ob = take(4)
    c_scr, m_scr, s_scr, gl_scr = take(4)

    @pl.when(pl.program_id(0) == 0)
    def _():
        c_scr[...] = jnp.zeros_like(c_scr)
        m_scr[...] = jnp.zeros_like(m_scr)
        s_scr[...] = jnp.zeros_like(s_scr)

    bodies = []
    for b in range(n_b):
        bodies.append(_mlstm_body(*m_in[b], brow, bcol, m_of.at[0, b], m_ob.at[0, b], c_scr.at[b], m_scr.at[b]))
        bodies.append(_gla_body(*g_in[b], w2_ref, b2_ref, g_of.at[0, b], g_ob.at[0, b], s_scr.at[b], gl_scr.at[b]))
    while bodies:
        bodies = [body for body in bodies if next(body, "done") != "done"]


def _scan_call(mq, mk, p, ps, gate_b, w2p, b2p, cfg):
    nc = (cfg.n_ctx + cfg.n_lat) // CHUNK
    fwd = lambda b, col: (lambda s: (_chunk_block(b, s, cfg), col))
    bwd = lambda b, col: (lambda s: (_chunk_block(b, _bwd_chunk(s, cfg), cfg), col))
    blk = lambda w, f: pl.BlockSpec((CHUNK, w), f)
    brow = jnp.zeros((1, HEAD_DIM), F32).at[0, :16].set(gate_b)
    const = lambda s: (0, 0)
    in_specs, args = [], []
    for b in range(cfg.B):
        for way in (fwd, bwd):
            in_specs += [blk(512, way(b, 0)), blk(512, way(b, 0)), blk(512, way(b, 2)), blk(HEAD_DIM, way(b, 0))]
            args += [mq, mk, p, ps]
    in_specs += [pl.BlockSpec((1, HEAD_DIM), const), pl.BlockSpec((HEAD_DIM, 1), const)]
    args += [brow, brow.reshape(HEAD_DIM, 1)]
    for b in range(cfg.B):
        for way in (fwd, bwd):
            in_specs += [blk(512, way(b, 4)), blk(512, way(b, 5)), blk(HEAD_DIM, way(b, 0))]
            args += [p, p, ps]
    in_specs += [pl.BlockSpec((HEAD_DIM, 2 * G_QK), const), pl.BlockSpec((1, 2 * G_QK), const)]
    args += [w2p, b2p]
    out_f = pl.BlockSpec((1, cfg.B, CHUNK, 512), lambda s: (s, 0, 0, 0))
    out_b = pl.BlockSpec((1, cfg.B, CHUNK, 512), lambda s: (_bwd_chunk(s, cfg), 0, 0, 0))
    outs = pl.pallas_call(
        functools.partial(_scan_kernel, n_b=cfg.B),
        grid=(nc,),
        in_specs=in_specs,
        out_specs=[out_f, out_b, out_f, out_b],
        out_shape=[jax.ShapeDtypeStruct((nc, cfg.B, CHUNK, 512), F32)] * 4,
        scratch_shapes=[pltpu.VMEM((cfg.B, 2, M_HEADS, HEAD_DIM, 2 * HEAD_DIM), F32),
                        pltpu.VMEM((cfg.B, 2 * M_HEADS, 2 * CHUNK), F32),
                        pltpu.VMEM((cfg.B, 2, HEAD_DIM, G_QK), F32),
                        pltpu.VMEM((cfg.B, 2, CHUNK, G_QK), F32)],
        compiler_params=_cparams(1),
        name="recurrent_scan",
    )(*args)
    return outs


G_QK = 256
G_V = 512


G_HEADS = 4
G_DK = 64
GLA_BLK = 16


def _gla_body(qkf, vf, lf, qkb, vb, lb, w2_ref, b2_ref, of, ob, s_scr, gl_scr):
    L, C = CHUNK, GLA_BLK
    nb = L // C
    row = lax.broadcasted_iota(jnp.int32, (L, L), 0)
    col = lax.broadcasted_iota(jnp.int32, (L, L), 1)
    same_blk = (row // C) == (col // C)
    rows8 = lax.broadcasted_iota(jnp.int32, (8, G_QK), 0)
    neg_inf = jnp.float32(-jnp.inf)
    he_r = lax.broadcasted_iota(jnp.int32, (G_QK, G_V), 0) // G_DK
    he_c = lax.broadcasted_iota(jnp.int32, (G_QK, G_V), 1) // HEAD_DIM
    head_expand = (he_r == he_c).astype(BF16)

    def per_head_rows(x):
        lane_head = lax.broadcasted_iota(jnp.int32, x.shape, 1) // G_DK
        return jnp.concatenate([jnp.where(lane_head == h, x, 0.0) for h in range(G_HEADS)], axis=0).astype(BF16)

    def head_blocks(r, n):
        return jnp.concatenate([r[h * n:(h + 1) * n, h * HEAD_DIM:(h + 1) * HEAD_DIM] for h in range(G_HEADS)],
                               axis=1)

    def direction(d, qk_ref, v_ref, l_ref, o_ref):
        cs = slice(d * G_QK, (d + 1) * G_QK)
        z = jnp.dot(l_ref[...], w2_ref[:, cs], precision=HIGHEST, preferred_element_type=F32) + b2_ref[:, cs]
        log_a = _log_sigmoid(z) * (1.0 / GLA_TAU)
        tri = (col <= row) if d == 0 else (col >= row)
        g = _cumsum_matmul(tri.astype(BF16), log_a)
        gl = _cumsum_matmul(jnp.logical_and(tri, same_blk).astype(BF16), log_a)
        gl_scr[d] = gl
        q = qk_ref[:, 0:G_QK] * (G_DK ** -0.5)
        k = qk_ref[:, G_QK:2 * G_QK]
        v = v_ref[...]
        vb16 = v.astype(BF16)
        g_end = g[L - 1:L, :] if d == 0 else g[0:1, :]
        st = s_scr[d]

        r = lax.dot_general(per_head_rows(q * jnp.exp(g)), st.astype(BF16), (((1,), (1,)), ((), ())),
                            preferred_element_type=F32)
        inter = jnp.concatenate([r[h * L:(h + 1) * L] for h in range(G_HEADS)], axis=1)
        yield

        ql = q * jnp.exp(gl)
        first = lambda b: b * C + (C - 1 if d == 0 else 0)
        tot = [gl[first(b):first(b) + 1] for b in range(nb)]
        k_end = [k[b * C:(b + 1) * C] * jnp.exp(tot[b] - gl[b * C:(b + 1) * C]) for b in range(nb)]

        atts, v_cats = {}, {}
        for bi in range(nb):
            earlier = list(range(bi)) if d == 0 else list(range(bi + 1, nb))
            if not earlier:
                continue
            ks, vs = [], []
            for bj in earlier:
                between = range(bj + 1, bi) if d == 0 else range(bi + 1, bj)
                kj = k_end[bj]
                if len(between):
                    kj = kj * jnp.exp(functools.reduce(jnp.add, [tot[m] for m in between]))
                ks.append(kj)
                vs.append(vb16[bj * C:(bj + 1) * C])
            v_cats[bi] = jnp.concatenate(vs, axis=0)
            atts[bi] = lax.dot_general(per_head_rows(ql[bi * C:(bi + 1) * C]),
                                       jnp.concatenate(ks, axis=0).astype(BF16), (((1,), (1,)), ((), ())),
                                       preferred_element_type=F32)

        k_hat = per_head_rows(k * jnp.exp(g_end - g))
        v_rows = jnp.concatenate([vb16[:, h * HEAD_DIM:(h + 1) * HEAD_DIM] for h in range(G_HEADS)], axis=0)
        s_new = st * jnp.exp(g_end) + lax.dot_general(v_rows, k_hat, (((0,), (0,)), ((), ())),
                                                      preferred_element_type=F32)
        yield

        pieces, spans = [], []
        for i in range(L):
            bi, il = divmod(i, C)
            lo, hi = ((il // 8) * 8, C) if d == 0 else (0, (il // 8) * 8 + 8)
            edge = (lo, lo + 8) if d == 0 else (hi - 8, hi)
            k_i = qk_ref[i:i + 1, G_QK:2 * G_QK]
            g_i = gl_scr[d, i:i + 1, :]
            vis = (rows8 >= il % 8) if d == 0 else (rows8 <= il % 8)
            diff = jnp.where(vis, gl[bi * C + edge[0]:bi * C + edge[1]] - g_i, neg_inf)
            if hi - lo > 8:
                other = gl[bi * C + 8:bi * C + 16] if d == 0 else gl[bi * C:bi * C + 8]
                diff = jnp.concatenate([diff, other - g_i] if d == 0 else [other - g_i, diff], axis=0)
            pieces.append((q[bi * C + lo:bi * C + hi] * k_i) * jnp.exp(diff))
            spans.append((lo, hi))
        w_all = jnp.dot(jnp.concatenate(pieces, axis=0).astype(BF16), head_expand, preferred_element_type=F32)
        yield
        cross = {bi: head_blocks(jnp.dot(att.astype(BF16), v_cats[bi], preferred_element_type=F32), C)
                 for bi, att in atts.items()}
        yield

        blocks = []
        w_off = 0
        for bi in range(nb):
            acc = inter[bi * C:(bi + 1) * C]
            if bi in cross:
                acc = acc + cross[bi]
            for il in range(C):
                i = bi * C + il
                lo, hi = spans[i]
                upd = acc[lo:hi] + w_all[w_off:w_off + hi - lo] * v_ref[i:i + 1, :]
                w_off += hi - lo
                parts = ([acc[:lo]] if lo > 0 else []) + [upd] + ([acc[hi:]] if hi < C else [])
                acc = jnp.concatenate(parts, axis=0) if len(parts) > 1 else upd
            blocks.append(acc)
        o_ref[...] = jnp.concatenate(blocks, axis=0)
        s_scr[d] = s_new

    stages = [direction(0, qkf, vf, lf, of), direction(1, qkb, vb, lb, ob)]
    for _ in range(5):
        for stage in stages:
            next(stage, None)
        yield


ATT_SUB_ROWS = 256
ATT_TQ = 1024
ATT_KEY_CHUNK = 1024


def _attn_kernel(*refs, has_lat, lam_init):
    if has_lat:
        q_ref, kc_ref, vc_ref, kl_ref, vl_ref, dl_ref, sub_ref, o_ref, s_scr, vo_scr = refs
        kv = ((kc_ref, vc_ref), (kl_ref, vl_ref))
    else:
        q_ref, kc_ref, vc_ref, dl_ref, sub_ref, o_ref, s_scr, vo_scr = refs
        kv = ((kc_ref, vc_ref),)

    @pl.when(pl.program_id(2) == 0)
    def _():
        off = 0
        for _, v_ref in kv:
            n = v_ref.shape[0]
            vo_scr[off:off + n, 0:HEAD_DIM] = v_ref[...]
            vo_scr[off:off + n, HEAD_DIM:2 * HEAD_DIM] = jnp.ones((n, HEAD_DIM), BF16)
            off += n

    n_sub = q_ref.shape[0] // ATT_SUB_ROWS
    lane = lax.broadcasted_iota(jnp.int32, (ATT_SUB_ROWS, HEAD_DIM), 1)
    zero = jnp.zeros((ATT_SUB_ROWS, HEAD_DIM), BF16)
    dl = dl_ref[...]
    lam = (jnp.exp(jnp.sum(dl[0:1] * dl[1:2], axis=-1, keepdims=True))
           - jnp.exp(jnp.sum(dl[2:3] * dl[3:4], axis=-1, keepdims=True)) + lam_init)
    key_chunks, off = [], 0
    for k_ref, _ in kv:
        for c0 in range(0, k_ref.shape[0], ATT_KEY_CHUNK):
            n = min(ATT_KEY_CHUNK, k_ref.shape[0] - c0)
            key_chunks.append((k_ref, c0, off + c0, n))
        off += k_ref.shape[0]

    prev = None
    for sb in range(n_sub + 1):
        if sb < n_sub:
            q = q_ref[sb * ATT_SUB_ROWS:(sb + 1) * ATT_SUB_ROWS, :]
            qms = (jnp.where(lane < 64, q, zero), jnp.where(lane >= 64, q, zero))
            part_max = [None, None]
        accs = [None, None]
        for k_ref, c0, off, n in key_chunks:
            for m in range(2):
                if sb < n_sub:
                    s = lax.dot_general(qms[m], k_ref[c0:c0 + n, :], (((1,), (1,)), ((), ())),
                                        preferred_element_type=F32)
                    s_scr[2 * sb + m, :, off:off + n] = s
                    folded = functools.reduce(jnp.maximum, [s[:, j:j + HEAD_DIM] for j in range(0, n, HEAD_DIM)])
                    part_max[m] = folded if part_max[m] is None else jnp.maximum(part_max[m], folded)
                if prev is not None:
                    p = jnp.exp2(s_scr[2 * (sb - 1) + m, :, off:off + n] - prev[m]).astype(BF16)
                    part = jnp.dot(p, vo_scr[off:off + n, :], preferred_element_type=F32)
                    accs[m] = part if accs[m] is None else accs[m] + part
        if prev is not None:
            outs = [a[:, 0:HEAD_DIM] / a[:, HEAD_DIM:HEAD_DIM + 1] for a in accs]
            out = outs[0] - lam * outs[1]
            ms = jnp.mean(out * out, axis=-1, keepdims=True)
            o_ref[(sb - 1) * ATT_SUB_ROWS:sb * ATT_SUB_ROWS, :] = (
                (out * lax.rsqrt(ms + NORM_EPS) * sub_ref[...]) * (1.0 - lam_init))
        prev = [jnp.max(pm, axis=-1, keepdims=True) for pm in part_max] if sb < n_sub else None


def _attn_call(qkv, d_lam, d_subln, lam_init, cfg, latent):
    nlb = cfg.n_lat // ATT_TQ
    ctx_row0 = cfg.B * cfg.n_lat // cfg.n_ctx
    kern = functools.partial(_attn_kernel, has_lat=latent, lam_init=lam_init)
    n_keys = cfg.n_ctx + (cfg.n_lat if latent else 0)
    kc = pl.BlockSpec((cfg.n_ctx, HEAD_DIM), lambda b, h, i: (ctx_row0 + b, D_HEADS + h))
    vc = pl.BlockSpec((cfg.n_ctx, HEAD_DIM), lambda b, h, i: (ctx_row0 + b, 2 * D_HEADS + h))
    small = [pl.BlockSpec((4, 64), lambda b, h, i: (0, 0)), pl.BlockSpec((1, HEAD_DIM), lambda b, h, i: (0, 0))]
    if latent:
        tq = ATT_TQ
        grid = (cfg.B, D_HEADS, nlb)
        q_spec = pl.BlockSpec((tq, HEAD_DIM), lambda b, h, i: (b * nlb + i, h))
        kv = [kc, vc,
              pl.BlockSpec((cfg.n_lat, HEAD_DIM), lambda b, h, i: (b, D_HEADS + h)),
              pl.BlockSpec((cfg.n_lat, HEAD_DIM), lambda b, h, i: (b, 2 * D_HEADS + h))]
        out_spec = pl.BlockSpec((tq, HEAD_DIM), lambda b, h, i: (b * nlb + i, h))
        n_rows = cfg.TL
    else:
        tq = cfg.n_ctx
        grid = (cfg.B, D_HEADS, 1)
        q_spec = pl.BlockSpec((tq, HEAD_DIM), lambda b, h, i: (ctx_row0 + b, h))
        kv = [kc, vc]
        out_spec = pl.BlockSpec((tq, HEAD_DIM), lambda b, h, i: (b, h))
        n_rows = cfg.B * cfg.n_ctx
    args = [qkv] * (1 + len(kv)) + [d_lam, d_subln.reshape(1, HEAD_DIM)]
    return pl.pallas_call(
        kern,
        grid=grid,
        in_specs=[q_spec] + kv + small,
        out_specs=out_spec,
        out_shape=jax.ShapeDtypeStruct((n_rows, D_HEADS * HEAD_DIM), F32),
        scratch_shapes=[pltpu.VMEM((2 * tq // ATT_SUB_ROWS, ATT_SUB_ROWS, n_keys), F32),
                        pltpu.VMEM((n_keys, 2 * HEAD_DIM), BF16)],
        compiler_params=_cparams(3),
        name="diff_attn_lat" if latent else "diff_attn_ctx",
    )(*args)


OUT_TILE = 256


def _group_rmsnorm(x, w, groups):
    parts = []
    for gi in range(groups):
        xs = x[:, gi * HEAD_DIM:(gi + 1) * HEAD_DIM]
        ms = jnp.mean(xs * xs, axis=-1, keepdims=True)
        parts.append(xs * lax.rsqrt(ms + NORM_EPS) * w[:, gi * HEAD_DIM:(gi + 1) * HEAD_DIM])
    return jnp.concatenate(parts, axis=-1)


def _out_kernel(*refs, n_lat_tiles, has_ctx):
    if has_ctx:
        x_ref, hmf, hmb, hgf, hgb, hdl, hdc, mo, go, mn, gn, w_ref, nw_ref, gate_ref, o_ref = refs
        hd = jnp.where(pl.program_id(0) < n_lat_tiles, hdl[...], hdc[...])
    else:
        x_ref, hmf, hmb, hgf, hgb, hdl, mo, go, mn, gn, w_ref, nw_ref, gate_ref, o_ref = refs
        hd = hdl[...]
    rows = lambda ref: ref[...].reshape(OUT_TILE, 512)
    ym = _group_rmsnorm(rows(hmf) + rows(hmb), mn[...], 4) * jax.nn.sigmoid(mo[...])
    yg = _group_rmsnorm(rows(hgf) + rows(hgb), gn[...], 4) * _silu(go[...])
    y = jnp.concatenate([ym.astype(BF16), yg.astype(BF16), hd.astype(BF16)], axis=-1)
    z = jnp.dot(y, w_ref[...], preferred_element_type=F32)
    ms = jnp.mean(z * z, axis=-1, keepdims=True)
    o_ref[...] = x_ref[...] + gate_ref[0] * (z * lax.rsqrt(ms + NORM_EPS) * nw_ref[...])


def _out_call(x, hmf, hmb, hgf, hgb, hd_lat, hd_ctx, p, m_norm, g_norm, w_out, nw, mod, cfg, n_rows):
    d = cfg.D
    tm = OUT_TILE
    nl = cfg.TL // tm
    row = functools.partial(_mod_row, cfg=cfg, tm=tm)
    rt = lambda w, c: pl.BlockSpec((tm, w), lambda i: (i, c))
    const = lambda i: (0, 0)
    lt, ct = cfg.n_lat // tm, cfg.n_ctx // tm

    def scan_index(i):
        k = i - cfg.B * lt
        return (jnp.where(k < 0, ct + i % lt, k % ct), jnp.where(k < 0, i // lt, k // ct), 0, 0)

    scan_blk = pl.BlockSpec((tm // CHUNK, 1, CHUNK, 512), scan_index)
    hd_specs = [pl.BlockSpec((tm, 1024), lambda i: (jnp.minimum(i, nl - 1), 0))]
    hd_args = [hd_lat]
    if hd_ctx is not None:
        hd_specs.append(pl.BlockSpec((tm, 1024), lambda i: (jnp.maximum(i - nl, 0), 0)))
        hd_args.append(hd_ctx)
    return pl.pallas_call(
        functools.partial(_out_kernel, n_lat_tiles=nl, has_ctx=hd_ctx is not None),
        grid=(n_rows // tm,),
        in_specs=[rt(d, 0), scan_blk, scan_blk, scan_blk, scan_blk] + hd_specs + [
                  rt(512, 3), rt(512, 6),
                  pl.BlockSpec((1, 512), const), pl.BlockSpec((1, 512), const),
                  pl.BlockSpec((d, d), const), pl.BlockSpec((1, d), const),
                  pl.BlockSpec((1, 1, d), lambda i: (row(i) * 6 + 2, 0, 0))],
        out_specs=rt(d, 0),
        out_shape=jax.ShapeDtypeStruct((n_rows, d), F32),
        compiler_params=_cparams(1),
        name="out_proj",
    )(x, hmf, hmb, hgf, hgb, *hd_args, p, p, m_norm, g_norm, w_out, nw, mod)


def _ffn_kernel(x_ref, nw_ref, shift_ref, scale_ref, wg_ref, wu_ref, wd_ref, pw_ref, gate_ref, o_ref,
                h_scr, acc_scr):
    j = pl.program_id(1)

    @pl.when(j == 0)
    def _():
        h_scr[...] = _prenorm(x_ref[...], nw_ref[...], shift_ref[0], scale_ref[0]).astype(BF16)
        acc_scr[...] = jnp.zeros_like(acc_scr)

    h = h_scr[...]
    a = jnp.dot(h, wg_ref[...], preferred_element_type=F32)
    u = jnp.dot(h, wu_ref[...], preferred_element_type=F32)
    acc_scr[...] += jnp.dot((_silu(a) * u).astype(BF16), wd_ref[...], preferred_element_type=F32)

    @pl.when(j == pl.num_programs(1) - 1)
    def _():
        z = acc_scr[...]
        ms = jnp.mean(z * z, axis=-1, keepdims=True)
        o_ref[...] = x_ref[...] + gate_ref[0] * (z * lax.rsqrt(ms + NORM_EPS) * pw_ref[...])


def _ffn_call(x, nw_pre, nw_post, mod, wg, wu, wd, cfg, n_rows, tm=512, tf=512):
    d, f = wg.shape
    row = functools.partial(_mod_row, cfg=cfg, tm=tm)
    const = lambda i, j: (0, 0)
    modspec = lambda kk: pl.BlockSpec((1, 1, d), lambda i, j: (row(i) * 6 + kk, 0, 0))
    return pl.pallas_call(
        _ffn_kernel,
        grid=(n_rows // tm, f // tf),
        in_specs=[pl.BlockSpec((tm, d), lambda i, j: (i, 0)),
                  pl.BlockSpec((1, d), const), modspec(3), modspec(4),
                  pl.BlockSpec((d, tf), lambda i, j: (0, j)),
                  pl.BlockSpec((d, tf), lambda i, j: (0, j)),
                  pl.BlockSpec((tf, d), lambda i, j: (j, 0)),
                  pl.BlockSpec((1, d), const), modspec(5)],
        out_specs=pl.BlockSpec((tm, d), lambda i, j: (i, 0)),
        out_shape=jax.ShapeDtypeStruct((n_rows, d), F32),
        scratch_shapes=[pltpu.VMEM((tm, d), BF16), pltpu.VMEM((tm, d), F32)],
        compiler_params=_cparams(2),
        name="ffn",
    )(x, nw_pre, mod, mod, wg, wu, wd, nw_post, mod)


_MIX = {}
_off = 0
for _name, _w in (("m_q", 512), ("m_k", 512), ("m_v", 512), ("m_o", 512), ("m_gates", 16),
                  ("g_q", 256), ("g_k", 256), ("g_v", 512), ("g_out", 512), ("g_lr", 32),
                  ("d_q", 1024), ("d_k", 1024), ("d_v", 1024)):
    _MIX[_name] = (_off, _w)
    _off += _w
_P_ORDER = ("m_q", "m_k", "m_v", "m_o", "g_q", "g_k", "g_v", "g_out")
_QKV_ORDER = ("d_q", "d_k", "d_v")


def _split_w_in(w_in):
    cols = lambda n: w_in[:, _MIX[n][0]:_MIX[n][0] + _MIX[n][1]]
    pad = jnp.zeros((w_in.shape[0], HEAD_DIM - 48), w_in.dtype)
    groups = [cols(n) for n in _P_ORDER + _QKV_ORDER] + [cols("m_gates"), cols("g_lr"), pad]
    return jnp.concatenate(groups, axis=1).astype(BF16)


def _rope_tables(cfg):
    rows = cfg.n_lat // GRID_W
    r = np.repeat(np.arange(rows, dtype=np.float64), GRID_W)
    c = np.tile(np.arange(GRID_W, dtype=np.float64), rows)
    half = 16
    inv_freq = ROPE_BASE ** (-np.arange(half, dtype=np.float64) / half)
    ang_r, ang_c = r[:, None] * inv_freq, c[:, None] * inv_freq
    ang = np.concatenate([ang_r, ang_r, ang_c, ang_c], axis=-1)
    ang = np.tile(ang, (cfg.B, 2))
    n_c = cfg.B * cfg.n_ctx
    cos_t = np.concatenate([np.cos(ang), np.ones((n_c, HEAD_DIM))], axis=0).astype(np.float32)
    sin_t = np.concatenate([np.sin(ang), np.zeros((n_c, HEAD_DIM))], axis=0).astype(np.float32)
    return jnp.asarray(cos_t), jnp.asarray(sin_t)


def _layer(xt, mod, lw, lam_init, rope, cfg, need_ctx):
    d = cfg.D
    p, qkv, ps = _in_call(xt, lw["norm_mix_pre"].reshape(1, d), mod, _split_w_in(lw["w_in"]), *rope, cfg)

    mq, mk = _conv_call(p, lw["mlstm_conv_w"], lw["mlstm_conv_b"], cfg)
    w2 = lw["gla_gate_w2"]
    w2p = jnp.zeros((HEAD_DIM, 2 * G_QK), F32)
    w2p = w2p.at[16:32, 0:G_QK].set(w2[0]).at[32:48, G_QK:].set(w2[1])
    hmf, hmb, hgf, hgb = _scan_call(mq, mk, p, ps, lw["mlstm_gate_b"], w2p,
                                    lw["gla_gate_b"].reshape(1, 2 * G_QK), cfg)

    hd_lat = _attn_call(qkv, lw["diff_lambda"], lw["diff_subln"], lam_init, cfg, latent=True)
    hd_ctx = _attn_call(qkv, lw["diff_lambda"], lw["diff_subln"], lam_init, cfg, latent=False) if need_ctx else None

    n_rows = cfg.T if need_ctx else cfg.TL
    xt = _out_call(xt, hmf, hmb, hgf, hgb, hd_lat, hd_ctx, p, lw["mlstm_norm"].reshape(1, 512),
                   lw["gla_norm"].reshape(1, 512), lw["w_out"].astype(BF16),
                   lw["norm_mix_post"].reshape(1, d), mod, cfg, n_rows)
    xt = _ffn_call(xt, lw["norm_ffn_pre"].reshape(1, d), lw["norm_ffn_post"].reshape(1, d), mod,
                   lw["w_ffn_gate"].astype(BF16), lw["w_ffn_up"].astype(BF16), lw["w_ffn_down"].astype(BF16),
                   cfg, n_rows)
    return xt


_LAYER_KEYS = ("norm_mix_pre", "norm_mix_post", "norm_ffn_pre", "norm_ffn_post", "w_in", "mlstm_conv_w",
               "mlstm_conv_b", "mlstm_gate_b", "mlstm_norm", "gla_gate_w2", "gla_gate_b", "gla_norm",
               "diff_lambda", "diff_subln", "w_out", "w_ffn_gate", "w_ffn_up", "w_ffn_down")


def kernel(x, c, ctx, c_ctx, w_mod, b_mod, norm_mix_pre, norm_mix_post, norm_ffn_pre, norm_ffn_post, w_in, mlstm_conv_w, mlstm_conv_b, mlstm_gate_b, mlstm_norm, gla_gate_w2, gla_gate_b, gla_norm, diff_lambda, diff_subln, w_out, w_ffn_gate, w_ffn_up, w_ffn_down):
    weights = dict(zip(_LAYER_KEYS, (norm_mix_pre, norm_mix_post, norm_ffn_pre, norm_ffn_post, w_in,
                                     mlstm_conv_w, mlstm_conv_b, mlstm_gate_b, mlstm_norm, gla_gate_w2,
                                     gla_gate_b, gla_norm, diff_lambda, diff_subln, w_out, w_ffn_gate,
                                     w_ffn_up, w_ffn_down)))
    b, n_lat, d = x.shape
    cfg = Cfg(B=b, n_ctx=ctx.shape[1], n_lat=n_lat, D=d, F=w_ffn_gate.shape[-1])
    depth = w_mod.shape[0]
    c8 = jnp.zeros((8, d), F32).at[0].set(c_ctx).at[1:1 + b].set(c)
    mods = _mod_call(c8, w_mod, b_mod).reshape(depth, 8 * 6, 1, d)
    rope = _rope_tables(cfg)
    xt = jnp.concatenate([x.reshape(b * n_lat, d), ctx.reshape(b * ctx.shape[1], d)], axis=0)
    for layer in range(depth):
        lw = {k: v[layer] for k, v in weights.items()}
        lam_init = 0.8 - 0.6 * math.exp(-0.3 * layer)
        xt = _layer(xt, mods[layer], lw, lam_init, rope, cfg, need_ctx=layer < depth - 1)
    return xt.reshape(b, n_lat, d)
```

```python
import dataclasses
import functools
import math

import jax
import jax.numpy as jnp
import numpy as np
from jax import lax
from jax.experimental import pallas as pl
from jax.experimental.pallas import tpu as pltpu

F32 = jnp.float32
BF16 = jnp.bfloat16
NORM_EPS = 1e-6
CHUNK = 64
GRID_W = 64
ROPE_BASE = 10000.0
GLA_TAU = 16.0
HEAD_DIM = 128
VMEM_LIMIT_BYTES = 56 * 1024 * 1024
HIGHEST = lax.Precision.HIGHEST


@dataclasses.dataclass(frozen=True)
class Cfg:
    B: int = 2
    n_ctx: int = 256
    n_lat: int = 4096
    D: int = 2048
    F: int = 5632

    @property
    def T(self):
        return self.B * (self.n_ctx + self.n_lat)

    @property
    def TL(self):
        return self.B * self.n_lat


def _cparams(n_axes):
    return pltpu.CompilerParams(dimension_semantics=("arbitrary",) * n_axes,
                                vmem_limit_bytes=VMEM_LIMIT_BYTES)


def _mod_row(i, cfg, tm):
    lt = cfg.n_lat // tm
    return jnp.where(i < cfg.B * lt, 1 + i // lt, 0)


def _chunk_block(b, c, cfg):
    ncc, ncl = cfg.n_ctx // CHUNK, cfg.n_lat // CHUNK
    return jnp.where(c < ncc, cfg.B * ncl + b * ncc + c, b * ncl + (c - ncc))


def _bwd_chunk(s, cfg):
    ncc, ncl = cfg.n_ctx // CHUNK, cfg.n_lat // CHUNK
    return jnp.where(s < ncc, ncc - 1 - s, ncc + ncl - 1 - (s - ncc))


def _log_sigmoid(x):
    return jnp.minimum(x, 0.0) - jnp.log1p(jnp.exp(-jnp.abs(x)))


def _silu(x):
    return x * jax.nn.sigmoid(x)


def _mod_kernel(c_ref, w_ref, b_ref, o_ref):
    s = _silu(c_ref[...]).astype(BF16)
    o_ref[0] = jnp.dot(s, w_ref[0].astype(BF16), preferred_element_type=F32) + b_ref[0]


def _mod_call(c8, w_mod, b_mod, tn=2048):
    depth, d, n = w_mod.shape
    return pl.pallas_call(
        _mod_kernel,
        grid=(depth, n // tn),
        in_specs=[pl.BlockSpec((8, d), lambda l, j: (0, 0)),
                  pl.BlockSpec((1, d, tn), lambda l, j: (l, 0, j)),
                  pl.BlockSpec((1, 1, tn), lambda l, j: (l, 0, j))],
        out_specs=pl.BlockSpec((1, 8, tn), lambda l, j: (l, 0, j)),
        out_shape=jax.ShapeDtypeStruct((depth, 8, n), F32),
        compiler_params=_cparams(2),
        name="adaln_mod",
    )(c8, w_mod, b_mod.reshape(depth, 1, n))


def _prenorm(x, nw, shift, scale):
    ms = jnp.mean(x * x, axis=-1, keepdims=True)
    return (x * lax.rsqrt(ms + NORM_EPS) * nw) * (1.0 + scale) + shift


D_HEADS = 8
P_COLS = 3584
QKV_COLS = 3072
IN_TILE = 256


def _token_specs(x_parts, tm, cfg):
    d = cfg.D
    if len(x_parts) == 1:
        return [pl.BlockSpec((tm, d), lambda i: (i, 0))]
    nl = cfg.TL // tm
    return [pl.BlockSpec((tm, d), lambda i: (jnp.minimum(i, nl - 1), 0)),
            pl.BlockSpec((tm, d), lambda i: (jnp.maximum(i - nl, 0), 0))]


def _token_tile(x_refs, n_lat_tiles):
    if len(x_refs) == 1:
        return x_refs[0][...]
    return jnp.where(pl.program_id(0) < n_lat_tiles, x_refs[0][...], x_refs[1][...])


def _in_kernel(*refs, n_src, n_lat_tiles):
    x_refs = refs[:n_src]
    nw_ref, shift_ref, scale_ref, wa_ref, wb_ref, ws_ref, cos_ref, sin_ref, p_ref, qkv_ref, ps_ref = refs[n_src:]
    h = _prenorm(_token_tile(x_refs, n_lat_tiles), nw_ref[...], shift_ref[0], scale_ref[0]).astype(BF16)
    ps_ref[...] = jnp.dot(h, ws_ref[...], preferred_element_type=F32)
    p_ref[...] = jnp.dot(h, wa_ref[...], preferred_element_type=F32)
    qkv = jnp.dot(h, wb_ref[...], preferred_element_type=F32)
    lane = lax.broadcasted_iota(jnp.int32, (IN_TILE, HEAD_DIM), 1)
    low = (lane % 32) < 16
    cos, sin = cos_ref[...], sin_ref[...]
    q_scale = 64 ** -0.5 * math.log2(math.e)
    for s in range(2 * D_HEADS):
        cs = slice(s * HEAD_DIM, (s + 1) * HEAD_DIM)
        x = qkv[:, cs]
        rot = jnp.where(low, -pltpu.roll(x, HEAD_DIM - 16, 1), pltpu.roll(x, 16, 1))
        y = x * cos + rot * sin
        qkv_ref[:, cs] = ((y * q_scale) if s < D_HEADS else y).astype(BF16)
    vs = slice(2 * D_HEADS * HEAD_DIM, QKV_COLS)
    qkv_ref[:, vs] = qkv[:, vs].astype(BF16)


def _in_call(x_parts, nw, mod, w_a, w_b, w_small, cos_t, sin_t, cfg):
    t, d = cfg.T, cfg.D
    tm = IN_TILE
    row = functools.partial(_mod_row, cfg=cfg, tm=tm)
    resident = lambda shape: pl.BlockSpec(shape, lambda i: (0, 0), pipeline_mode=pl.Buffered(1))
    return pl.pallas_call(
        functools.partial(_in_kernel, n_src=len(x_parts), n_lat_tiles=cfg.TL // tm),
        grid=(t // tm,),
        in_specs=_token_specs(x_parts, tm, cfg) + [
                  resident((1, d)),
                  pl.BlockSpec((1, 1, d), lambda i: (row(i) * 6 + 0, 0, 0)),
                  pl.BlockSpec((1, 1, d), lambda i: (row(i) * 6 + 1, 0, 0)),
                  resident((d, P_COLS)), resident((d, QKV_COLS)), resident((d, HEAD_DIM)),
                  pl.BlockSpec((tm, HEAD_DIM), lambda i: (i, 0)),
                  pl.BlockSpec((tm, HEAD_DIM), lambda i: (i, 0))],
        out_specs=[pl.BlockSpec((tm, P_COLS), lambda i: (i, 0)),
                   pl.BlockSpec((tm, QKV_COLS), lambda i: (i, 0)),
                   pl.BlockSpec((tm, HEAD_DIM), lambda i: (i, 0))],
        out_shape=[jax.ShapeDtypeStruct((t, P_COLS), F32), jax.ShapeDtypeStruct((t, QKV_COLS), BF16),
                   jax.ShapeDtypeStruct((t, HEAD_DIM), F32)],
        compiler_params=_cparams(1),
        name="in_proj",
    )(*x_parts, nw, mod, mod, w_a, w_b, w_small, cos_t, sin_t)


CONV_TILE = 256


def _conv_kernel(xq_ref, xk_ref, pq_ref, pk_ref, nq_ref, nk_ref, w_ref, b_ref, q_ref, k_ref, *,
                 starts, ends):
    i = pl.program_id(0)
    is_start = functools.reduce(jnp.logical_or, [i == s for s in starts])
    is_end = functools.reduce(jnp.logical_or, [i == s for s in ends])
    tr = CONV_TILE
    rows = lax.broadcasted_iota(jnp.int32, (tr, 512), 0)

    def conv(x_ref, p_ref, n_ref, half):
        x = x_ref[...]
        prev_row = jnp.where(is_start, 0.0, p_ref[7:8, :])
        next_row = jnp.where(is_end, 0.0, n_ref[0:1, :])
        xp = jnp.where(rows == 0, prev_row, pltpu.roll(x, 1, 0))
        xn = jnp.where(rows == tr - 1, next_row, pltpu.roll(x, tr - 1, 0))
        lo, hi = half * 512, (half + 1) * 512
        y = xp * w_ref[0:1, lo:hi] + x * w_ref[1:2, lo:hi] + xn * w_ref[2:3, lo:hi] + b_ref[:, lo:hi]
        return _silu(y)

    q_ref[...] = conv(xq_ref, pq_ref, nq_ref, 0) * (HEAD_DIM ** -0.5)
    k_ref[...] = conv(xk_ref, pk_ref, nk_ref, 1)


def _conv_call(p, conv_w, conv_b, cfg):
    t = p.shape[0]
    tr = CONV_TILE
    nt = t // tr
    lt, ct = cfg.n_lat // tr, cfg.n_ctx // tr
    seg_first = [b * lt for b in range(cfg.B)] + [cfg.B * lt + b * ct for b in range(cfg.B)]
    seg_last = [b * lt + lt - 1 for b in range(cfg.B)] + [cfg.B * lt + b * ct + ct - 1 for b in range(cfg.B)]
    r8 = tr // 8
    last8 = t // 8 - 1
    kern = functools.partial(_conv_kernel, starts=tuple(seg_first), ends=tuple(seg_last))
    prev = lambda c: (lambda i: (jnp.maximum(i * r8 - 1, 0), c))
    nxt = lambda c: (lambda i: (jnp.minimum((i + 1) * r8, last8), c))
    return pl.pallas_call(
        kern,
        grid=(nt,),
        in_specs=[pl.BlockSpec((tr, 512), lambda i: (i, 0)),
                  pl.BlockSpec((tr, 512), lambda i: (i, 1)),
                  pl.BlockSpec((8, 512), prev(0)), pl.BlockSpec((8, 512), prev(1)),
                  pl.BlockSpec((8, 512), nxt(0)), pl.BlockSpec((8, 512), nxt(1)),
                  pl.BlockSpec((3, 1024), lambda i: (0, 0)),
                  pl.BlockSpec((1, 1024), lambda i: (0, 0))],
        out_specs=[pl.BlockSpec((tr, 512), lambda i: (i, 0)), pl.BlockSpec((tr, 512), lambda i: (i, 0))],
        out_shape=[jax.ShapeDtypeStruct((t, 512), F32)] * 2,
        compiler_params=_cparams(1),
        name="mlstm_conv",
    )(p, p, p, p, p, p, conv_w, conv_b.reshape(1, 1024))


M_HEADS = 4


def _split3(x):
    hi = x.astype(BF16)
    r1 = x - hi.astype(F32)
    mid = r1.astype(BF16)
    return hi, mid, (r1 - mid.astype(F32)).astype(BF16)


def _cumsum_matmul(mask, x):
    return sum(jnp.dot(mask, part, preferred_element_type=F32) for part in _split3(x))


def _mlstm_body(qf, kf, vf, gf, qb, kb, vb, gb, brow, bcol, of, ob, c_scr, m_scr):
    L = CHUNK
    row = lax.broadcasted_iota(jnp.int32, (L, L), 0)
    col = lax.broadcasted_iota(jnp.int32, (L, L), 1)
    neg_inf = jnp.float32(-jnp.inf)

    gcol = jnp.concatenate([gf[...], gb[...]], axis=0) + brow[...]
    grow = jnp.concatenate([gf[...], gb[...]], axis=0).T + bcol[...]
    r8 = lax.broadcasted_iota(jnp.int32, (8, 2 * L), 0)
    l8 = lax.broadcasted_iota(jnp.int32, (8, 2 * L), 1)
    fwd_row = r8 < M_HEADS
    own = (r8 // M_HEADS) == (l8 // L)
    ig = jnp.concatenate([grow[0:4], grow[8:12]], axis=0)
    logf = _log_sigmoid(jnp.concatenate([grow[4:8], grow[12:16]], axis=0))
    tl = lax.broadcasted_iota(jnp.int32, (2 * L, 2 * L), 0)
    ti = lax.broadcasted_iota(jnp.int32, (2 * L, 2 * L), 1)
    same_half = (tl // L) == (ti // L)
    scan_rows = jnp.logical_and(same_half, jnp.where(ti < L, tl - ti, ti - tl) <= 0)
    scan_cols = jnp.logical_and(same_half, jnp.where(tl < L, ti - tl, tl - ti) <= 0)
    cf = sum(jnp.dot(part, scan_rows.astype(BF16), preferred_element_type=F32) for part in _split3(logf))
    u = jnp.where(own, ig - cf, neg_inf)
    m_st = m_scr[...]
    end_lane = jnp.where(fwd_row, L - 1, L)
    f_end = jnp.broadcast_to(jnp.sum(jnp.where(l8 == end_lane, cf, 0.0), axis=-1, keepdims=True),
                             (8, 2 * L))
    dec = jnp.where(own, f_end - cf + ig, neg_inf)
    m_new = jnp.maximum(f_end + m_st, jnp.max(dec, axis=-1, keepdims=True))
    a_prev = jnp.exp(f_end + m_st - m_new)
    m_scr[...] = m_new
    u_keys = jnp.where(fwd_row, u, pltpu.roll(u, L, 1))
    cum_col = _cumsum_matmul(scan_cols.astype(BF16), _log_sigmoid(gcol))

    rep = lambda colv: jnp.broadcast_to(colv, (L, HEAD_DIM))
    dir_refs = ((qf, kf, vf, of), (qb, kb, vb, ob))
    pairs = [(d, h) for d in range(2) for h in range(M_HEADS)]
    work = []
    for d, h in pairs:
        q_ref, k_ref, v_ref, _ = dir_refs[d]
        hs = slice(h * HEAD_DIM, (h + 1) * HEAD_DIM)
        q = q_ref[:, hs].astype(BF16)
        k32 = k_ref[:, hs]
        v1 = jnp.concatenate([v_ref[:, hs].astype(BF16), jnp.ones((L, HEAD_DIM), BF16)], axis=1)
        cn = c_scr[d, h]
        qk = lax.dot_general(q, k32.astype(BF16), (((1,), (1,)), ((), ())), preferred_element_type=F32)
        r_state = jnp.dot(q, cn.astype(BF16), preferred_element_type=F32)
        work.append((k32, v1, cn, qk, r_state))
    yield
    for (d, h), (k32, v1, cn, qk, r_state) in zip(pairs, work):
        o_ref = dir_refs[d][3]
        valid = (col <= row) if d == 0 else (col >= row)
        ts = slice(d * L, (d + 1) * L)
        r = d * M_HEADS + h
        ci, cfc = d * 8 + h, d * 8 + 4 + h
        hs = slice(h * HEAD_DIM, (h + 1) * HEAD_DIM)
        u_tile = jnp.where(valid, u_keys[r:r + 1, 0:L], neg_inf)
        m_rep = rep(jnp.maximum(jnp.max(u_tile, axis=-1, keepdims=True), m_st[r:r + 1, 0:1]))
        cf_rep = rep(cum_col[ts, cfc:cfc + 1])
        ig_rep = rep(gcol[ts, ci:ci + 1])
        w_inter = jnp.exp(m_st[r:r + 1] - m_rep)
        e = jnp.exp(u_tile - m_rep[:, 0:L])
        r_chunk = jnp.dot((qk * e).astype(BF16), v1, preferred_element_type=F32)
        num = w_inter * r_state[:, 0:HEAD_DIM] + r_chunk[:, 0:HEAD_DIM]
        den = w_inter * r_state[:, HEAD_DIM:] + r_chunk[:, HEAD_DIM:]
        o_ref[:, hs] = num / jnp.maximum(jnp.abs(den), jnp.exp(-(cf_rep + m_rep)))

        ws = jnp.exp(f_end[r:r + 1] - cf_rep + ig_rep - m_new[r:r + 1])
        wk = (ws * k32).astype(BF16)
        decay = jnp.concatenate([a_prev[r:r + 1], a_prev[r:r + 1]], axis=1)
        c_scr[d, h] = decay * cn + lax.dot_general(wk, v1, (((0,), (0,)), ((), ())),
                                                   preferred_element_type=F32)


N_MLSTM_REFS = 8
N_GLA_REFS = 6


def _scan_kernel(*refs, n_b):
    it = iter(refs)
    take = lambda n: [next(it) for _ in range(n)]
    m_in = [take(N_MLSTM_REFS) for _ in range(n_b)]
    brow, bcol = take(2)
    g_in = [take(N_GLA_REFS) for _ in range(n_b)]
    w2_ref, b2_ref = take(2)
    m_of, m_ob, g_of, g_ob = take(4)
    c_scr, m_scr, s_scr, gl_scr = take(4)

    @pl.when(pl.program_id(0) == 0)
    def _():
        c_scr[...] = jnp.zeros_like(c_scr)
        m_scr[...] = jnp.zeros_like(m_scr)
        s_scr[...] = jnp.zeros_like(s_scr)

    bodies = []
    for b in range(n_b):
        bodies.append(_mlstm_body(*m_in[b], brow, bcol, m_of.at[0, b], m_ob.at[0, b], c_scr.at[b], m_scr.at[b]))
        bodies.append(_gla_body(*g_in[b], w2_ref, b2_ref, g_of.at[0, b], g_ob.at[0, b], s_scr.at[b], gl_scr.at[b]))
    while bodies:
        bodies = [body for body in bodies if next(body, "done") != "done"]


def _scan_call(mq, mk, p, ps, gate_b, w2p, b2p, cfg):
    nc = (cfg.n_ctx + cfg.n_lat) // CHUNK
    fwd = lambda b, col: (lambda s: (_chunk_block(b, s, cfg), col))
    bwd = lambda b, col: (lambda s: (_chunk_block(b, _bwd_chunk(s, cfg), cfg), col))
    blk = lambda w, f: pl.BlockSpec((CHUNK, w), f)
    brow = jnp.zeros((1, HEAD_DIM), F32).at[0, :16].set(gate_b)
    const = lambda s: (0, 0)
    in_specs, args = [], []
    for b in range(cfg.B):
        for way in (fwd, bwd):
            in_specs += [blk(512, way(b, 0)), blk(512, way(b, 0)), blk(512, way(b, 2)), blk(HEAD_DIM, way(b, 0))]
            args += [mq, mk, p, ps]
    in_specs += [pl.BlockSpec((1, HEAD_DIM), const), pl.BlockSpec((HEAD_DIM, 1), const)]
    args += [brow, brow.reshape(HEAD_DIM, 1)]
    for b in range(cfg.B):
        for way in (fwd, bwd):
            in_specs += [blk(512, way(b, 4)), blk(512, way(b, 5)), blk(HEAD_DIM, way(b, 0))]
            args += [p, p, ps]
    in_specs += [pl.BlockSpec((HEAD_DIM, 2 * G_QK), const), pl.BlockSpec((1, 2 * G_QK), const)]
    args += [w2p, b2p]
    out_f = pl.BlockSpec((1, cfg.B, CHUNK, 512), lambda s: (s, 0, 0, 0))
    out_b = pl.BlockSpec((1, cfg.B, CHUNK, 512), lambda s: (_bwd_chunk(s, cfg), 0, 0, 0))
    outs = pl.pallas_call(
        functools.partial(_scan_kernel, n_b=cfg.B),
        grid=(nc,),
        in_specs=in_specs,
        out_specs=[out_f, out_b, out_f, out_b],
        out_shape=[jax.ShapeDtypeStruct((nc, cfg.B, CHUNK, 512), F32)] * 4,
        scratch_shapes=[pltpu.VMEM((cfg.B, 2, M_HEADS, HEAD_DIM, 2 * HEAD_DIM), F32),
                        pltpu.VMEM((cfg.B, 2 * M_HEADS, 2 * CHUNK), F32),
                        pltpu.VMEM((cfg.B, 2, HEAD_DIM, G_QK), F32),
                        pltpu.VMEM((cfg.B, 2, CHUNK, G_QK), F32)],
        compiler_params=_cparams(1),
        name="recurrent_scan",
    )(*args)
    return outs


G_QK = 256
G_V = 512


G_HEADS = 4
G_DK = 64
GLA_BLK = 16


def _gla_body(qkf, vf, lf, qkb, vb, lb, w2_ref, b2_ref, of, ob, s_scr, gl_scr):
    L, C = CHUNK, GLA_BLK
    nb = L // C
    row = lax.broadcasted_iota(jnp.int32, (L, L), 0)
    col = lax.broadcasted_iota(jnp.int32, (L, L), 1)
    same_blk = (row // C) == (col // C)
    rows8 = lax.broadcasted_iota(jnp.int32, (8, G_QK), 0)
    neg_inf = jnp.float32(-jnp.inf)
    he_r = lax.broadcasted_iota(jnp.int32, (G_QK, G_V), 0) // G_DK
    he_c = lax.broadcasted_iota(jnp.int32, (G_QK, G_V), 1) // HEAD_DIM
    head_expand = (he_r == he_c).astype(BF16)

    def per_head_rows(x):
        lane_head = lax.broadcasted_iota(jnp.int32, x.shape, 1) // G_DK
        return jnp.concatenate([jnp.where(lane_head == h, x, 0.0) for h in range(G_HEADS)], axis=0).astype(BF16)

    def head_blocks(r, n):
        return jnp.concatenate([r[h * n:(h + 1) * n, h * HEAD_DIM:(h + 1) * HEAD_DIM] for h in range(G_HEADS)],
                               axis=1)

    def direction(d, qk_ref, v_ref, l_ref, o_ref):
        cs = slice(d * G_QK, (d + 1) * G_QK)
        z = jnp.dot(l_ref[...], w2_ref[:, cs], precision=HIGHEST, preferred_element_type=F32) + b2_ref[:, cs]
        log_a = _log_sigmoid(z) * (1.0 / GLA_TAU)
        tri = (col <= row) if d == 0 else (col >= row)
        g = _cumsum_matmul(tri.astype(BF16), log_a)
        gl = _cumsum_matmul(jnp.logical_and(tri, same_blk).astype(BF16), log_a)
        gl_scr[d] = gl
        q = qk_ref[:, 0:G_QK] * (G_DK ** -0.5)
        k = qk_ref[:, G_QK:2 * G_QK]
        v = v_ref[...]
        vb16 = v.astype(BF16)
        g_end = g[L - 1:L, :] if d == 0 else g[0:1, :]
        st = s_scr[d]

        r = lax.dot_general(per_head_rows(q * jnp.exp(g)), st.astype(BF16), (((1,), (1,)), ((), ())),
                            preferred_element_type=F32)
        inter = jnp.concatenate([r[h * L:(h + 1) * L] for h in range(G_HEADS)], axis=1)
        yield

        ql = q * jnp.exp(gl)
        first = lambda b: b * C + (C - 1 if d == 0 else 0)
        tot = [gl[first(b):first(b) + 1] for b in range(nb)]
        k_end = [k[b * C:(b + 1) * C] * jnp.exp(tot[b] - gl[b * C:(b + 1) * C]) for b in range(nb)]

        atts, v_cats = {}, {}
        for bi in range(nb):
            earlier = list(range(bi)) if d == 0 else list(range(bi + 1, nb))
            if not earlier:
                continue
            ks, vs = [], []
            for bj in earlier:
                between = range(bj + 1, bi) if d == 0 else range(bi + 1, bj)
                kj = k_end[bj]
                if len(between):
                    kj = kj * jnp.exp(functools.reduce(jnp.add, [tot[m] for m in between]))
                ks.append(kj)
                vs.append(vb16[bj * C:(bj + 1) * C])
            v_cats[bi] = jnp.concatenate(vs, axis=0)
            atts[bi] = lax.dot_general(per_head_rows(ql[bi * C:(bi + 1) * C]),
                                       jnp.concatenate(ks, axis=0).astype(BF16), (((1,), (1,)), ((), ())),
                                       preferred_element_type=F32)

        k_hat = per_head_rows(k * jnp.exp(g_end - g))
        v_rows = jnp.concatenate([vb16[:, h * HEAD_DIM:(h + 1) * HEAD_DIM] for h in range(G_HEADS)], axis=0)
        s_new = st * jnp.exp(g_end) + lax.dot_general(v_rows, k_hat, (((0,), (0,)), ((), ())),
                                                      preferred_element_type=F32)
        yield

        pieces, spans = [], []
        for i in range(L):
            bi, il = divmod(i, C)
            lo, hi = ((il // 8) * 8, C) if d == 0 else (0, (il // 8) * 8 + 8)
            edge = (lo, lo + 8) if d == 0 else (hi - 8, hi)
            k_i = qk_ref[i:i + 1, G_QK:2 * G_QK]
            g_i = gl_scr[d, i:i + 1, :]
            vis = (rows8 >= il % 8) if d == 0 else (rows8 <= il % 8)
            diff = jnp.where(vis, gl[bi * C + edge[0]:bi * C + edge[1]] - g_i, neg_inf)
            if hi - lo > 8:
                other = gl[bi * C + 8:bi * C + 16] if d == 0 else gl[bi * C:bi * C + 8]
                diff = jnp.concatenate([diff, other - g_i] if d == 0 else [other - g_i, diff], axis=0)
            pieces.append((q[bi * C + lo:bi * C + hi] * k_i) * jnp.exp(diff))
            spans.append((lo, hi))
        w_all = jnp.dot(jnp.concatenate(pieces, axis=0).astype(BF16), head_expand, preferred_element_type=F32)
        yield
        cross = {bi: head_blocks(jnp.dot(att.astype(BF16), v_cats[bi], preferred_element_type=F32), C)
                 for bi, att in atts.items()}
        yield

        blocks = []
        w_off = 0
        for bi in range(nb):
            acc = inter[bi * C:(bi + 1) * C]
            if bi in cross:
                acc = acc + cross[bi]
            for il in range(C):
                i = bi * C + il
                lo, hi = spans[i]
                upd = acc[lo:hi] + w_all[w_off:w_off + hi - lo] * v_ref[i:i + 1, :]
                w_off += hi - lo
                parts = ([acc[:lo]] if lo > 0 else []) + [upd] + ([acc[hi:]] if hi < C else [])
                acc = jnp.concatenate(parts, axis=0) if len(parts) > 1 else upd
            blocks.append(acc)
        o_ref[...] = jnp.concatenate(blocks, axis=0)
        s_scr[d] = s_new

    stages = [direction(0, qkf, vf, lf, of), direction(1, qkb, vb, lb, ob)]
    for _ in range(5):
        for stage in stages:
            next(stage, None)
        yield


ATT_SUB_ROWS = 256
ATT_TQ = 1024
ATT_KEY_CHUNK = 1024


def _attn_kernel(*refs, has_lat, lam_init):
    if has_lat:
        q_ref, kc_ref, vc_ref, kl_ref, vl_ref, dl_ref, sub_ref, o_ref, s_scr, vo_scr = refs
        kv = ((kc_ref, vc_ref), (kl_ref, vl_ref))
    else:
        q_ref, kc_ref, vc_ref, dl_ref, sub_ref, o_ref, s_scr, vo_scr = refs
        kv = ((kc_ref, vc_ref),)

    @pl.when(pl.program_id(2) == 0)
    def _():
        off = 0
        for _, v_ref in kv:
            n = v_ref.shape[0]
            vo_scr[off:off + n, 0:HEAD_DIM] = v_ref[...]
            vo_scr[off:off + n, HEAD_DIM:2 * HEAD_DIM] = jnp.ones((n, HEAD_DIM), BF16)
            off += n

    n_sub = q_ref.shape[0] // ATT_SUB_ROWS
    lane = lax.broadcasted_iota(jnp.int32, (ATT_SUB_ROWS, HEAD_DIM), 1)
    zero = jnp.zeros((ATT_SUB_ROWS, HEAD_DIM), BF16)
    dl = dl_ref[...]
    lam = (jnp.exp(jnp.sum(dl[0:1] * dl[1:2], axis=-1, keepdims=True))
           - jnp.exp(jnp.sum(dl[2:3] * dl[3:4], axis=-1, keepdims=True)) + lam_init)
    key_chunks, off = [], 0
    for k_ref, _ in kv:
        for c0 in range(0, k_ref.shape[0], ATT_KEY_CHUNK):
            n = min(ATT_KEY_CHUNK, k_ref.shape[0] - c0)
            key_chunks.append((k_ref, c0, off + c0, n))
        off += k_ref.shape[0]

    prev = None
    for sb in range(n_sub + 1):
        if sb < n_sub:
            q = q_ref[sb * ATT_SUB_ROWS:(sb + 1) * ATT_SUB_ROWS, :]
            qms = (jnp.where(lane < 64, q, zero), jnp.where(lane >= 64, q, zero))
            part_max = [None, None]
        accs = [None, None]
        for k_ref, c0, off, n in key_chunks:
            for m in range(2):
                if sb < n_sub:
                    s = lax.dot_general(qms[m], k_ref[c0:c0 + n, :], (((1,), (1,)), ((), ())),
                                        preferred_element_type=F32)
                    s_scr[2 * sb + m, :, off:off + n] = s
                    folded = functools.reduce(jnp.maximum, [s[:, j:j + HEAD_DIM] for j in range(0, n, HEAD_DIM)])
                    part_max[m] = folded if part_max[m] is None else jnp.maximum(part_max[m], folded)
                if prev is not None:
                    p = jnp.exp2(s_scr[2 * (sb - 1) + m, :, off:off + n] - prev[m]).astype(BF16)
                    part = jnp.dot(p, vo_scr[off:off + n, :], preferred_element_type=F32)
                    accs[m] = part if accs[m] is None else accs[m] + part
        if prev is not None:
            outs = [a[:, 0:HEAD_DIM] / a[:, HEAD_DIM:HEAD_DIM + 1] for a in accs]
            out = outs[0] - lam * outs[1]
            ms = jnp.mean(out * out, axis=-1, keepdims=True)
            o_ref[(sb - 1) * ATT_SUB_ROWS:sb * ATT_SUB_ROWS, :] = (
                (out * lax.rsqrt(ms + NORM_EPS) * sub_ref[...]) * (1.0 - lam_init))
        prev = [jnp.max(pm, axis=-1, keepdims=True) for pm in part_max] if sb < n_sub else None


def _attn_call(qkv, d_lam, d_subln, lam_init, cfg, latent):
    nlb = cfg.n_lat // ATT_TQ
    ctx_row0 = cfg.B * cfg.n_lat // cfg.n_ctx
    kern = functools.partial(_attn_kernel, has_lat=latent, lam_init=lam_init)
    n_keys = cfg.n_ctx + (cfg.n_lat if latent else 0)
    kc = pl.BlockSpec((cfg.n_ctx, HEAD_DIM), lambda b, h, i: (ctx_row0 + b, D_HEADS + h))
    vc = pl.BlockSpec((cfg.n_ctx, HEAD_DIM), lambda b, h, i: (ctx_row0 + b, 2 * D_HEADS + h))
    small = [pl.BlockSpec((4, 64), lambda b, h, i: (0, 0)), pl.BlockSpec((1, HEAD_DIM), lambda b, h, i: (0, 0))]
    if latent:
        tq = ATT_TQ
        grid = (cfg.B, D_HEADS, nlb)
        q_spec = pl.BlockSpec((tq, HEAD_DIM), lambda b, h, i: (b * nlb + i, h))
        kv = [kc, vc,
              pl.BlockSpec((cfg.n_lat, HEAD_DIM), lambda b, h, i: (b, D_HEADS + h)),
              pl.BlockSpec((cfg.n_lat, HEAD_DIM), lambda b, h, i: (b, 2 * D_HEADS + h))]
        out_spec = pl.BlockSpec((tq, HEAD_DIM), lambda b, h, i: (b * nlb + i, h))
        n_rows = cfg.TL
    else:
        tq = cfg.n_ctx
        grid = (cfg.B, D_HEADS, 1)
        q_spec = pl.BlockSpec((tq, HEAD_DIM), lambda b, h, i: (ctx_row0 + b, h))
        kv = [kc, vc]
        out_spec = pl.BlockSpec((tq, HEAD_DIM), lambda b, h, i: (b, h))
        n_rows = cfg.B * cfg.n_ctx
    args = [qkv] * (1 + len(kv)) + [d_lam, d_subln.reshape(1, HEAD_DIM)]
    return pl.pallas_call(
        kern,
        grid=grid,
        in_specs=[q_spec] + kv + small,
        out_specs=out_spec,
        out_shape=jax.ShapeDtypeStruct((n_rows, D_HEADS * HEAD_DIM), F32),
        scratch_shapes=[pltpu.VMEM((2 * tq // ATT_SUB_ROWS, ATT_SUB_ROWS, n_keys), F32),
                        pltpu.VMEM((n_keys, 2 * HEAD_DIM), BF16)],
        compiler_params=_cparams(3),
        name="diff_attn_lat" if latent else "diff_attn_ctx",
    )(*args)


OUT_TILE = 256


def _group_rmsnorm(x, w, groups):
    parts = []
    for gi in range(groups):
        xs = x[:, gi * HEAD_DIM:(gi + 1) * HEAD_DIM]
        ms = jnp.mean(xs * xs, axis=-1, keepdims=True)
        parts.append(xs * lax.rsqrt(ms + NORM_EPS) * w[:, gi * HEAD_DIM:(gi + 1) * HEAD_DIM])
    return jnp.concatenate(parts, axis=-1)


def _out_kernel(*refs, n_src, n_lat_tiles, has_ctx):
    x_refs, refs = refs[:n_src], refs[n_src:]
    if has_ctx:
        hmf, hmb, hgf, hgb, hdl, hdc, mo, go, mn, gn, w_ref, nw_ref, gate_ref, o_ref = refs
        hd = jnp.where(pl.program_id(0) < n_lat_tiles, hdl[...], hdc[...])
    else:
        hmf, hmb, hgf, hgb, hdl, mo, go, mn, gn, w_ref, nw_ref, gate_ref, o_ref = refs
        hd = hdl[...]
    rows = lambda ref: ref[...].reshape(OUT_TILE, 512)
    ym = _group_rmsnorm(rows(hmf) + rows(hmb), mn[...], 4) * jax.nn.sigmoid(mo[...])
    yg = _group_rmsnorm(rows(hgf) + rows(hgb), gn[...], 4) * _silu(go[...])
    y = jnp.concatenate([ym.astype(BF16), yg.astype(BF16), hd.astype(BF16)], axis=-1)
    z = jnp.dot(y, w_ref[...], preferred_element_type=F32)
    ms = jnp.mean(z * z, axis=-1, keepdims=True)
    o_ref[...] = _token_tile(x_refs, n_lat_tiles) + gate_ref[0] * (z * lax.rsqrt(ms + NORM_EPS) * nw_ref[...])


def _out_call(x_parts, hmf, hmb, hgf, hgb, hd_lat, hd_ctx, p, m_norm, g_norm, w_out, layer, nw, mod, cfg, n_rows):
    d = cfg.D
    tm = OUT_TILE
    nl = cfg.TL // tm
    row = functools.partial(_mod_row, cfg=cfg, tm=tm)
    rt = lambda w, c: pl.BlockSpec((tm, w), lambda i: (i, c))
    const = lambda i: (0, 0)
    lt, ct = cfg.n_lat // tm, cfg.n_ctx // tm

    def scan_index(i):
        k = i - cfg.B * lt
        return (jnp.where(k < 0, ct + i % lt, k % ct), jnp.where(k < 0, i // lt, k // ct), 0, 0)

    scan_blk = pl.BlockSpec((tm // CHUNK, 1, CHUNK, 512), scan_index)
    hd_specs = [pl.BlockSpec((tm, 1024), lambda i: (jnp.minimum(i, nl - 1), 0))]
    hd_args = [hd_lat]
    if hd_ctx is not None:
        hd_specs.append(pl.BlockSpec((tm, 1024), lambda i: (jnp.maximum(i - nl, 0), 0)))
        hd_args.append(hd_ctx)
    return pl.pallas_call(
        functools.partial(_out_kernel, n_src=len(x_parts), n_lat_tiles=nl, has_ctx=hd_ctx is not None),
        grid=(n_rows // tm,),
        in_specs=_token_specs(x_parts, tm, cfg) + [scan_blk, scan_blk, scan_blk, scan_blk] + hd_specs + [
                  rt(512, 3), rt(512, 6),
                  pl.BlockSpec((1, 512), const), pl.BlockSpec((1, 512), const),
                  pl.BlockSpec((None, d, d), lambda i: (layer, 0, 0)), pl.BlockSpec((1, d), const),
                  pl.BlockSpec((1, 1, d), lambda i: (row(i) * 6 + 2, 0, 0))],
        out_specs=rt(d, 0),
        out_shape=jax.ShapeDtypeStruct((n_rows, d), F32),
        compiler_params=_cparams(1),
        name="out_proj",
    )(*x_parts, hmf, hmb, hgf, hgb, *hd_args, p, p, m_norm, g_norm, w_out, nw, mod)


def _ffn_kernel(x_ref, nw_ref, shift_ref, scale_ref, wg_ref, wu_ref, wd_ref, pw_ref, gate_ref, o_ref,
                h_scr, acc_scr):
    j = pl.program_id(1)

    @pl.when(j == 0)
    def _():
        h_scr[...] = _prenorm(x_ref[...], nw_ref[...], shift_ref[0], scale_ref[0]).astype(BF16)
        acc_scr[...] = jnp.zeros_like(acc_scr)

    h = h_scr[...]
    a = jnp.dot(h, wg_ref[...], preferred_element_type=F32)
    u = jnp.dot(h, wu_ref[...], preferred_element_type=F32)
    acc_scr[...] += jnp.dot((_silu(a) * u).astype(BF16), wd_ref[...], preferred_element_type=F32)

    @pl.when(j == pl.num_programs(1) - 1)
    def _():
        z = acc_scr[...]
        ms = jnp.mean(z * z, axis=-1, keepdims=True)
        o_ref[...] = x_ref[...] + gate_ref[0] * (z * lax.rsqrt(ms + NORM_EPS) * pw_ref[...])


def _ffn_call(x, nw_pre, nw_post, mod, wg, wu, wd, layer, cfg, n_rows, tm=512, tf=512):
    _, d, f = wg.shape
    row = functools.partial(_mod_row, cfg=cfg, tm=tm)
    const = lambda i, j: (0, 0)
    modspec = lambda kk: pl.BlockSpec((1, 1, d), lambda i, j: (row(i) * 6 + kk, 0, 0))
    return pl.pallas_call(
        _ffn_kernel,
        grid=(n_rows // tm, f // tf),
        in_specs=[pl.BlockSpec((tm, d), lambda i, j: (i, 0)),
                  pl.BlockSpec((1, d), const), modspec(3), modspec(4),
                  pl.BlockSpec((None, d, tf), lambda i, j: (layer, 0, j)),
                  pl.BlockSpec((None, d, tf), lambda i, j: (layer, 0, j)),
                  pl.BlockSpec((None, tf, d), lambda i, j: (layer, j, 0)),
                  pl.BlockSpec((1, d), const), modspec(5)],
        out_specs=pl.BlockSpec((tm, d), lambda i, j: (i, 0)),
        out_shape=jax.ShapeDtypeStruct((n_rows, d), F32),
        scratch_shapes=[pltpu.VMEM((tm, d), BF16), pltpu.VMEM((tm, d), F32)],
        compiler_params=_cparams(2),
        name="ffn",
    )(x, nw_pre, mod, mod, wg, wu, wd, nw_post, mod)


_MIX = {}
_off = 0
for _name, _w in (("m_q", 512), ("m_k", 512), ("m_v", 512), ("m_o", 512), ("m_gates", 16),
                  ("g_q", 256), ("g_k", 256), ("g_v", 512), ("g_out", 512), ("g_lr", 32),
                  ("d_q", 1024), ("d_k", 1024), ("d_v", 1024)):
    _MIX[_name] = (_off, _w)
    _off += _w
_P_ORDER = ("m_q", "m_k", "m_v", "m_o", "g_q", "g_k", "g_v", "g_out")
_QKV_ORDER = ("d_q", "d_k", "d_v")


def _split_w_in(w_in):
    cols = lambda n: w_in[:, _MIX[n][0]:_MIX[n][0] + _MIX[n][1]]
    w_a = jnp.concatenate([cols(n) for n in _P_ORDER], axis=1).astype(BF16)
    w_b = jnp.concatenate([cols(n) for n in _QKV_ORDER], axis=1).astype(BF16)
    w_small = jnp.concatenate([cols("m_gates"), cols("g_lr"),
                               jnp.zeros((w_in.shape[0], HEAD_DIM - 48), w_in.dtype)], axis=1).astype(BF16)
    return w_a, w_b, w_small


def _rope_tables(cfg):
    rows = cfg.n_lat // GRID_W
    r = np.repeat(np.arange(rows, dtype=np.float64), GRID_W)
    c = np.tile(np.arange(GRID_W, dtype=np.float64), rows)
    half = 16
    inv_freq = ROPE_BASE ** (-np.arange(half, dtype=np.float64) / half)
    ang_r, ang_c = r[:, None] * inv_freq, c[:, None] * inv_freq
    ang = np.concatenate([ang_r, ang_r, ang_c, ang_c], axis=-1)
    ang = np.tile(ang, (cfg.B, 2))
    n_c = cfg.B * cfg.n_ctx
    cos_t = np.concatenate([np.cos(ang), np.ones((n_c, HEAD_DIM))], axis=0).astype(np.float32)
    sin_t = np.concatenate([np.sin(ang), np.zeros((n_c, HEAD_DIM))], axis=0).astype(np.float32)
    return jnp.asarray(cos_t), jnp.asarray(sin_t)


def _layer(x_parts, mod, lw, stacks, layer, lam_init, rope, cfg, need_ctx):
    d = cfg.D
    p, qkv, ps = _in_call(x_parts, lw["norm_mix_pre"].reshape(1, d), mod, *_split_w_in(lw["w_in"]), *rope, cfg)

    mq, mk = _conv_call(p, lw["mlstm_conv_w"], lw["mlstm_conv_b"], cfg)
    w2 = lw["gla_gate_w2"]
    w2p = jnp.zeros((HEAD_DIM, 2 * G_QK), F32)
    w2p = w2p.at[16:32, 0:G_QK].set(w2[0]).at[32:48, G_QK:].set(w2[1])
    hmf, hmb, hgf, hgb = _scan_call(mq, mk, p, ps, lw["mlstm_gate_b"], w2p,
                                    lw["gla_gate_b"].reshape(1, 2 * G_QK), cfg)

    hd_lat = _attn_call(qkv, lw["diff_lambda"], lw["diff_subln"], lam_init, cfg, latent=True)
    hd_ctx = _attn_call(qkv, lw["diff_lambda"], lw["diff_subln"], lam_init, cfg, latent=False) if need_ctx else None

    n_rows = cfg.T if need_ctx else cfg.TL
    xt = _out_call(x_parts, hmf, hmb, hgf, hgb, hd_lat, hd_ctx, p, lw["mlstm_norm"].reshape(1, 512),
                   lw["gla_norm"].reshape(1, 512), stacks["w_out"], layer,
                   lw["norm_mix_post"].reshape(1, d), mod, cfg, n_rows)
    xt = _ffn_call(xt, lw["norm_ffn_pre"].reshape(1, d), lw["norm_ffn_post"].reshape(1, d), mod,
                   stacks["w_ffn_gate"], stacks["w_ffn_up"], stacks["w_ffn_down"], layer, cfg, n_rows)
    return xt


_LAYER_KEYS = ("norm_mix_pre", "norm_mix_post", "norm_ffn_pre", "norm_ffn_post", "w_in", "mlstm_conv_w",
               "mlstm_conv_b", "mlstm_gate_b", "mlstm_norm", "gla_gate_w2", "gla_gate_b", "gla_norm",
               "diff_lambda", "diff_subln", "w_out", "w_ffn_gate", "w_ffn_up", "w_ffn_down")
_STACK_KEYS = ("w_out", "w_ffn_gate", "w_ffn_up", "w_ffn_down")


def kernel(x, c, ctx, c_ctx, w_mod, b_mod, norm_mix_pre, norm_mix_post, norm_ffn_pre, norm_ffn_post, w_in, mlstm_conv_w, mlstm_conv_b, mlstm_gate_b, mlstm_norm, gla_gate_w2, gla_gate_b, gla_norm, diff_lambda, diff_subln, w_out, w_ffn_gate, w_ffn_up, w_ffn_down):
    weights = dict(zip(_LAYER_KEYS, (norm_mix_pre, norm_mix_post, norm_ffn_pre, norm_ffn_post, w_in,
                                     mlstm_conv_w, mlstm_conv_b, mlstm_gate_b, mlstm_norm, gla_gate_w2,
                                     gla_gate_b, gla_norm, diff_lambda, diff_subln, w_out, w_ffn_gate,
                                     w_ffn_up, w_ffn_down)))
    b, n_lat, d = x.shape
    cfg = Cfg(B=b, n_ctx=ctx.shape[1], n_lat=n_lat, D=d, F=w_ffn_gate.shape[-1])
    depth = w_mod.shape[0]
    c8 = jnp.zeros((8, d), F32).at[0].set(c_ctx).at[1:1 + b].set(c)
    mods = _mod_call(c8, w_mod, b_mod).reshape(depth, 8 * 6, 1, d)
    rope = _rope_tables(cfg)
    x_parts = (x.reshape(b * n_lat, d), ctx.reshape(b * ctx.shape[1], d))
    stacks = {k: weights.pop(k).astype(BF16) for k in _STACK_KEYS}
    for layer in range(depth):
        lw = {k: v[layer] for k, v in weights.items()}
        lam_init = 0.8 - 0.6 * math.exp(-0.3 * layer)
        x_parts = (_layer(x_parts, mods[layer], lw, stacks, layer, lam_init, rope, cfg,
                          need_ctx=layer < depth - 1),)
    return x_parts[0].reshape(b, n_lat, d)
```

```python
import dataclasses
import functools
import math

import jax
import jax.numpy as jnp
import numpy as np
from jax import lax
from jax.experimental import pallas as pl
from jax.experimental.pallas import tpu as pltpu

F32 = jnp.float32
BF16 = jnp.bfloat16
NORM_EPS = 1e-6
CHUNK = 64
GRID_W = 64
ROPE_BASE = 10000.0
GLA_TAU = 16.0
HEAD_DIM = 128
VMEM_LIMIT_BYTES = 56 * 1024 * 1024
HIGHEST = lax.Precision.HIGHEST


@dataclasses.dataclass(frozen=True)
class Cfg:
    B: int = 2
    n_ctx: int = 256
    n_lat: int = 4096
    D: int = 2048
    F: int = 5632

    @property
    def T(self):
        return self.B * (self.n_ctx + self.n_lat)

    @property
    def TL(self):
        return self.B * self.n_lat


def _cparams(n_axes):
    return pltpu.CompilerParams(dimension_semantics=("arbitrary",) * n_axes,
                                vmem_limit_bytes=VMEM_LIMIT_BYTES)


def _mod_row(i, cfg, tm):
    lt = cfg.n_lat // tm
    return jnp.where(i < cfg.B * lt, 1 + i // lt, 0)


def _chunk_block(b, c, cfg):
    ncc, ncl = cfg.n_ctx // CHUNK, cfg.n_lat // CHUNK
    return jnp.where(c < ncc, cfg.B * ncl + b * ncc + c, b * ncl + (c - ncc))


def _bwd_chunk(s, cfg):
    ncc, ncl = cfg.n_ctx // CHUNK, cfg.n_lat // CHUNK
    return jnp.where(s < ncc, ncc - 1 - s, ncc + ncl - 1 - (s - ncc))


def _log_sigmoid(x):
    return jnp.minimum(x, 0.0) - jnp.log1p(jnp.exp(-jnp.abs(x)))


def _silu(x):
    return x * jax.nn.sigmoid(x)


MOD_K_TILE = 256


def _mod_kernel(c_ref, w_ref, b_ref, o_ref):
    k = pl.program_id(1)

    @pl.when(k == 0)
    def _():
        o_ref[0] = jnp.broadcast_to(b_ref[0], o_ref.shape[1:])

    s = _silu(c_ref[...]).astype(BF16)
    o_ref[0] += jnp.dot(s, w_ref[0].astype(BF16), preferred_element_type=F32)


def _mod_call(c8, w_mod, b_mod):
    depth, d, n = w_mod.shape
    tk = MOD_K_TILE
    return pl.pallas_call(
        _mod_kernel,
        grid=(depth, d // tk),
        in_specs=[pl.BlockSpec((8, tk), lambda l, k: (0, k)),
                  pl.BlockSpec((1, tk, n), lambda l, k: (l, k, 0)),
                  pl.BlockSpec((1, 1, n), lambda l, k: (l, 0, 0))],
        out_specs=pl.BlockSpec((1, 8, n), lambda l, k: (l, 0, 0)),
        out_shape=jax.ShapeDtypeStruct((depth, 8, n), F32),
        compiler_params=_cparams(2),
        name="adaln_mod",
    )(c8, w_mod, b_mod.reshape(depth, 1, n))


def _prenorm(x, nw, shift, scale):
    ms = jnp.mean(x * x, axis=-1, keepdims=True)
    return (x * lax.rsqrt(ms + NORM_EPS) * nw) * (1.0 + scale) + shift


D_HEADS = 8
P_COLS = 3584
M_CONV_COLS = 1024
P_BLK_M_V, P_BLK_M_O, P_BLK_G_QK, P_BLK_G_V, P_BLK_G_OUT = range(5)
QKV_COLS = 3072
IN_TILE = 256


def _token_specs(x_parts, tm, cfg):
    d = cfg.D
    if len(x_parts) == 1:
        return [pl.BlockSpec((tm, d), lambda i: (i, 0))]
    nl = cfg.TL // tm
    return [pl.BlockSpec((tm, d), lambda i: (jnp.minimum(i, nl - 1), 0)),
            pl.BlockSpec((tm, d), lambda i: (jnp.maximum(i - nl, 0), 0))]


def _token_tile(x_refs, n_lat_tiles):
    if len(x_refs) == 1:
        return x_refs[0][...]
    return jnp.where(pl.program_id(0) < n_lat_tiles, x_refs[0][...], x_refs[1][...])


def _halo_specs(x_parts, tm, cfg):
    d, r8 = cfg.D, tm // 8
    nl = cfg.TL // tm
    specs = []
    for part, tile0 in zip(x_parts, (0, nl)):
        last8 = part.shape[0] // 8 - 1
        clamp = lambda v, last8=last8: jnp.clip(v, 0, last8)
        specs.append(pl.BlockSpec((8, d), lambda i, t0=tile0, c=clamp: (c((i - t0) * r8 - 1), 0)))
        specs.append(pl.BlockSpec((8, d), lambda i, t0=tile0, c=clamp: (c((i - t0 + 1) * r8), 0)))
    return specs


def _in_kernel(*refs, n_src, n_lat_tiles, seg_first, seg_last):
    x_refs, halo_refs = refs[:n_src], refs[n_src:3 * n_src]
    (nw_ref, shift_ref, scale_ref, wa_ref, wb_ref, ws_ref, cos_ref, sin_ref, cw_ref, cb_ref,
     mq_ref, mk_ref, p_ref, qkv_ref, ps_ref) = refs[3 * n_src:]
    i = pl.program_id(0)
    norm = lambda x: _prenorm(x, nw_ref[...], shift_ref[0], scale_ref[0]).astype(BF16)
    h = norm(_token_tile(x_refs, n_lat_tiles))
    ps_ref[...] = jnp.dot(h, ws_ref[...], preferred_element_type=F32)
    p_ref[...] = jnp.dot(h, wa_ref[:, M_CONV_COLS:], preferred_element_type=F32)

    halo = jnp.concatenate([_token_tile(halo_refs[0::2], n_lat_tiles), _token_tile(halo_refs[1::2], n_lat_tiles)],
                           axis=0)
    pm = jnp.dot(jnp.concatenate([h, norm(halo)], axis=0), wa_ref[:, 0:M_CONV_COLS], preferred_element_type=F32)
    is_first = functools.reduce(jnp.logical_or, [i == s for s in seg_first])
    is_last = functools.reduce(jnp.logical_or, [i == s for s in seg_last])
    prev_row = jnp.where(is_first, 0.0, pm[IN_TILE + 7:IN_TILE + 8])
    next_row = jnp.where(is_last, 0.0, pm[IN_TILE + 8:IN_TILE + 9])
    xm = pm[0:IN_TILE]
    rows = lax.broadcasted_iota(jnp.int32, xm.shape, 0)
    x_prev = jnp.where(rows == 0, prev_row, pltpu.roll(xm, 1, 0))
    x_next = jnp.where(rows == IN_TILE - 1, next_row, pltpu.roll(xm, IN_TILE - 1, 0))
    conv = _silu(x_prev * cw_ref[0:1, :] + xm * cw_ref[1:2, :] + x_next * cw_ref[2:3, :] + cb_ref[...])
    mq_ref[...] = conv[:, 0:M_CONV_COLS // 2] * (HEAD_DIM ** -0.5)
    mk_ref[...] = conv[:, M_CONV_COLS // 2:]

    qkv = jnp.dot(h, wb_ref[...], preferred_element_type=F32)
    lane = lax.broadcasted_iota(jnp.int32, (IN_TILE, HEAD_DIM), 1)
    low = (lane % 32) < 16
    cos, sin = cos_ref[...], sin_ref[...]
    q_scale = 64 ** -0.5 * math.log2(math.e)
    for s in range(2 * D_HEADS):
        cs = slice(s * HEAD_DIM, (s + 1) * HEAD_DIM)
        x = qkv[:, cs]
        rot = jnp.where(low, -pltpu.roll(x, HEAD_DIM - 16, 1), pltpu.roll(x, 16, 1))
        y = x * cos + rot * sin
        qkv_ref[:, cs] = ((y * q_scale) if s < D_HEADS else y).astype(BF16)
    vs = slice(2 * D_HEADS * HEAD_DIM, QKV_COLS)
    qkv_ref[:, vs] = qkv[:, vs].astype(BF16)


def _in_call(x_parts, nw, mod, w_a, w_b, w_small, cos_t, sin_t, conv_w, conv_b, cfg):
    t, d = cfg.T, cfg.D
    tm = IN_TILE
    row = functools.partial(_mod_row, cfg=cfg, tm=tm)
    resident = lambda shape: pl.BlockSpec(shape, lambda i: (0, 0), pipeline_mode=pl.Buffered(1))
    lt, ct = cfg.n_lat // tm, cfg.n_ctx // tm
    seg_first = [b * lt for b in range(cfg.B)] + [cfg.B * lt + b * ct for b in range(cfg.B)]
    seg_last = [b * lt + lt - 1 for b in range(cfg.B)] + [cfg.B * lt + b * ct + ct - 1 for b in range(cfg.B)]
    half = M_CONV_COLS // 2
    tile = lambda w: pl.BlockSpec((tm, w), lambda i: (i, 0))
    return pl.pallas_call(
        functools.partial(_in_kernel, n_src=len(x_parts), n_lat_tiles=cfg.TL // tm,
                          seg_first=tuple(seg_first), seg_last=tuple(seg_last)),
        grid=(t // tm,),
        in_specs=_token_specs(x_parts, tm, cfg) + _halo_specs(x_parts, tm, cfg) + [
                  resident((1, d)),
                  pl.BlockSpec((1, 1, d), lambda i: (row(i) * 6 + 0, 0, 0)),
                  pl.BlockSpec((1, 1, d), lambda i: (row(i) * 6 + 1, 0, 0)),
                  resident((d, P_COLS)), resident((d, QKV_COLS)), resident((d, HEAD_DIM)),
                  tile(HEAD_DIM), tile(HEAD_DIM),
                  resident((3, M_CONV_COLS)), resident((1, M_CONV_COLS))],
        out_specs=[tile(half), tile(half), tile(P_COLS - M_CONV_COLS), tile(QKV_COLS), tile(HEAD_DIM)],
        out_shape=[jax.ShapeDtypeStruct((t, half), F32), jax.ShapeDtypeStruct((t, half), F32),
                   jax.ShapeDtypeStruct((t, P_COLS - M_CONV_COLS), F32),
                   jax.ShapeDtypeStruct((t, QKV_COLS), BF16), jax.ShapeDtypeStruct((t, HEAD_DIM), F32)],
        compiler_params=_cparams(1),
        name="in_proj",
    )(*x_parts, *[part for part in x_parts for _ in range(2)], nw, mod, mod, w_a, w_b, w_small, cos_t, sin_t,
      conv_w, conv_b.reshape(1, M_CONV_COLS))


M_HEADS = 4


def _split3(x):
    hi = x.astype(BF16)
    r1 = x - hi.astype(F32)
    mid = r1.astype(BF16)
    return hi, mid, (r1 - mid.astype(F32)).astype(BF16)


def _cumsum_matmul(mask, x):
    return sum(jnp.dot(mask, part, preferred_element_type=F32) for part in _split3(x))


def _mlstm_body(qf, kf, vf, gf, qb, kb, vb, gb, brow, bcol, of, ob, c_scr, m_scr):
    L = CHUNK
    row = lax.broadcasted_iota(jnp.int32, (L, L), 0)
    col = lax.broadcasted_iota(jnp.int32, (L, L), 1)
    neg_inf = jnp.float32(-jnp.inf)

    gcol = jnp.concatenate([gf[...], gb[...]], axis=0) + brow[...]
    grow = jnp.concatenate([gf[...], gb[...]], axis=0).T + bcol[...]
    r8 = lax.broadcasted_iota(jnp.int32, (8, 2 * L), 0)
    l8 = lax.broadcasted_iota(jnp.int32, (8, 2 * L), 1)
    fwd_row = r8 < M_HEADS
    own = (r8 // M_HEADS) == (l8 // L)
    ig = jnp.concatenate([grow[0:4], grow[8:12]], axis=0)
    logf = _log_sigmoid(jnp.concatenate([grow[4:8], grow[12:16]], axis=0))
    tl = lax.broadcasted_iota(jnp.int32, (2 * L, 2 * L), 0)
    ti = lax.broadcasted_iota(jnp.int32, (2 * L, 2 * L), 1)
    same_half = (tl // L) == (ti // L)
    scan_rows = jnp.logical_and(same_half, jnp.where(ti < L, tl - ti, ti - tl) <= 0)
    scan_cols = jnp.logical_and(same_half, jnp.where(tl < L, ti - tl, tl - ti) <= 0)
    cf = sum(jnp.dot(part, scan_rows.astype(BF16), preferred_element_type=F32) for part in _split3(logf))
    u = jnp.where(own, ig - cf, neg_inf)
    m_st = m_scr[...]
    end_lane = jnp.where(fwd_row, L - 1, L)
    f_end = jnp.broadcast_to(jnp.sum(jnp.where(l8 == end_lane, cf, 0.0), axis=-1, keepdims=True),
                             (8, 2 * L))
    dec = jnp.where(own, f_end - cf + ig, neg_inf)
    m_new = jnp.maximum(f_end + m_st, jnp.max(dec, axis=-1, keepdims=True))
    a_prev = jnp.exp(f_end + m_st - m_new)
    m_scr[...] = m_new
    u_keys = jnp.where(fwd_row, u, pltpu.roll(u, L, 1))
    cum_col = _cumsum_matmul(scan_cols.astype(BF16), _log_sigmoid(gcol))

    rep = lambda colv: jnp.broadcast_to(colv, (L, HEAD_DIM))
    dir_refs = ((qf, kf, vf, of), (qb, kb, vb, ob))
    pairs = [(d, h) for d in range(2) for h in range(M_HEADS)]
    work = []
    for d, h in pairs:
        q_ref, k_ref, v_ref, _ = dir_refs[d]
        hs = slice(h * HEAD_DIM, (h + 1) * HEAD_DIM)
        q = q_ref[:, hs].astype(BF16)
        k32 = k_ref[:, hs]
        v1 = jnp.concatenate([v_ref[:, hs].astype(BF16), jnp.ones((L, HEAD_DIM), BF16)], axis=1)
        cn = c_scr[d, h]
        qk = lax.dot_general(q, k32.astype(BF16), (((1,), (1,)), ((), ())), preferred_element_type=F32)
        r_state = jnp.dot(q, cn.astype(BF16), preferred_element_type=F32)
        work.append((k32, v1, cn, qk, r_state))
    yield
    for (d, h), (k32, v1, cn, qk, r_state) in zip(pairs, work):
        o_ref = dir_refs[d][3]
        valid = (col <= row) if d == 0 else (col >= row)
        ts = slice(d * L, (d + 1) * L)
        r = d * M_HEADS + h
        ci, cfc = d * 8 + h, d * 8 + 4 + h
        hs = slice(h * HEAD_DIM, (h + 1) * HEAD_DIM)
        u_tile = jnp.where(valid, u_keys[r:r + 1, 0:L], neg_inf)
        m_rep = rep(jnp.maximum(jnp.max(u_tile, axis=-1, keepdims=True), m_st[r:r + 1, 0:1]))
        cf_rep = rep(cum_col[ts, cfc:cfc + 1])
        ig_rep = rep(gcol[ts, ci:ci + 1])
        w_inter = jnp.exp(m_st[r:r + 1] - m_rep)
        e = jnp.exp(u_tile - m_rep[:, 0:L])
        r_chunk = jnp.dot((qk * e).astype(BF16), v1, preferred_element_type=F32)
        num = w_inter * r_state[:, 0:HEAD_DIM] + r_chunk[:, 0:HEAD_DIM]
        den = w_inter * r_state[:, HEAD_DIM:] + r_chunk[:, HEAD_DIM:]
        o_ref[:, hs] = num / jnp.maximum(jnp.abs(den), jnp.exp(-(cf_rep + m_rep)))

        ws = jnp.exp(f_end[r:r + 1] - cf_rep + ig_rep - m_new[r:r + 1])
        wk = (ws * k32).astype(BF16)
        decay = jnp.concatenate([a_prev[r:r + 1], a_prev[r:r + 1]], axis=1)
        c_scr[d, h] = decay * cn + lax.dot_general(wk, v1, (((0,), (0,)), ((), ())),
                                                   preferred_element_type=F32)


N_MLSTM_REFS = 8
N_GLA_REFS = 6


def _scan_kernel(*refs, n_b):
    it = iter(refs)
    take = lambda n: [next(it) for _ in range(n)]
    m_in = [take(N_MLSTM_REFS) for _ in range(n_b)]
    brow, bcol = take(2)
    g_in = [take(N_GLA_REFS) for _ in range(n_b)]
    w2_ref, b2_ref = take(2)
    m_of, m_ob, g_of, g_ob = take(4)
    c_scr, m_scr, s_scr, gl_scr = take(4)

    @pl.when(pl.program_id(0) == 0)
    def _():
        c_scr[...] = jnp.zeros_like(c_scr)
        m_scr[...] = jnp.zeros_like(m_scr)
        s_scr[...] = jnp.zeros_like(s_scr)

    bodies = []
    for b in range(n_b):
        bodies.append(_mlstm_body(*m_in[b], brow, bcol, m_of.at[0, b], m_ob.at[0, b], c_scr.at[b], m_scr.at[b]))
        bodies.append(_gla_body(*g_in[b], w2_ref, b2_ref, g_of.at[0, b], g_ob.at[0, b], s_scr.at[b], gl_scr.at[b]))
    while bodies:
        bodies = [body for body in bodies if next(body, "done") != "done"]


def _scan_call(mq, mk, p, ps, gate_b, w2p, b2p, cfg):
    nc = (cfg.n_ctx + cfg.n_lat) // CHUNK
    fwd = lambda b, col: (lambda s: (_chunk_block(b, s, cfg), col))
    bwd = lambda b, col: (lambda s: (_chunk_block(b, _bwd_chunk(s, cfg), cfg), col))
    blk = lambda w, f: pl.BlockSpec((CHUNK, w), f)
    brow = jnp.zeros((1, HEAD_DIM), F32).at[0, :16].set(gate_b)
    const = lambda s: (0, 0)
    in_specs, args = [], []
    for b in range(cfg.B):
        for way in (fwd, bwd):
            in_specs += [blk(512, way(b, 0)), blk(512, way(b, 0)), blk(512, way(b, P_BLK_M_V)),
                         blk(HEAD_DIM, way(b, 0))]
            args += [mq, mk, p, ps]
    in_specs += [pl.BlockSpec((1, HEAD_DIM), const), pl.BlockSpec((HEAD_DIM, 1), const)]
    args += [brow, brow.reshape(HEAD_DIM, 1)]
    for b in range(cfg.B):
        for way in (fwd, bwd):
            in_specs += [blk(512, way(b, P_BLK_G_QK)), blk(512, way(b, P_BLK_G_V)), blk(HEAD_DIM, way(b, 0))]
            args += [p, p, ps]
    in_specs += [pl.BlockSpec((HEAD_DIM, 2 * G_QK), const), pl.BlockSpec((1, 2 * G_QK), const)]
    args += [w2p, b2p]
    out_f = pl.BlockSpec((1, cfg.B, CHUNK, 512), lambda s: (s, 0, 0, 0))
    out_b = pl.BlockSpec((1, cfg.B, CHUNK, 512), lambda s: (_bwd_chunk(s, cfg), 0, 0, 0))
    outs = pl.pallas_call(
        functools.partial(_scan_kernel, n_b=cfg.B),
        grid=(nc,),
        in_specs=in_specs,
        out_specs=[out_f, out_b, out_f, out_b],
        out_shape=[jax.ShapeDtypeStruct((nc, cfg.B, CHUNK, 512), F32)] * 4,
        scratch_shapes=[pltpu.VMEM((cfg.B, 2, M_HEADS, HEAD_DIM, 2 * HEAD_DIM), F32),
                        pltpu.VMEM((cfg.B, 2 * M_HEADS, 2 * CHUNK), F32),
                        pltpu.VMEM((cfg.B, 2, HEAD_DIM, G_QK), F32),
                        pltpu.VMEM((cfg.B, 2, CHUNK, G_QK), F32)],
        compiler_params=_cparams(1),
        name="recurrent_scan",
    )(*args)
    return outs


G_QK = 256
G_V = 512


G_HEADS = 4
G_DK = 64
GLA_BLK = 16


def _gla_body(qkf, vf, lf, qkb, vb, lb, w2_ref, b2_ref, of, ob, s_scr, gl_scr):
    L, C = CHUNK, GLA_BLK
    nb = L // C
    row = lax.broadcasted_iota(jnp.int32, (L, L), 0)
    col = lax.broadcasted_iota(jnp.int32, (L, L), 1)
    same_blk = (row // C) == (col // C)
    rows8 = lax.broadcasted_iota(jnp.int32, (8, G_QK), 0)
    neg_inf = jnp.float32(-jnp.inf)
    he_r = lax.broadcasted_iota(jnp.int32, (G_QK, G_V), 0) // G_DK
    he_c = lax.broadcasted_iota(jnp.int32, (G_QK, G_V), 1) // HEAD_DIM
    head_expand = (he_r == he_c).astype(BF16)

    def per_head_rows(x):
        lane_head = lax.broadcasted_iota(jnp.int32, x.shape, 1) // G_DK
        return jnp.concatenate([jnp.where(lane_head == h, x, 0.0) for h in range(G_HEADS)], axis=0).astype(BF16)

    def head_blocks(r, n):
        return jnp.concatenate([r[h * n:(h + 1) * n, h * HEAD_DIM:(h + 1) * HEAD_DIM] for h in range(G_HEADS)],
                               axis=1)

    def direction(d, qk_ref, v_ref, l_ref, o_ref):
        cs = slice(d * G_QK, (d + 1) * G_QK)
        z = jnp.dot(l_ref[...], w2_ref[:, cs], precision=HIGHEST, preferred_element_type=F32) + b2_ref[:, cs]
        log_a = _log_sigmoid(z) * (1.0 / GLA_TAU)
        tri = (col <= row) if d == 0 else (col >= row)
        g = _cumsum_matmul(tri.astype(BF16), log_a)
        gl = _cumsum_matmul(jnp.logical_and(tri, same_blk).astype(BF16), log_a)
        gl_scr[d] = gl
        q = qk_ref[:, 0:G_QK] * (G_DK ** -0.5)
        k = qk_ref[:, G_QK:2 * G_QK]
        v = v_ref[...]
        vb16 = v.astype(BF16)
        g_end = g[L - 1:L, :] if d == 0 else g[0:1, :]
        st = s_scr[d]

        r = lax.dot_general(per_head_rows(q * jnp.exp(g)), st.astype(BF16), (((1,), (1,)), ((), ())),
                            preferred_element_type=F32)
        inter = jnp.concatenate([r[h * L:(h + 1) * L] for h in range(G_HEADS)], axis=1)
        yield

        ql = q * jnp.exp(gl)
        first = lambda b: b * C + (C - 1 if d == 0 else 0)
        tot = [gl[first(b):first(b) + 1] for b in range(nb)]
        k_end = [k[b * C:(b + 1) * C] * jnp.exp(tot[b] - gl[b * C:(b + 1) * C]) for b in range(nb)]

        atts, v_cats = {}, {}
        for bi in range(nb):
            earlier = list(range(bi)) if d == 0 else list(range(bi + 1, nb))
            if not earlier:
                continue
            ks, vs = [], []
            for bj in earlier:
                between = range(bj + 1, bi) if d == 0 else range(bi + 1, bj)
                kj = k_end[bj]
                if len(between):
                    kj = kj * jnp.exp(functools.reduce(jnp.add, [tot[m] for m in between]))
                ks.append(kj)
                vs.append(vb16[bj * C:(bj + 1) * C])
            v_cats[bi] = jnp.concatenate(vs, axis=0)
            atts[bi] = lax.dot_general(per_head_rows(ql[bi * C:(bi + 1) * C]),
                                       jnp.concatenate(ks, axis=0).astype(BF16), (((1,), (1,)), ((), ())),
                                       preferred_element_type=F32)

        k_hat = per_head_rows(k * jnp.exp(g_end - g))
        v_rows = jnp.concatenate([vb16[:, h * HEAD_DIM:(h + 1) * HEAD_DIM] for h in range(G_HEADS)], axis=0)
        s_new = st * jnp.exp(g_end) + lax.dot_general(v_rows, k_hat, (((0,), (0,)), ((), ())),
                                                      preferred_element_type=F32)
        yield

        pieces, spans = [], []
        for i in range(L):
            bi, il = divmod(i, C)
            lo, hi = ((il // 8) * 8, C) if d == 0 else (0, (il // 8) * 8 + 8)
            edge = (lo, lo + 8) if d == 0 else (hi - 8, hi)
            k_i = qk_ref[i:i + 1, G_QK:2 * G_QK]
            g_i = gl_scr[d, i:i + 1, :]
            vis = (rows8 >= il % 8) if d == 0 else (rows8 <= il % 8)
            diff = jnp.where(vis, gl[bi * C + edge[0]:bi * C + edge[1]] - g_i, neg_inf)
            if hi - lo > 8:
                other = gl[bi * C + 8:bi * C + 16] if d == 0 else gl[bi * C:bi * C + 8]
                diff = jnp.concatenate([diff, other - g_i] if d == 0 else [other - g_i, diff], axis=0)
            pieces.append((q[bi * C + lo:bi * C + hi] * k_i) * jnp.exp(diff))
            spans.append((lo, hi))
        w_all = jnp.dot(jnp.concatenate(pieces, axis=0).astype(BF16), head_expand, preferred_element_type=F32)
        yield
        cross = {bi: head_blocks(jnp.dot(att.astype(BF16), v_cats[bi], preferred_element_type=F32), C)
                 for bi, att in atts.items()}
        yield

        blocks = []
        w_off = 0
        for bi in range(nb):
            acc = inter[bi * C:(bi + 1) * C]
            if bi in cross:
                acc = acc + cross[bi]
            for il in range(C):
                i = bi * C + il
                lo, hi = spans[i]
                upd = acc[lo:hi] + w_all[w_off:w_off + hi - lo] * v_ref[i:i + 1, :]
                w_off += hi - lo
                parts = ([acc[:lo]] if lo > 0 else []) + [upd] + ([acc[hi:]] if hi < C else [])
                acc = jnp.concatenate(parts, axis=0) if len(parts) > 1 else upd
            blocks.append(acc)
        o_ref[...] = jnp.concatenate(blocks, axis=0)
        s_scr[d] = s_new

    stages = [direction(0, qkf, vf, lf, of), direction(1, qkb, vb, lb, ob)]
    for _ in range(5):
        for stage in stages:
            next(stage, None)
        yield


ATT_SUB_ROWS = 256
ATT_TQ = 1024
ATT_KEY_CHUNK = 1024


def _attn_kernel(*refs, has_lat, lam_init):
    if has_lat:
        q_ref, kc_ref, vc_ref, kl_ref, vl_ref, dl_ref, sub_ref, o_ref, s_scr, vo_scr = refs
        kv = ((kc_ref, vc_ref), (kl_ref, vl_ref))
    else:
        q_ref, kc_ref, vc_ref, dl_ref, sub_ref, o_ref, s_scr, vo_scr = refs
        kv = ((kc_ref, vc_ref),)

    @pl.when(pl.program_id(2) == 0)
    def _():
        off = 0
        for _, v_ref in kv:
            n = v_ref.shape[0]
            vo_scr[off:off + n, 0:HEAD_DIM] = v_ref[...]
            vo_scr[off:off + n, HEAD_DIM:2 * HEAD_DIM] = jnp.ones((n, HEAD_DIM), BF16)
            off += n

    n_sub = q_ref.shape[0] // ATT_SUB_ROWS
    lane = lax.broadcasted_iota(jnp.int32, (ATT_SUB_ROWS, HEAD_DIM), 1)
    zero = jnp.zeros((ATT_SUB_ROWS, HEAD_DIM), BF16)
    dl = dl_ref[...]
    lam = (jnp.exp(jnp.sum(dl[0:1] * dl[1:2], axis=-1, keepdims=True))
           - jnp.exp(jnp.sum(dl[2:3] * dl[3:4], axis=-1, keepdims=True)) + lam_init)
    key_chunks, off = [], 0
    for k_ref, _ in kv:
        for c0 in range(0, k_ref.shape[0], ATT_KEY_CHUNK):
            n = min(ATT_KEY_CHUNK, k_ref.shape[0] - c0)
            key_chunks.append((k_ref, c0, off + c0, n))
        off += k_ref.shape[0]

    prev = None
    for sb in range(n_sub + 1):
        if sb < n_sub:
            q = q_ref[sb * ATT_SUB_ROWS:(sb + 1) * ATT_SUB_ROWS, :]
            qms = (jnp.where(lane < 64, q, zero), jnp.where(lane >= 64, q, zero))
            part_max = [None, None]
        accs = [None, None]
        for k_ref, c0, off, n in key_chunks:
            for m in range(2):
                if sb < n_sub:
                    s = lax.dot_general(qms[m], k_ref[c0:c0 + n, :], (((1,), (1,)), ((), ())),
                                        preferred_element_type=F32)
                    s_scr[2 * sb + m, :, off:off + n] = s
                    folded = functools.reduce(jnp.maximum, [s[:, j:j + HEAD_DIM] for j in range(0, n, HEAD_DIM)])
                    part_max[m] = folded if part_max[m] is None else jnp.maximum(part_max[m], folded)
                if prev is not None:
                    p = jnp.exp2(s_scr[2 * (sb - 1) + m, :, off:off + n] - prev[m]).astype(BF16)
                    part = jnp.dot(p, vo_scr[off:off + n, :], preferred_element_type=F32)
                    accs[m] = part if accs[m] is None else accs[m] + part
        if prev is not None:
            outs = [a[:, 0:HEAD_DIM] / a[:, HEAD_DIM:HEAD_DIM + 1] for a in accs]
            out = outs[0] - lam * outs[1]
            ms = jnp.mean(out * out, axis=-1, keepdims=True)
            o_ref[(sb - 1) * ATT_SUB_ROWS:sb * ATT_SUB_ROWS, :] = (
                (out * lax.rsqrt(ms + NORM_EPS) * sub_ref[...]) * (1.0 - lam_init))
        prev = [jnp.max(pm, axis=-1, keepdims=True) for pm in part_max] if sb < n_sub else None


def _attn_call(qkv, d_lam, d_subln, lam_init, cfg, latent):
    nlb = cfg.n_lat // ATT_TQ
    ctx_row0 = cfg.B * cfg.n_lat // cfg.n_ctx
    kern = functools.partial(_attn_kernel, has_lat=latent, lam_init=lam_init)
    n_keys = cfg.n_ctx + (cfg.n_lat if latent else 0)
    kc = pl.BlockSpec((cfg.n_ctx, HEAD_DIM), lambda b, h, i: (ctx_row0 + b, D_HEADS + h))
    vc = pl.BlockSpec((cfg.n_ctx, HEAD_DIM), lambda b, h, i: (ctx_row0 + b, 2 * D_HEADS + h))
    small = [pl.BlockSpec((4, 64), lambda b, h, i: (0, 0)), pl.BlockSpec((1, HEAD_DIM), lambda b, h, i: (0, 0))]
    if latent:
        tq = ATT_TQ
        grid = (cfg.B, D_HEADS, nlb)
        q_spec = pl.BlockSpec((tq, HEAD_DIM), lambda b, h, i: (b * nlb + i, h))
        kv = [kc, vc,
              pl.BlockSpec((cfg.n_lat, HEAD_DIM), lambda b, h, i: (b, D_HEADS + h)),
              pl.BlockSpec((cfg.n_lat, HEAD_DIM), lambda b, h, i: (b, 2 * D_HEADS + h))]
        out_spec = pl.BlockSpec((tq, HEAD_DIM), lambda b, h, i: (b * nlb + i, h))
        n_rows = cfg.TL
    else:
        tq = cfg.n_ctx
        grid = (cfg.B, D_HEADS, 1)
        q_spec = pl.BlockSpec((tq, HEAD_DIM), lambda b, h, i: (ctx_row0 + b, h))
        kv = [kc, vc]
        out_spec = pl.BlockSpec((tq, HEAD_DIM), lambda b, h, i: (b, h))
        n_rows = cfg.B * cfg.n_ctx
    args = [qkv] * (1 + len(kv)) + [d_lam, d_subln.reshape(1, HEAD_DIM)]
    return pl.pallas_call(
        kern,
        grid=grid,
        in_specs=[q_spec] + kv + small,
        out_specs=out_spec,
        out_shape=jax.ShapeDtypeStruct((n_rows, D_HEADS * HEAD_DIM), F32),
        scratch_shapes=[pltpu.VMEM((2 * tq // ATT_SUB_ROWS, ATT_SUB_ROWS, n_keys), F32),
                        pltpu.VMEM((n_keys, 2 * HEAD_DIM), BF16)],
        compiler_params=_cparams(3),
        name="diff_attn_lat" if latent else "diff_attn_ctx",
    )(*args)


OUT_TILE = 256


def _group_rmsnorm(x, w, groups):
    parts = []
    for gi in range(groups):
        xs = x[:, gi * HEAD_DIM:(gi + 1) * HEAD_DIM]
        ms = jnp.mean(xs * xs, axis=-1, keepdims=True)
        parts.append(xs * lax.rsqrt(ms + NORM_EPS) * w[:, gi * HEAD_DIM:(gi + 1) * HEAD_DIM])
    return jnp.concatenate(parts, axis=-1)


def _out_kernel(*refs, n_src, n_lat_tiles, has_ctx):
    x_refs, refs = refs[:n_src], refs[n_src:]
    if has_ctx:
        hmf, hmb, hgf, hgb, hdl, hdc, mo, go, mn, gn, w_ref, nw_ref, gate_ref, o_ref = refs
        hd = jnp.where(pl.program_id(0) < n_lat_tiles, hdl[...], hdc[...])
    else:
        hmf, hmb, hgf, hgb, hdl, mo, go, mn, gn, w_ref, nw_ref, gate_ref, o_ref = refs
        hd = hdl[...]
    rows = lambda ref: ref[...].reshape(OUT_TILE, 512)
    ym = _group_rmsnorm(rows(hmf) + rows(hmb), mn[...], 4) * jax.nn.sigmoid(mo[...])
    yg = _group_rmsnorm(rows(hgf) + rows(hgb), gn[...], 4) * _silu(go[...])
    y = jnp.concatenate([ym.astype(BF16), yg.astype(BF16), hd.astype(BF16)], axis=-1)
    z = jnp.dot(y, w_ref[...], preferred_element_type=F32)
    ms = jnp.mean(z * z, axis=-1, keepdims=True)
    o_ref[...] = _token_tile(x_refs, n_lat_tiles) + gate_ref[0] * (z * lax.rsqrt(ms + NORM_EPS) * nw_ref[...])


def _out_call(x_parts, hmf, hmb, hgf, hgb, hd_lat, hd_ctx, p, m_norm, g_norm, w_out, layer, nw, mod, cfg, n_rows):
    d = cfg.D
    tm = OUT_TILE
    nl = cfg.TL // tm
    row = functools.partial(_mod_row, cfg=cfg, tm=tm)
    rt = lambda w, c: pl.BlockSpec((tm, w), lambda i: (i, c))
    const = lambda i: (0, 0)
    lt, ct = cfg.n_lat // tm, cfg.n_ctx // tm

    def scan_index(i):
        k = i - cfg.B * lt
        return (jnp.where(k < 0, ct + i % lt, k % ct), jnp.where(k < 0, i // lt, k // ct), 0, 0)

    scan_blk = pl.BlockSpec((tm // CHUNK, 1, CHUNK, 512), scan_index)
    hd_specs = [pl.BlockSpec((tm, 1024), lambda i: (jnp.minimum(i, nl - 1), 0))]
    hd_args = [hd_lat]
    if hd_ctx is not None:
        hd_specs.append(pl.BlockSpec((tm, 1024), lambda i: (jnp.maximum(i - nl, 0), 0)))
        hd_args.append(hd_ctx)
    return pl.pallas_call(
        functools.partial(_out_kernel, n_src=len(x_parts), n_lat_tiles=nl, has_ctx=hd_ctx is not None),
        grid=(n_rows // tm,),
        in_specs=_token_specs(x_parts, tm, cfg) + [scan_blk, scan_blk, scan_blk, scan_blk] + hd_specs + [
                  rt(512, P_BLK_M_O), rt(512, P_BLK_G_OUT),
                  pl.BlockSpec((1, 512), const), pl.BlockSpec((1, 512), const),
                  pl.BlockSpec((None, d, d), lambda i: (layer, 0, 0)), pl.BlockSpec((1, d), const),
                  pl.BlockSpec((1, 1, d), lambda i: (row(i) * 6 + 2, 0, 0))],
        out_specs=rt(d, 0),
        out_shape=jax.ShapeDtypeStruct((n_rows, d), F32),
        compiler_params=_cparams(1),
        name="out_proj",
    )(*x_parts, hmf, hmb, hgf, hgb, *hd_args, p, p, m_norm, g_norm, w_out, nw, mod)


def _ffn_kernel(x_ref, nw_ref, shift_ref, scale_ref, wg_ref, wu_ref, wd_ref, pw_ref, gate_ref, o_ref,
                h_scr, acc_scr):
    j = pl.program_id(1)

    @pl.when(j == 0)
    def _():
        h_scr[...] = _prenorm(x_ref[...], nw_ref[...], shift_ref[0], scale_ref[0]).astype(BF16)
        acc_scr[...] = jnp.zeros_like(acc_scr)

    h = h_scr[...]
    a = jnp.dot(h, wg_ref[...], preferred_element_type=F32)
    u = jnp.dot(h, wu_ref[...], preferred_element_type=F32)
    acc_scr[...] += jnp.dot((_silu(a) * u).astype(BF16), wd_ref[...], preferred_element_type=F32)

    @pl.when(j == pl.num_programs(1) - 1)
    def _():
        z = acc_scr[...]
        ms = jnp.mean(z * z, axis=-1, keepdims=True)
        o_ref[...] = x_ref[...] + gate_ref[0] * (z * lax.rsqrt(ms + NORM_EPS) * pw_ref[...])


FFN_LAT_TILE = 512
FFN_CTX_TILE = 512
FFN_HID_TILE = 512


def _ffn_call(x, nw_pre, nw_post, mod, wg, wu, wd, layer, cfg, row0, n_rows, tm):
    _, d, f = wg.shape
    tf = FFN_HID_TILE
    t0 = row0 // tm
    row = lambda i: _mod_row(i + t0, cfg, tm)
    const = lambda i, j: (0, 0)
    modspec = lambda kk: pl.BlockSpec((1, 1, d), lambda i, j: (row(i) * 6 + kk, 0, 0))
    return pl.pallas_call(
        _ffn_kernel,
        grid=(n_rows // tm, f // tf),
        in_specs=[pl.BlockSpec((tm, d), lambda i, j: (i + t0, 0)),
                  pl.BlockSpec((1, d), const), modspec(3), modspec(4),
                  pl.BlockSpec((None, d, tf), lambda i, j: (layer, 0, j)),
                  pl.BlockSpec((None, d, tf), lambda i, j: (layer, 0, j)),
                  pl.BlockSpec((None, tf, d), lambda i, j: (layer, j, 0)),
                  pl.BlockSpec((1, d), const), modspec(5)],
        out_specs=pl.BlockSpec((tm, d), lambda i, j: (i, 0)),
        out_shape=jax.ShapeDtypeStruct((n_rows, d), F32),
        scratch_shapes=[pltpu.VMEM((tm, d), BF16), pltpu.VMEM((tm, d), F32)],
        compiler_params=_cparams(2),
        name="ffn",
    )(x, nw_pre, mod, mod, wg, wu, wd, nw_post, mod)


_MIX = {}
_off = 0
for _name, _w in (("m_q", 512), ("m_k", 512), ("m_v", 512), ("m_o", 512), ("m_gates", 16),
                  ("g_q", 256), ("g_k", 256), ("g_v", 512), ("g_out", 512), ("g_lr", 32),
                  ("d_q", 1024), ("d_k", 1024), ("d_v", 1024)):
    _MIX[_name] = (_off, _w)
    _off += _w
_P_ORDER = ("m_q", "m_k", "m_v", "m_o", "g_q", "g_k", "g_v", "g_out")
_QKV_ORDER = ("d_q", "d_k", "d_v")


def _split_w_in(w_in):
    cols = lambda n: w_in[:, _MIX[n][0]:_MIX[n][0] + _MIX[n][1]]
    w_a = jnp.concatenate([cols(n) for n in _P_ORDER], axis=1).astype(BF16)
    w_b = jnp.concatenate([cols(n) for n in _QKV_ORDER], axis=1).astype(BF16)
    w_small = jnp.concatenate([cols("m_gates"), cols("g_lr"),
                               jnp.zeros((w_in.shape[0], HEAD_DIM - 48), w_in.dtype)], axis=1).astype(BF16)
    return w_a, w_b, w_small


def _rope_tables(cfg):
    rows = cfg.n_lat // GRID_W
    r = np.repeat(np.arange(rows, dtype=np.float64), GRID_W)
    c = np.tile(np.arange(GRID_W, dtype=np.float64), rows)
    half = 16
    inv_freq = ROPE_BASE ** (-np.arange(half, dtype=np.float64) / half)
    ang_r, ang_c = r[:, None] * inv_freq, c[:, None] * inv_freq
    ang = np.concatenate([ang_r, ang_r, ang_c, ang_c], axis=-1)
    ang = np.tile(ang, (cfg.B, 2))
    n_c = cfg.B * cfg.n_ctx
    cos_t = np.concatenate([np.cos(ang), np.ones((n_c, HEAD_DIM))], axis=0).astype(np.float32)
    sin_t = np.concatenate([np.sin(ang), np.zeros((n_c, HEAD_DIM))], axis=0).astype(np.float32)
    return jnp.asarray(cos_t), jnp.asarray(sin_t)


def _layer(x_parts, mod, lw, stacks, layer, lam_init, rope, cfg, need_ctx):
    d = cfg.D
    mq, mk, p, qkv, ps = _in_call(x_parts, lw["norm_mix_pre"].reshape(1, d), mod, *_split_w_in(lw["w_in"]), *rope,
                                  lw["mlstm_conv_w"], lw["mlstm_conv_b"], cfg)
    w2 = lw["gla_gate_w2"]
    w2p = jnp.zeros((HEAD_DIM, 2 * G_QK), F32)
    w2p = w2p.at[16:32, 0:G_QK].set(w2[0]).at[32:48, G_QK:].set(w2[1])
    hmf, hmb, hgf, hgb = _scan_call(mq, mk, p, ps, lw["mlstm_gate_b"], w2p,
                                    lw["gla_gate_b"].reshape(1, 2 * G_QK), cfg)

    hd_lat = _attn_call(qkv, lw["diff_lambda"], lw["diff_subln"], lam_init, cfg, latent=True)
    hd_ctx = _attn_call(qkv, lw["diff_lambda"], lw["diff_subln"], lam_init, cfg, latent=False) if need_ctx else None

    n_rows = cfg.T if need_ctx else cfg.TL
    xt = _out_call(x_parts, hmf, hmb, hgf, hgb, hd_lat, hd_ctx, p, lw["mlstm_norm"].reshape(1, 512),
                   lw["gla_norm"].reshape(1, 512), stacks["w_out"], layer,
                   lw["norm_mix_post"].reshape(1, d), mod, cfg, n_rows)
    ffn = functools.partial(_ffn_call, xt, lw["norm_ffn_pre"].reshape(1, d), lw["norm_ffn_post"].reshape(1, d),
                            mod, stacks["w_ffn_gate"], stacks["w_ffn_up"], stacks["w_ffn_down"], layer, cfg)
    out = (ffn(0, cfg.TL, FFN_LAT_TILE),)
    if need_ctx:
        out += (ffn(cfg.TL, cfg.T - cfg.TL, FFN_CTX_TILE),)
    return out


_LAYER_KEYS = ("norm_mix_pre", "norm_mix_post", "norm_ffn_pre", "norm_ffn_post", "w_in", "mlstm_conv_w",
               "mlstm_conv_b", "mlstm_gate_b", "mlstm_norm", "gla_gate_w2", "gla_gate_b", "gla_norm",
               "diff_lambda", "diff_subln", "w_out", "w_ffn_gate", "w_ffn_up", "w_ffn_down")
_STACK_KEYS = ("w_out", "w_ffn_gate", "w_ffn_up", "w_ffn_down")


def kernel(x, c, ctx, c_ctx, w_mod, b_mod, norm_mix_pre, norm_mix_post, norm_ffn_pre, norm_ffn_post, w_in, mlstm_conv_w, mlstm_conv_b, mlstm_gate_b, mlstm_norm, gla_gate_w2, gla_gate_b, gla_norm, diff_lambda, diff_subln, w_out, w_ffn_gate, w_ffn_up, w_ffn_down):
    weights = dict(zip(_LAYER_KEYS, (norm_mix_pre, norm_mix_post, norm_ffn_pre, norm_ffn_post, w_in,
                                     mlstm_conv_w, mlstm_conv_b, mlstm_gate_b, mlstm_norm, gla_gate_w2,
                                     gla_gate_b, gla_norm, diff_lambda, diff_subln, w_out, w_ffn_gate,
                                     w_ffn_up, w_ffn_down)))
    b, n_lat, d = x.shape
    cfg = Cfg(B=b, n_ctx=ctx.shape[1], n_lat=n_lat, D=d, F=w_ffn_gate.shape[-1])
    depth = w_mod.shape[0]
    c8 = jnp.zeros((8, d), F32).at[0].set(c_ctx).at[1:1 + b].set(c)
    mods = _mod_call(c8, w_mod, b_mod).reshape(depth, 8 * 6, 1, d)
    rope = _rope_tables(cfg)
    x_parts = (x.reshape(b * n_lat, d), ctx.reshape(b * ctx.shape[1], d))
    stacks = {k: weights.pop(k).astype(BF16) for k in _STACK_KEYS}
    for layer in range(depth):
        lw = {k: v[layer] for k, v in weights.items()}
        lam_init = 0.8 - 0.6 * math.exp(-0.3 * layer)
        x_parts = _layer(x_parts, mods[layer], lw, stacks, layer, lam_init, rope, cfg, need_ctx=layer < depth - 1)
    return x_parts[0].reshape(b, n_lat, d)
```

```python
import dataclasses
import functools
import math

import jax
import jax.numpy as jnp
import numpy as np
from jax import lax
from jax.experimental import pallas as pl
from jax.experimental.pallas import tpu as pltpu

F32 = jnp.float32
BF16 = jnp.bfloat16
NORM_EPS = 1e-6
CHUNK = 64
GRID_W = 64
ROPE_BASE = 10000.0
GLA_TAU = 16.0
HEAD_DIM = 128
VMEM_LIMIT_BYTES = 56 * 1024 * 1024
HIGHEST = lax.Precision.HIGHEST


@dataclasses.dataclass(frozen=True)
class Cfg:
    B: int = 2
    n_ctx: int = 256
    n_lat: int = 4096
    D: int = 2048
    F: int = 5632

    @property
    def T(self):
        return self.B * (self.n_ctx + self.n_lat)

    @property
    def TL(self):
        return self.B * self.n_lat


def _cparams(n_axes):
    return pltpu.CompilerParams(dimension_semantics=("arbitrary",) * n_axes,
                                vmem_limit_bytes=VMEM_LIMIT_BYTES)


def _mod_row(i, cfg, tm):
    lt = cfg.n_lat // tm
    return jnp.where(i < cfg.B * lt, 1 + i // lt, 0)


def _chunk_block(b, c, cfg):
    ncc, ncl = cfg.n_ctx // CHUNK, cfg.n_lat // CHUNK
    return jnp.where(c < ncc, cfg.B * ncl + b * ncc + c, b * ncl + (c - ncc))


def _bwd_chunk(s, cfg):
    ncc, ncl = cfg.n_ctx // CHUNK, cfg.n_lat // CHUNK
    return jnp.where(s < ncc, ncc - 1 - s, ncc + ncl - 1 - (s - ncc))


def _log_sigmoid(x):
    return jnp.minimum(x, 0.0) - jnp.log1p(jnp.exp(-jnp.abs(x)))


def _silu(x):
    return x * jax.nn.sigmoid(x)


MOD_K_TILE = 256


def _mod_kernel(c_ref, w0_ref, w1_ref, b_ref, o_ref):
    k = pl.program_id(1)

    @pl.when(k == 0)
    def _():
        o_ref[0] = jnp.broadcast_to(b_ref[0], o_ref.shape[1:])

    s = _silu(c_ref[...]).astype(BF16)
    half = MOD_K_TILE // 2
    o_ref[0] += (jnp.dot(s[:, 0:half], w0_ref[0].astype(BF16), preferred_element_type=F32)
                 + jnp.dot(s[:, half:], w1_ref[0].astype(BF16), preferred_element_type=F32))


def _mod_call(c8, w_mod, b_mod):
    depth, d, n = w_mod.shape
    tk = MOD_K_TILE
    return pl.pallas_call(
        _mod_kernel,
        grid=(depth, d // tk),
        in_specs=[pl.BlockSpec((8, tk), lambda l, k: (0, k)),
                  pl.BlockSpec((1, tk // 2, n), lambda l, k: (l, 2 * k, 0)),
                  pl.BlockSpec((1, tk // 2, n), lambda l, k: (l, 2 * k + 1, 0)),
                  pl.BlockSpec((1, 1, n), lambda l, k: (l, 0, 0))],
        out_specs=pl.BlockSpec((1, 8, n), lambda l, k: (l, 0, 0)),
        out_shape=jax.ShapeDtypeStruct((depth, 8, n), F32),
        compiler_params=_cparams(2),
        name="adaln_mod",
    )(c8, w_mod, w_mod, b_mod.reshape(depth, 1, n))


def _prenorm(x, nw, shift, scale):
    ms = jnp.mean(x * x, axis=-1, keepdims=True)
    return (x * lax.rsqrt(ms + NORM_EPS) * nw) * (1.0 + scale) + shift


D_HEADS = 8
P_COLS = 3584
M_CONV_COLS = 1024
P_BLK_M_V, P_BLK_M_O, P_BLK_G_QK, P_BLK_G_V, P_BLK_G_OUT = range(5)
QKV_COLS = 3072
IN_TILE = 256


def _token_specs(x_parts, tm, cfg):
    d = cfg.D
    if len(x_parts) == 1:
        return [pl.BlockSpec((tm, d), lambda i: (i, 0))]
    nl = cfg.TL // tm
    return [pl.BlockSpec((tm, d), lambda i: (jnp.minimum(i, nl - 1), 0)),
            pl.BlockSpec((tm, d), lambda i: (jnp.maximum(i - nl, 0), 0))]


def _token_tile(x_refs, n_lat_tiles):
    if len(x_refs) == 1:
        return x_refs[0][...]
    return jnp.where(pl.program_id(0) < n_lat_tiles, x_refs[0][...], x_refs[1][...])


def _halo_specs(x_parts, tm, cfg):
    d, r8 = cfg.D, tm // 8
    nl = cfg.TL // tm
    specs = []
    for part, tile0 in zip(x_parts, (0, nl)):
        last8 = part.shape[0] // 8 - 1
        clamp = lambda v, last8=last8: jnp.clip(v, 0, last8)
        specs.append(pl.BlockSpec((8, d), lambda i, t0=tile0, c=clamp: (c((i - t0) * r8 - 1), 0)))
        specs.append(pl.BlockSpec((8, d), lambda i, t0=tile0, c=clamp: (c((i - t0 + 1) * r8), 0)))
    return specs


def _in_kernel(*refs, n_src, n_lat_tiles, seg_first, seg_last):
    x_refs, halo_refs = refs[:n_src], refs[n_src:3 * n_src]
    (nw_ref, shift_ref, scale_ref, wa_ref, wb_ref, ws_ref, cos_ref, sin_ref, cw_ref, cb_ref,
     mq_ref, mk_ref, p_ref, qkv_ref, ps_ref) = refs[3 * n_src:]
    i = pl.program_id(0)
    norm = lambda x: _prenorm(x, nw_ref[...], shift_ref[0], scale_ref[0]).astype(BF16)
    h = norm(_token_tile(x_refs, n_lat_tiles))
    ps_ref[...] = jnp.dot(h, ws_ref[...], preferred_element_type=F32)
    p_ref[...] = jnp.dot(h, wa_ref[:, M_CONV_COLS:], preferred_element_type=F32)

    halo = jnp.concatenate([_token_tile(halo_refs[0::2], n_lat_tiles), _token_tile(halo_refs[1::2], n_lat_tiles)],
                           axis=0)
    pm = jnp.dot(jnp.concatenate([h, norm(halo)], axis=0), wa_ref[:, 0:M_CONV_COLS], preferred_element_type=F32)
    is_first = functools.reduce(jnp.logical_or, [i == s for s in seg_first])
    is_last = functools.reduce(jnp.logical_or, [i == s for s in seg_last])
    prev_row = jnp.where(is_first, 0.0, pm[IN_TILE + 7:IN_TILE + 8])
    next_row = jnp.where(is_last, 0.0, pm[IN_TILE + 8:IN_TILE + 9])
    xm = pm[0:IN_TILE]
    rows = lax.broadcasted_iota(jnp.int32, xm.shape, 0)
    x_prev = jnp.where(rows == 0, prev_row, pltpu.roll(xm, 1, 0))
    x_next = jnp.where(rows == IN_TILE - 1, next_row, pltpu.roll(xm, IN_TILE - 1, 0))
    conv = _silu(x_prev * cw_ref[0:1, :] + xm * cw_ref[1:2, :] + x_next * cw_ref[2:3, :] + cb_ref[...])
    mq_ref[...] = conv[:, 0:M_CONV_COLS // 2] * (HEAD_DIM ** -0.5)
    mk_ref[...] = conv[:, M_CONV_COLS // 2:]

    qkv = jnp.dot(h, wb_ref[...], preferred_element_type=F32)
    lane = lax.broadcasted_iota(jnp.int32, (IN_TILE, HEAD_DIM), 1)
    low = (lane % 32) < 16
    cos, sin = cos_ref[...], sin_ref[...]
    q_scale = 64 ** -0.5 * math.log2(math.e)
    for s in range(2 * D_HEADS):
        cs = slice(s * HEAD_DIM, (s + 1) * HEAD_DIM)
        x = qkv[:, cs]
        rot = jnp.where(low, -pltpu.roll(x, HEAD_DIM - 16, 1), pltpu.roll(x, 16, 1))
        y = x * cos + rot * sin
        qkv_ref[:, cs] = ((y * q_scale) if s < D_HEADS else y).astype(BF16)
    vs = slice(2 * D_HEADS * HEAD_DIM, QKV_COLS)
    qkv_ref[:, vs] = qkv[:, vs].astype(BF16)


def _in_call(x_parts, nw, mod, w_a, w_b, w_small, cos_t, sin_t, conv_w, conv_b, cfg):
    t, d = cfg.T, cfg.D
    tm = IN_TILE
    row = functools.partial(_mod_row, cfg=cfg, tm=tm)
    resident = lambda shape: pl.BlockSpec(shape, lambda i: (0, 0), pipeline_mode=pl.Buffered(1))
    lt, ct = cfg.n_lat // tm, cfg.n_ctx // tm
    seg_first = [b * lt for b in range(cfg.B)] + [cfg.B * lt + b * ct for b in range(cfg.B)]
    seg_last = [b * lt + lt - 1 for b in range(cfg.B)] + [cfg.B * lt + b * ct + ct - 1 for b in range(cfg.B)]
    half = M_CONV_COLS // 2
    tile = lambda w: pl.BlockSpec((tm, w), lambda i: (i, 0))
    return pl.pallas_call(
        functools.partial(_in_kernel, n_src=len(x_parts), n_lat_tiles=cfg.TL // tm,
                          seg_first=tuple(seg_first), seg_last=tuple(seg_last)),
        grid=(t // tm,),
        in_specs=_token_specs(x_parts, tm, cfg) + _halo_specs(x_parts, tm, cfg) + [
                  resident((1, d)),
                  pl.BlockSpec((1, 1, d), lambda i: (row(i) * 6 + 0, 0, 0)),
                  pl.BlockSpec((1, 1, d), lambda i: (row(i) * 6 + 1, 0, 0)),
                  resident((d, P_COLS)), resident((d, QKV_COLS)), resident((d, HEAD_DIM)),
                  tile(HEAD_DIM), tile(HEAD_DIM),
                  resident((3, M_CONV_COLS)), resident((1, M_CONV_COLS))],
        out_specs=[tile(half), tile(half), tile(P_COLS - M_CONV_COLS), tile(QKV_COLS), tile(HEAD_DIM)],
        out_shape=[jax.ShapeDtypeStruct((t, half), F32), jax.ShapeDtypeStruct((t, half), F32),
                   jax.ShapeDtypeStruct((t, P_COLS - M_CONV_COLS), F32),
                   jax.ShapeDtypeStruct((t, QKV_COLS), BF16), jax.ShapeDtypeStruct((t, HEAD_DIM), F32)],
        compiler_params=_cparams(1),
        name="in_proj",
    )(*x_parts, *[part for part in x_parts for _ in range(2)], nw, mod, mod, w_a, w_b, w_small, cos_t, sin_t,
      conv_w, conv_b.reshape(1, M_CONV_COLS))


M_HEADS = 4


def _split3(x):
    hi = x.astype(BF16)
    r1 = x - hi.astype(F32)
    mid = r1.astype(BF16)
    return hi, mid, (r1 - mid.astype(F32)).astype(BF16)


def _cumsum_matmul(mask, parts):
    return sum(jnp.dot(mask, part, preferred_element_type=F32) for part in parts)


def _mlstm_body(qf, kf, vf, gf, qb, kb, vb, gb, brow, bcol, of, ob, c_scr, m_scr):
    L = CHUNK
    row = lax.broadcasted_iota(jnp.int32, (L, L), 0)
    col = lax.broadcasted_iota(jnp.int32, (L, L), 1)
    neg_inf = jnp.float32(-jnp.inf)

    gcol = jnp.concatenate([gf[...], gb[...]], axis=0) + brow[...]
    grow = jnp.concatenate([gf[...], gb[...]], axis=0).T + bcol[...]
    r8 = lax.broadcasted_iota(jnp.int32, (8, 2 * L), 0)
    l8 = lax.broadcasted_iota(jnp.int32, (8, 2 * L), 1)
    fwd_row = r8 < M_HEADS
    own = (r8 // M_HEADS) == (l8 // L)
    ig = jnp.concatenate([grow[0:4], grow[8:12]], axis=0)
    logf = _log_sigmoid(jnp.concatenate([grow[4:8], grow[12:16]], axis=0))
    tl = lax.broadcasted_iota(jnp.int32, (2 * L, 2 * L), 0)
    ti = lax.broadcasted_iota(jnp.int32, (2 * L, 2 * L), 1)
    same_half = (tl // L) == (ti // L)
    scan_rows = jnp.logical_and(same_half, jnp.where(ti < L, tl - ti, ti - tl) <= 0)
    scan_cols = jnp.logical_and(same_half, jnp.where(tl < L, ti - tl, tl - ti) <= 0)
    cf = sum(jnp.dot(part, scan_rows.astype(BF16), preferred_element_type=F32) for part in _split3(logf))
    u = jnp.where(own, ig - cf, neg_inf)
    m_st = m_scr[...]
    end_lane = jnp.where(fwd_row, L - 1, L)
    f_end = jnp.broadcast_to(jnp.sum(jnp.where(l8 == end_lane, cf, 0.0), axis=-1, keepdims=True),
                             (8, 2 * L))
    dec = jnp.where(own, f_end - cf + ig, neg_inf)
    m_new = jnp.maximum(f_end + m_st, jnp.max(dec, axis=-1, keepdims=True))
    a_prev = jnp.exp(f_end + m_st - m_new)
    m_scr[...] = m_new
    u_keys = jnp.where(fwd_row, u, pltpu.roll(u, L, 1))
    cum_col = _cumsum_matmul(scan_cols.astype(BF16), _split3(_log_sigmoid(gcol)))

    rep = lambda colv: jnp.broadcast_to(colv, (L, HEAD_DIM))
    dir_refs = ((qf, kf, vf, of), (qb, kb, vb, ob))
    pairs = [(d, h) for d in range(2) for h in range(M_HEADS)]
    work = []
    for d, h in pairs:
        q_ref, k_ref, v_ref, _ = dir_refs[d]
        hs = slice(h * HEAD_DIM, (h + 1) * HEAD_DIM)
        q = q_ref[:, hs].astype(BF16)
        k32 = k_ref[:, hs]
        v1 = jnp.concatenate([v_ref[:, hs].astype(BF16), jnp.ones((L, HEAD_DIM), BF16)], axis=1)
        cn = c_scr[d, h]
        qk = lax.dot_general(q, k32.astype(BF16), (((1,), (1,)), ((), ())), preferred_element_type=F32)
        r_state = jnp.dot(q, cn.astype(BF16), preferred_element_type=F32)
        work.append((k32, v1, cn, qk, r_state))
    yield
    for (d, h), (k32, v1, cn, qk, r_state) in zip(pairs, work):
        o_ref = dir_refs[d][3]
        valid = (col <= row) if d == 0 else (col >= row)
        ts = slice(d * L, (d + 1) * L)
        r = d * M_HEADS + h
        ci, cfc = d * 8 + h, d * 8 + 4 + h
        hs = slice(h * HEAD_DIM, (h + 1) * HEAD_DIM)
        u_tile = jnp.where(valid, u_keys[r:r + 1, 0:L], neg_inf)
        m_rep = rep(jnp.maximum(jnp.max(u_tile, axis=-1, keepdims=True), m_st[r:r + 1, 0:1]))
        cf_rep = rep(cum_col[ts, cfc:cfc + 1])
        ig_rep = rep(gcol[ts, ci:ci + 1])
        w_inter = jnp.exp(m_st[r:r + 1] - m_rep)
        e = jnp.exp(u_tile - m_rep[:, 0:L])
        r_chunk = jnp.dot((qk * e).astype(BF16), v1, preferred_element_type=F32)
        num = w_inter * r_state[:, 0:HEAD_DIM] + r_chunk[:, 0:HEAD_DIM]
        den = w_inter * r_state[:, HEAD_DIM:] + r_chunk[:, HEAD_DIM:]
        o_ref[:, hs] = num / jnp.maximum(jnp.abs(den), jnp.exp(-(cf_rep + m_rep)))

        ws = jnp.exp(f_end[r:r + 1] - cf_rep + ig_rep - m_new[r:r + 1])
        wk = (ws * k32).astype(BF16)
        decay = jnp.concatenate([a_prev[r:r + 1], a_prev[r:r + 1]], axis=1)
        c_scr[d, h] = decay * cn + lax.dot_general(wk, v1, (((0,), (0,)), ((), ())),
                                                   preferred_element_type=F32)


N_MLSTM_REFS = 8
N_GLA_REFS = 6


def _scan_kernel(*refs, n_b):
    it = iter(refs)
    take = lambda n: [next(it) for _ in range(n)]
    m_in = [take(N_MLSTM_REFS) for _ in range(n_b)]
    brow, bcol = take(2)
    g_in = [take(N_GLA_REFS) for _ in range(n_b)]
    w2_ref, b2_ref = take(2)
    m_of, m_ob, g_of, g_ob = take(4)
    c_scr, m_scr, s_scr, gl_scr = take(4)

    @pl.when(pl.program_id(0) == 0)
    def _():
        c_scr[...] = jnp.zeros_like(c_scr)
        m_scr[...] = jnp.zeros_like(m_scr)
        s_scr[...] = jnp.zeros_like(s_scr)

    bodies = []
    for b in range(n_b):
        bodies.append(_mlstm_body(*m_in[b], brow, bcol, m_of.at[0, b], m_ob.at[0, b], c_scr.at[b], m_scr.at[b]))
        bodies.append(_gla_body(*g_in[b], w2_ref, b2_ref, g_of.at[0, b], g_ob.at[0, b], s_scr.at[b], gl_scr.at[b]))
    while bodies:
        bodies = [body for body in bodies if next(body, "done") != "done"]


def _scan_call(mq, mk, p, ps, gate_b, w2p, b2p, cfg):
    nc = (cfg.n_ctx + cfg.n_lat) // CHUNK
    fwd = lambda b, col: (lambda s: (_chunk_block(b, s, cfg), col))
    bwd = lambda b, col: (lambda s: (_chunk_block(b, _bwd_chunk(s, cfg), cfg), col))
    blk = lambda w, f: pl.BlockSpec((CHUNK, w), f)
    brow = jnp.zeros((1, HEAD_DIM), F32).at[0, :16].set(gate_b)
    const = lambda s: (0, 0)
    in_specs, args = [], []
    for b in range(cfg.B):
        for way in (fwd, bwd):
            in_specs += [blk(512, way(b, 0)), blk(512, way(b, 0)), blk(512, way(b, P_BLK_M_V)),
                         blk(HEAD_DIM, way(b, 0))]
            args += [mq, mk, p, ps]
    in_specs += [pl.BlockSpec((1, HEAD_DIM), const), pl.BlockSpec((HEAD_DIM, 1), const)]
    args += [brow, brow.reshape(HEAD_DIM, 1)]
    for b in range(cfg.B):
        for way in (fwd, bwd):
            in_specs += [blk(512, way(b, P_BLK_G_QK)), blk(512, way(b, P_BLK_G_V)), blk(HEAD_DIM, way(b, 0))]
            args += [p, p, ps]
    in_specs += [pl.BlockSpec((HEAD_DIM, 2 * G_QK), const), pl.BlockSpec((1, 2 * G_QK), const)]
    args += [w2p, b2p]
    out_f = pl.BlockSpec((1, cfg.B, CHUNK, 512), lambda s: (s, 0, 0, 0))
    out_b = pl.BlockSpec((1, cfg.B, CHUNK, 512), lambda s: (_bwd_chunk(s, cfg), 0, 0, 0))
    outs = pl.pallas_call(
        functools.partial(_scan_kernel, n_b=cfg.B),
        grid=(nc,),
        in_specs=in_specs,
        out_specs=[out_f, out_b, out_f, out_b],
        out_shape=[jax.ShapeDtypeStruct((nc, cfg.B, CHUNK, 512), F32)] * 4,
        scratch_shapes=[pltpu.VMEM((cfg.B, 2, M_HEADS, HEAD_DIM, 2 * HEAD_DIM), F32),
                        pltpu.VMEM((cfg.B, 2 * M_HEADS, 2 * CHUNK), F32),
                        pltpu.VMEM((cfg.B, 2, HEAD_DIM, G_QK), F32),
                        pltpu.VMEM((cfg.B, 2, CHUNK, G_QK), F32)],
        compiler_params=_cparams(1),
        name="recurrent_scan",
    )(*args)
    return outs


G_QK = 256
G_V = 512


G_HEADS = 4
G_DK = 64
GLA_BLK = 16


def _gla_body(qkf, vf, lf, qkb, vb, lb, w2_ref, b2_ref, of, ob, s_scr, gl_scr):
    L, C = CHUNK, GLA_BLK
    nb = L // C
    row = lax.broadcasted_iota(jnp.int32, (L, L), 0)
    col = lax.broadcasted_iota(jnp.int32, (L, L), 1)
    same_blk = (row // C) == (col // C)
    rows8 = lax.broadcasted_iota(jnp.int32, (8, G_QK), 0)
    neg_inf = jnp.float32(-jnp.inf)
    he_r = lax.broadcasted_iota(jnp.int32, (G_QK, G_V), 0) // G_DK
    he_c = lax.broadcasted_iota(jnp.int32, (G_QK, G_V), 1) // HEAD_DIM
    head_expand = (he_r == he_c).astype(BF16)

    def per_head_rows(x):
        lane_head = lax.broadcasted_iota(jnp.int32, x.shape, 1) // G_DK
        return jnp.concatenate([jnp.where(lane_head == h, x, 0.0) for h in range(G_HEADS)], axis=0).astype(BF16)

    def head_blocks(r, n):
        return jnp.concatenate([r[h * n:(h + 1) * n, h * HEAD_DIM:(h + 1) * HEAD_DIM] for h in range(G_HEADS)],
                               axis=1)

    def direction(d, qk_ref, v_ref, l_ref, o_ref):
        cs = slice(d * G_QK, (d + 1) * G_QK)
        z = jnp.dot(l_ref[...], w2_ref[:, cs], precision=HIGHEST, preferred_element_type=F32) + b2_ref[:, cs]
        log_a = _log_sigmoid(z) * (1.0 / GLA_TAU)
        tri = (col <= row) if d == 0 else (col >= row)
        log_a_parts = _split3(log_a)
        g = _cumsum_matmul(tri.astype(BF16), log_a_parts)
        gl = _cumsum_matmul(jnp.logical_and(tri, same_blk).astype(BF16), log_a_parts)
        gl_scr[d] = gl
        q = qk_ref[:, 0:G_QK] * (G_DK ** -0.5)
        k = qk_ref[:, G_QK:2 * G_QK]
        v = v_ref[...]
        vb16 = v.astype(BF16)
        g_end = g[L - 1:L, :] if d == 0 else g[0:1, :]
        st = s_scr[d]

        r = lax.dot_general(per_head_rows(q * jnp.exp(g)), st.astype(BF16), (((1,), (1,)), ((), ())),
                            preferred_element_type=F32)
        inter = jnp.concatenate([r[h * L:(h + 1) * L] for h in range(G_HEADS)], axis=1)
        yield

        ql = q * jnp.exp(gl)
        first = lambda b: b * C + (C - 1 if d == 0 else 0)
        tot = [gl[first(b):first(b) + 1] for b in range(nb)]
        k_end = [k[b * C:(b + 1) * C] * jnp.exp(tot[b] - gl[b * C:(b + 1) * C]) for b in range(nb)]

        atts, v_cats = {}, {}
        for bi in range(nb):
            earlier = list(range(bi)) if d == 0 else list(range(bi + 1, nb))
            if not earlier:
                continue
            ks, vs = [], []
            for bj in earlier:
                between = range(bj + 1, bi) if d == 0 else range(bi + 1, bj)
                kj = k_end[bj]
                if len(between):
                    kj = kj * jnp.exp(functools.reduce(jnp.add, [tot[m] for m in between]))
                ks.append(kj)
                vs.append(vb16[bj * C:(bj + 1) * C])
            v_cats[bi] = jnp.concatenate(vs, axis=0)
            atts[bi] = lax.dot_general(per_head_rows(ql[bi * C:(bi + 1) * C]),
                                       jnp.concatenate(ks, axis=0).astype(BF16), (((1,), (1,)), ((), ())),
                                       preferred_element_type=F32)

        k_hat = per_head_rows(k * jnp.exp(g_end - g))
        v_rows = jnp.concatenate([vb16[:, h * HEAD_DIM:(h + 1) * HEAD_DIM] for h in range(G_HEADS)], axis=0)
        s_new = st * jnp.exp(g_end) + lax.dot_general(v_rows, k_hat, (((0,), (0,)), ((), ())),
                                                      preferred_element_type=F32)
        yield

        pieces, spans = [], []
        for i in range(L):
            bi, il = divmod(i, C)
            lo, hi = ((il // 8) * 8, C) if d == 0 else (0, (il // 8) * 8 + 8)
            edge = (lo, lo + 8) if d == 0 else (hi - 8, hi)
            k_i = qk_ref[i:i + 1, G_QK:2 * G_QK]
            g_i = gl_scr[d, i:i + 1, :]
            vis = (rows8 >= il % 8) if d == 0 else (rows8 <= il % 8)
            diff = jnp.where(vis, gl[bi * C + edge[0]:bi * C + edge[1]] - g_i, neg_inf)
            if hi - lo > 8:
                other = gl[bi * C + 8:bi * C + 16] if d == 0 else gl[bi * C:bi * C + 8]
                diff = jnp.concatenate([diff, other - g_i] if d == 0 else [other - g_i, diff], axis=0)
            pieces.append((q[bi * C + lo:bi * C + hi] * k_i) * jnp.exp(diff))
            spans.append((lo, hi))
        w_all = jnp.dot(jnp.concatenate(pieces, axis=0).astype(BF16), head_expand, preferred_element_type=F32)
        yield
        cross = {bi: head_blocks(jnp.dot(att.astype(BF16), v_cats[bi], preferred_element_type=F32), C)
                 for bi, att in atts.items()}
        yield

        blocks = []
        w_off = 0
        for bi in range(nb):
            acc = inter[bi * C:(bi + 1) * C]
            if bi in cross:
                acc = acc + cross[bi]
            for il in range(C):
                i = bi * C + il
                lo, hi = spans[i]
                upd = acc[lo:hi] + w_all[w_off:w_off + hi - lo] * v_ref[i:i + 1, :]
                w_off += hi - lo
                parts = ([acc[:lo]] if lo > 0 else []) + [upd] + ([acc[hi:]] if hi < C else [])
                acc = jnp.concatenate(parts, axis=0) if len(parts) > 1 else upd
            blocks.append(acc)
        o_ref[...] = jnp.concatenate(blocks, axis=0)
        s_scr[d] = s_new

    stages = [direction(0, qkf, vf, lf, of), direction(1, qkb, vb, lb, ob)]
    for _ in range(5):
        for stage in stages:
            next(stage, None)
        yield


ATT_SUB_ROWS = 256
ATT_TQ = 1024
ATT_KEY_CHUNK = 1024


def _attn_kernel(*refs, has_lat, lam_init):
    if has_lat:
        q_ref, kc_ref, vc_ref, kl_ref, vl_ref, dl_ref, sub_ref, o_ref, s_scr, vo_scr = refs
        kv = ((kc_ref, vc_ref), (kl_ref, vl_ref))
    else:
        q_ref, kc_ref, vc_ref, dl_ref, sub_ref, o_ref, s_scr, vo_scr = refs
        kv = ((kc_ref, vc_ref),)

    @pl.when(pl.program_id(2) == 0)
    def _():
        off = 0
        for _, v_ref in kv:
            n = v_ref.shape[0]
            vo_scr[off:off + n, 0:HEAD_DIM] = v_ref[...]
            vo_scr[off:off + n, HEAD_DIM:2 * HEAD_DIM] = jnp.ones((n, HEAD_DIM), BF16)
            off += n

    n_sub = q_ref.shape[0] // ATT_SUB_ROWS
    lane = lax.broadcasted_iota(jnp.int32, (ATT_SUB_ROWS, HEAD_DIM), 1)
    zero = jnp.zeros((ATT_SUB_ROWS, HEAD_DIM), BF16)
    dl = dl_ref[...]
    lam = (jnp.exp(jnp.sum(dl[0:1] * dl[1:2], axis=-1, keepdims=True))
           - jnp.exp(jnp.sum(dl[2:3] * dl[3:4], axis=-1, keepdims=True)) + lam_init)
    key_chunks, off = [], 0
    for k_ref, _ in kv:
        for c0 in range(0, k_ref.shape[0], ATT_KEY_CHUNK):
            n = min(ATT_KEY_CHUNK, k_ref.shape[0] - c0)
            key_chunks.append((k_ref, c0, off + c0, n))
        off += k_ref.shape[0]

    prev = None
    for sb in range(n_sub + 1):
        if sb < n_sub:
            q = q_ref[sb * ATT_SUB_ROWS:(sb + 1) * ATT_SUB_ROWS, :]
            qms = (jnp.where(lane < 64, q, zero), jnp.where(lane >= 64, q, zero))
            part_max = [None, None]
        accs = [None, None]
        for k_ref, c0, off, n in key_chunks:
            for m in range(2):
                if sb < n_sub:
                    s = lax.dot_general(qms[m], k_ref[c0:c0 + n, :], (((1,), (1,)), ((), ())),
                                        preferred_element_type=F32)
                    s_scr[2 * sb + m, :, off:off + n] = s
                    folded = functools.reduce(jnp.maximum, [s[:, j:j + HEAD_DIM] for j in range(0, n, HEAD_DIM)])
                    part_max[m] = folded if part_max[m] is None else jnp.maximum(part_max[m], folded)
                if prev is not None:
                    p = jnp.exp2(s_scr[2 * (sb - 1) + m, :, off:off + n] - prev[m]).astype(BF16)
                    part = jnp.dot(p, vo_scr[off:off + n, :], preferred_element_type=F32)
                    accs[m] = part if accs[m] is None else accs[m] + part
        if prev is not None:
            outs = [a[:, 0:HEAD_DIM] / a[:, HEAD_DIM:HEAD_DIM + 1] for a in accs]
            out = outs[0] - lam * outs[1]
            ms = jnp.mean(out * out, axis=-1, keepdims=True)
            o_ref[(sb - 1) * ATT_SUB_ROWS:sb * ATT_SUB_ROWS, :] = (
                (out * lax.rsqrt(ms + NORM_EPS) * sub_ref[...]) * (1.0 - lam_init))
        prev = [jnp.max(pm, axis=-1, keepdims=True) for pm in part_max] if sb < n_sub else None


def _attn_call(qkv, d_lam, d_subln, lam_init, cfg, latent):
    nlb = cfg.n_lat // ATT_TQ
    ctx_row0 = cfg.B * cfg.n_lat // cfg.n_ctx
    kern = functools.partial(_attn_kernel, has_lat=latent, lam_init=lam_init)
    n_keys = cfg.n_ctx + (cfg.n_lat if latent else 0)
    kc = pl.BlockSpec((cfg.n_ctx, HEAD_DIM), lambda b, h, i: (ctx_row0 + b, D_HEADS + h))
    vc = pl.BlockSpec((cfg.n_ctx, HEAD_DIM), lambda b, h, i: (ctx_row0 + b, 2 * D_HEADS + h))
    small = [pl.BlockSpec((4, 64), lambda b, h, i: (0, 0)), pl.BlockSpec((1, HEAD_DIM), lambda b, h, i: (0, 0))]
    if latent:
        tq = ATT_TQ
        grid = (cfg.B, D_HEADS, nlb)
        q_spec = pl.BlockSpec((tq, HEAD_DIM), lambda b, h, i: (b * nlb + i, h))
        kv = [kc, vc,
              pl.BlockSpec((cfg.n_lat, HEAD_DIM), lambda b, h, i: (b, D_HEADS + h)),
              pl.BlockSpec((cfg.n_lat, HEAD_DIM), lambda b, h, i: (b, 2 * D_HEADS + h))]
        out_spec = pl.BlockSpec((tq, HEAD_DIM), lambda b, h, i: (b * nlb + i, h))
        n_rows = cfg.TL
    else:
        tq = cfg.n_ctx
        grid = (cfg.B, D_HEADS, 1)
        q_spec = pl.BlockSpec((tq, HEAD_DIM), lambda b, h, i: (ctx_row0 + b, h))
        kv = [kc, vc]
        out_spec = pl.BlockSpec((tq, HEAD_DIM), lambda b, h, i: (b, h))
        n_rows = cfg.B * cfg.n_ctx
    args = [qkv] * (1 + len(kv)) + [d_lam, d_subln.reshape(1, HEAD_DIM)]
    return pl.pallas_call(
        kern,
        grid=grid,
        in_specs=[q_spec] + kv + small,
        out_specs=out_spec,
        out_shape=jax.ShapeDtypeStruct((n_rows, D_HEADS * HEAD_DIM), F32),
        scratch_shapes=[pltpu.VMEM((2 * tq // ATT_SUB_ROWS, ATT_SUB_ROWS, n_keys), F32),
                        pltpu.VMEM((n_keys, 2 * HEAD_DIM), BF16)],
        compiler_params=_cparams(3),
        name="diff_attn_lat" if latent else "diff_attn_ctx",
    )(*args)


OUT_TILE = 256


def _group_rmsnorm(x, w, groups):
    parts = []
    for gi in range(groups):
        xs = x[:, gi * HEAD_DIM:(gi + 1) * HEAD_DIM]
        ms = jnp.mean(xs * xs, axis=-1, keepdims=True)
        parts.append(xs * lax.rsqrt(ms + NORM_EPS) * w[:, gi * HEAD_DIM:(gi + 1) * HEAD_DIM])
    return jnp.concatenate(parts, axis=-1)


def _out_kernel(*refs, n_src, n_lat_tiles, has_ctx):
    x_refs, refs = refs[:n_src], refs[n_src:]
    if has_ctx:
        hmf, hmb, hgf, hgb, hdl, hdc, mo, go, mn, gn, w_ref, nw_ref, gate_ref, o_ref = refs
        hd = jnp.where(pl.program_id(0) < n_lat_tiles, hdl[...], hdc[...])
    else:
        hmf, hmb, hgf, hgb, hdl, mo, go, mn, gn, w_ref, nw_ref, gate_ref, o_ref = refs
        hd = hdl[...]
    rows = lambda ref: ref[...].reshape(OUT_TILE, 512)
    ym = _group_rmsnorm(rows(hmf) + rows(hmb), mn[...], 4) * jax.nn.sigmoid(mo[...])
    yg = _group_rmsnorm(rows(hgf) + rows(hgb), gn[...], 4) * _silu(go[...])
    y = jnp.concatenate([ym.astype(BF16), yg.astype(BF16), hd.astype(BF16)], axis=-1)
    z = jnp.dot(y, w_ref[...], preferred_element_type=F32)
    ms = jnp.mean(z * z, axis=-1, keepdims=True)
    o_ref[...] = _token_tile(x_refs, n_lat_tiles) + gate_ref[0] * (z * lax.rsqrt(ms + NORM_EPS) * nw_ref[...])


def _out_call(x_parts, hmf, hmb, hgf, hgb, hd_lat, hd_ctx, p, m_norm, g_norm, w_out, layer, nw, mod, cfg, n_rows):
    d = cfg.D
    tm = OUT_TILE
    nl = cfg.TL // tm
    row = functools.partial(_mod_row, cfg=cfg, tm=tm)
    rt = lambda w, c: pl.BlockSpec((tm, w), lambda i: (i, c))
    const = lambda i: (0, 0)
    lt, ct = cfg.n_lat // tm, cfg.n_ctx // tm

    def scan_index(i):
        k = i - cfg.B * lt
        return (jnp.where(k < 0, ct + i % lt, k % ct), jnp.where(k < 0, i // lt, k // ct), 0, 0)

    scan_blk = pl.BlockSpec((tm // CHUNK, 1, CHUNK, 512), scan_index)
    hd_specs = [pl.BlockSpec((tm, 1024), lambda i: (jnp.minimum(i, nl - 1), 0))]
    hd_args = [hd_lat]
    if hd_ctx is not None:
        hd_specs.append(pl.BlockSpec((tm, 1024), lambda i: (jnp.maximum(i - nl, 0), 0)))
        hd_args.append(hd_ctx)
    return pl.pallas_call(
        functools.partial(_out_kernel, n_src=len(x_parts), n_lat_tiles=nl, has_ctx=hd_ctx is not None),
        grid=(n_rows // tm,),
        in_specs=_token_specs(x_parts, tm, cfg) + [scan_blk, scan_blk, scan_blk, scan_blk] + hd_specs + [
                  rt(512, P_BLK_M_O), rt(512, P_BLK_G_OUT),
                  pl.BlockSpec((1, 512), const), pl.BlockSpec((1, 512), const),
                  pl.BlockSpec((None, d, d), lambda i: (layer, 0, 0)), pl.BlockSpec((1, d), const),
                  pl.BlockSpec((1, 1, d), lambda i: (row(i) * 6 + 2, 0, 0))],
        out_specs=rt(d, 0),
        out_shape=jax.ShapeDtypeStruct((n_rows, d), F32),
        compiler_params=_cparams(1),
        name="out_proj",
    )(*x_parts, hmf, hmb, hgf, hgb, *hd_args, p, p, m_norm, g_norm, w_out, nw, mod)


def _ffn_kernel(x_ref, nw_ref, shift_ref, scale_ref, wg_ref, wu_ref, wd_ref, pw_ref, gate_ref, o_ref,
                h_scr, acc_scr):
    j = pl.program_id(1)

    @pl.when(j == 0)
    def _():
        h_scr[...] = _prenorm(x_ref[...], nw_ref[...], shift_ref[0], scale_ref[0]).astype(BF16)
        acc_scr[...] = jnp.zeros_like(acc_scr)

    h = h_scr[...]
    a = jnp.dot(h, wg_ref[...], preferred_element_type=F32)
    u = jnp.dot(h, wu_ref[...], preferred_element_type=F32)
    acc_scr[...] += jnp.dot((_silu(a) * u).astype(BF16), wd_ref[...], preferred_element_type=F32)

    @pl.when(j == pl.num_programs(1) - 1)
    def _():
        z = acc_scr[...]
        ms = jnp.mean(z * z, axis=-1, keepdims=True)
        o_ref[...] = x_ref[...] + gate_ref[0] * (z * lax.rsqrt(ms + NORM_EPS) * pw_ref[...])


FFN_LAT_TILE = 512
FFN_CTX_TILE = 512
FFN_HID_TILE = 512


def _ffn_call(x, nw_pre, nw_post, mod, wg, wu, wd, layer, cfg, row0, n_rows, tm):
    _, d, f = wg.shape
    tf = FFN_HID_TILE
    t0 = row0 // tm
    row = lambda i: _mod_row(i + t0, cfg, tm)
    const = lambda i, j: (0, 0)
    modspec = lambda kk: pl.BlockSpec((1, 1, d), lambda i, j: (row(i) * 6 + kk, 0, 0))
    return pl.pallas_call(
        _ffn_kernel,
        grid=(n_rows // tm, f // tf),
        in_specs=[pl.BlockSpec((tm, d), lambda i, j: (i + t0, 0)),
                  pl.BlockSpec((1, d), const), modspec(3), modspec(4),
                  pl.BlockSpec((None, d, tf), lambda i, j: (layer, 0, j)),
                  pl.BlockSpec((None, d, tf), lambda i, j: (layer, 0, j)),
                  pl.BlockSpec((None, tf, d), lambda i, j: (layer, j, 0)),
                  pl.BlockSpec((1, d), const), modspec(5)],
        out_specs=pl.BlockSpec((tm, d), lambda i, j: (i, 0)),
        out_shape=jax.ShapeDtypeStruct((n_rows, d), F32),
        scratch_shapes=[pltpu.VMEM((tm, d), BF16), pltpu.VMEM((tm, d), F32)],
        compiler_params=_cparams(2),
        name="ffn",
    )(x, nw_pre, mod, mod, wg, wu, wd, nw_post, mod)


_MIX = {}
_off = 0
for _name, _w in (("m_q", 512), ("m_k", 512), ("m_v", 512), ("m_o", 512), ("m_gates", 16),
                  ("g_q", 256), ("g_k", 256), ("g_v", 512), ("g_out", 512), ("g_lr", 32),
                  ("d_q", 1024), ("d_k", 1024), ("d_v", 1024)):
    _MIX[_name] = (_off, _w)
    _off += _w
_P_ORDER = ("m_q", "m_k", "m_v", "m_o", "g_q", "g_k", "g_v", "g_out")
_QKV_ORDER = ("d_q", "d_k", "d_v")


def _split_w_in(w_in):
    cols = lambda n: w_in[:, _MIX[n][0]:_MIX[n][0] + _MIX[n][1]]
    w_a = jnp.concatenate([cols(n) for n in _P_ORDER], axis=1).astype(BF16)
    w_b = jnp.concatenate([cols(n) for n in _QKV_ORDER], axis=1).astype(BF16)
    w_small = jnp.concatenate([cols("m_gates"), cols("g_lr"),
                               jnp.zeros((w_in.shape[0], HEAD_DIM - 48), w_in.dtype)], axis=1).astype(BF16)
    return w_a, w_b, w_small


def _rope_tables(cfg):
    rows = cfg.n_lat // GRID_W
    r = np.repeat(np.arange(rows, dtype=np.float64), GRID_W)
    c = np.tile(np.arange(GRID_W, dtype=np.float64), rows)
    half = 16
    inv_freq = ROPE_BASE ** (-np.arange(half, dtype=np.float64) / half)
    ang_r, ang_c = r[:, None] * inv_freq, c[:, None] * inv_freq
    ang = np.concatenate([ang_r, ang_r, ang_c, ang_c], axis=-1)
    ang = np.tile(ang, (cfg.B, 2))
    n_c = cfg.B * cfg.n_ctx
    cos_t = np.concatenate([np.cos(ang), np.ones((n_c, HEAD_DIM))], axis=0).astype(np.float32)
    sin_t = np.concatenate([np.sin(ang), np.zeros((n_c, HEAD_DIM))], axis=0).astype(np.float32)
    return jnp.asarray(cos_t), jnp.asarray(sin_t)


def _layer(x_parts, mod, lw, stacks, layer, lam_init, rope, cfg, need_ctx):
    d = cfg.D
    mq, mk, p, qkv, ps = _in_call(x_parts, lw["norm_mix_pre"].reshape(1, d), mod, *_split_w_in(lw["w_in"]), *rope,
                                  lw["mlstm_conv_w"], lw["mlstm_conv_b"], cfg)
    w2 = lw["gla_gate_w2"]
    w2p = jnp.zeros((HEAD_DIM, 2 * G_QK), F32)
    w2p = w2p.at[16:32, 0:G_QK].set(w2[0]).at[32:48, G_QK:].set(w2[1])
    hmf, hmb, hgf, hgb = _scan_call(mq, mk, p, ps, lw["mlstm_gate_b"], w2p,
                                    lw["gla_gate_b"].reshape(1, 2 * G_QK), cfg)

    hd_lat = _attn_call(qkv, lw["diff_lambda"], lw["diff_subln"], lam_init, cfg, latent=True)
    hd_ctx = _attn_call(qkv, lw["diff_lambda"], lw["diff_subln"], lam_init, cfg, latent=False) if need_ctx else None

    n_rows = cfg.T if need_ctx else cfg.TL
    xt = _out_call(x_parts, hmf, hmb, hgf, hgb, hd_lat, hd_ctx, p, lw["mlstm_norm"].reshape(1, 512),
                   lw["gla_norm"].reshape(1, 512), stacks["w_out"], layer,
                   lw["norm_mix_post"].reshape(1, d), mod, cfg, n_rows)
    ffn = functools.partial(_ffn_call, xt, lw["norm_ffn_pre"].reshape(1, d), lw["norm_ffn_post"].reshape(1, d),
                            mod, stacks["w_ffn_gate"], stacks["w_ffn_up"], stacks["w_ffn_down"], layer, cfg)
    out = (ffn(0, cfg.TL, FFN_LAT_TILE),)
    if need_ctx:
        out += (ffn(cfg.TL, cfg.T - cfg.TL, FFN_CTX_TILE),)
    return out


_LAYER_KEYS = ("norm_mix_pre", "norm_mix_post", "norm_ffn_pre", "norm_ffn_post", "w_in", "mlstm_conv_w",
               "mlstm_conv_b", "mlstm_gate_b", "mlstm_norm", "gla_gate_w2", "gla_gate_b", "gla_norm",
               "diff_lambda", "diff_subln", "w_out", "w_ffn_gate", "w_ffn_up", "w_ffn_down")
_STACK_KEYS = ("w_out", "w_ffn_gate", "w_ffn_up", "w_ffn_down")


def kernel(x, c, ctx, c_ctx, w_mod, b_mod, norm_mix_pre, norm_mix_post, norm_ffn_pre, norm_ffn_post, w_in, mlstm_conv_w, mlstm_conv_b, mlstm_gate_b, mlstm_norm, gla_gate_w2, gla_gate_b, gla_norm, diff_lambda, diff_subln, w_out, w_ffn_gate, w_ffn_up, w_ffn_down):
    weights = dict(zip(_LAYER_KEYS, (norm_mix_pre, norm_mix_post, norm_ffn_pre, norm_ffn_post, w_in,
                                     mlstm_conv_w, mlstm_conv_b, mlstm_gate_b, mlstm_norm, gla_gate_w2,
                                     gla_gate_b, gla_norm, diff_lambda, diff_subln, w_out, w_ffn_gate,
                                     w_ffn_up, w_ffn_down)))
    b, n_lat, d = x.shape
    cfg = Cfg(B=b, n_ctx=ctx.shape[1], n_lat=n_lat, D=d, F=w_ffn_gate.shape[-1])
    depth = w_mod.shape[0]
    c8 = jnp.zeros((8, d), F32).at[0].set(c_ctx).at[1:1 + b].set(c)
    mods = _mod_call(c8, w_mod, b_mod).reshape(depth, 8 * 6, 1, d)
    rope = _rope_tables(cfg)
    x_parts = (x.reshape(b * n_lat, d), ctx.reshape(b * ctx.shape[1], d))
    stacks = {k: weights.pop(k).astype(BF16) for k in _STACK_KEYS}
    for layer in range(depth):
        lw = {k: v[layer] for k, v in weights.items()}
        lam_init = 0.8 - 0.6 * math.exp(-0.3 * layer)
        x_parts = _layer(x_parts, mods[layer], lw, stacks, layer, lam_init, rope, cfg, need_ctx=layer < depth - 1)
    return x_parts[0].reshape(b, n_lat, d)
```

```python
import dataclasses
import functools
import math

import jax
import jax.numpy as jnp
import numpy as np
from jax import lax
from jax.experimental import pallas as pl
from jax.experimental.pallas import tpu as pltpu

F32 = jnp.float32
BF16 = jnp.bfloat16
NORM_EPS = 1e-6
CHUNK = 64
GRID_W = 64
ROPE_BASE = 10000.0
GLA_TAU = 16.0
HEAD_DIM = 128
VMEM_LIMIT_BYTES = 56 * 1024 * 1024
HIGHEST = lax.Precision.HIGHEST


@dataclasses.dataclass(frozen=True)
class Cfg:
    B: int = 2
    n_ctx: int = 256
    n_lat: int = 4096
    D: int = 2048
    F: int = 5632

    @property
    def T(self):
        return self.B * (self.n_ctx + self.n_lat)

    @property
    def TL(self):
        return self.B * self.n_lat


def _cparams(n_axes):
    return pltpu.CompilerParams(dimension_semantics=("arbitrary",) * n_axes,
                                vmem_limit_bytes=VMEM_LIMIT_BYTES)


def _mod_row(i, cfg, tm):
    lt = cfg.n_lat // tm
    return jnp.where(i < cfg.B * lt, 1 + i // lt, 0)


def _chunk_block(b, c, cfg):
    ncc, ncl = cfg.n_ctx // CHUNK, cfg.n_lat // CHUNK
    return jnp.where(c < ncc, cfg.B * ncl + b * ncc + c, b * ncl + (c - ncc))


def _bwd_chunk(s, cfg):
    ncc, ncl = cfg.n_ctx // CHUNK, cfg.n_lat // CHUNK
    return jnp.where(s < ncc, ncc - 1 - s, ncc + ncl - 1 - (s - ncc))


def _log_sigmoid(x):
    return jnp.minimum(x, 0.0) - jnp.log1p(jnp.exp(-jnp.abs(x)))


def _silu(x):
    return x * jax.nn.sigmoid(x)


MOD_K_TILE = 256


def _mod_kernel(c_ref, w0_ref, w1_ref, b_ref, o_ref):
    k = pl.program_id(1)

    @pl.when(k == 0)
    def _():
        o_ref[0] = jnp.broadcast_to(b_ref[0], o_ref.shape[1:])

    s = _silu(c_ref[...]).astype(BF16)
    half = MOD_K_TILE // 2
    o_ref[0] += (jnp.dot(s[:, 0:half], w0_ref[0].astype(BF16), preferred_element_type=F32)
                 + jnp.dot(s[:, half:], w1_ref[0].astype(BF16), preferred_element_type=F32))


def _mod_call(c8, w_mod, b_mod):
    depth, d, n = w_mod.shape
    tk = MOD_K_TILE
    return pl.pallas_call(
        _mod_kernel,
        grid=(depth, d // tk),
        in_specs=[pl.BlockSpec((8, tk), lambda l, k: (0, k)),
                  pl.BlockSpec((1, tk // 2, n), lambda l, k: (l, 2 * k, 0)),
                  pl.BlockSpec((1, tk // 2, n), lambda l, k: (l, 2 * k + 1, 0)),
                  pl.BlockSpec((1, 1, n), lambda l, k: (l, 0, 0))],
        out_specs=pl.BlockSpec((1, 8, n), lambda l, k: (l, 0, 0)),
        out_shape=jax.ShapeDtypeStruct((depth, 8, n), F32),
        compiler_params=_cparams(2),
        name="adaln_mod",
    )(c8, w_mod, w_mod, b_mod.reshape(depth, 1, n))


def _prenorm(x, nw, shift, scale):
    ms = jnp.mean(x * x, axis=-1, keepdims=True)
    return (x * lax.rsqrt(ms + NORM_EPS) * nw) * (1.0 + scale) + shift


D_HEADS = 8
P_COLS = 3584
M_CONV_COLS = 1024
P_BLK_M_V, P_BLK_M_O, P_BLK_G_QK, P_BLK_G_V, P_BLK_G_OUT = range(5)
QKV_COLS = 3072
IN_TILE = 256


def _token_specs(x_parts, tm, cfg):
    d = cfg.D
    if len(x_parts) == 1:
        return [pl.BlockSpec((tm, d), lambda i: (i, 0))]
    nl = cfg.TL // tm
    return [pl.BlockSpec((tm, d), lambda i: (jnp.minimum(i, nl - 1), 0)),
            pl.BlockSpec((tm, d), lambda i: (jnp.maximum(i - nl, 0), 0))]


def _token_tile(x_refs, n_lat_tiles):
    if len(x_refs) == 1:
        return x_refs[0][...]
    return jnp.where(pl.program_id(0) < n_lat_tiles, x_refs[0][...], x_refs[1][...])


def _halo_specs(x_parts, tm, cfg):
    d, r8 = cfg.D, tm // 8
    nl = cfg.TL // tm
    specs = []
    for part, tile0 in zip(x_parts, (0, nl)):
        last8 = part.shape[0] // 8 - 1
        clamp = lambda v, last8=last8: jnp.clip(v, 0, last8)
        specs.append(pl.BlockSpec((8, d), lambda i, t0=tile0, c=clamp: (c((i - t0) * r8 - 1), 0)))
        specs.append(pl.BlockSpec((8, d), lambda i, t0=tile0, c=clamp: (c((i - t0 + 1) * r8), 0)))
    return specs


def _in_kernel(*refs, n_src, n_lat_tiles, seg_first, seg_last):
    x_refs, halo_refs = refs[:n_src], refs[n_src:3 * n_src]
    (nw_ref, shift_ref, scale_ref, wa_ref, wb_ref, ws_ref, cos_ref, sin_ref, cw_ref, cb_ref,
     mq_ref, mk_ref, p_ref, qkv_ref, ps_ref) = refs[3 * n_src:]
    i = pl.program_id(0)
    norm = lambda x: _prenorm(x, nw_ref[...], shift_ref[0], scale_ref[0]).astype(BF16)
    h = norm(_token_tile(x_refs, n_lat_tiles))
    ps_ref[...] = jnp.dot(h, ws_ref[...], preferred_element_type=F32)
    p_ref[...] = jnp.dot(h, wa_ref[:, M_CONV_COLS:], preferred_element_type=F32)

    halo = jnp.concatenate([_token_tile(halo_refs[0::2], n_lat_tiles), _token_tile(halo_refs[1::2], n_lat_tiles)],
                           axis=0)
    pm = jnp.dot(jnp.concatenate([h, norm(halo)], axis=0), wa_ref[:, 0:M_CONV_COLS], preferred_element_type=F32)
    is_first = functools.reduce(jnp.logical_or, [i == s for s in seg_first])
    is_last = functools.reduce(jnp.logical_or, [i == s for s in seg_last])
    prev_row = jnp.where(is_first, 0.0, pm[IN_TILE + 7:IN_TILE + 8])
    next_row = jnp.where(is_last, 0.0, pm[IN_TILE + 8:IN_TILE + 9])
    xm = pm[0:IN_TILE]
    rows = lax.broadcasted_iota(jnp.int32, xm.shape, 0)
    x_prev = jnp.where(rows == 0, prev_row, pltpu.roll(xm, 1, 0))
    x_next = jnp.where(rows == IN_TILE - 1, next_row, pltpu.roll(xm, IN_TILE - 1, 0))
    conv = _silu(x_prev * cw_ref[0:1, :] + xm * cw_ref[1:2, :] + x_next * cw_ref[2:3, :] + cb_ref[...])
    mq_ref[...] = conv[:, 0:M_CONV_COLS // 2] * (HEAD_DIM ** -0.5)
    mk_ref[...] = conv[:, M_CONV_COLS // 2:]

    qkv = jnp.dot(h, wb_ref[...], preferred_element_type=F32)
    lane = lax.broadcasted_iota(jnp.int32, (IN_TILE, HEAD_DIM), 1)
    low = (lane % 32) < 16
    cos, sin = cos_ref[...], sin_ref[...]
    q_scale = 64 ** -0.5 * math.log2(math.e)
    for s in range(2 * D_HEADS):
        cs = slice(s * HEAD_DIM, (s + 1) * HEAD_DIM)
        x = qkv[:, cs]
        rot = jnp.where(low, -pltpu.roll(x, HEAD_DIM - 16, 1), pltpu.roll(x, 16, 1))
        y = x * cos + rot * sin
        qkv_ref[:, cs] = ((y * q_scale) if s < D_HEADS else y).astype(BF16)
    vs = slice(2 * D_HEADS * HEAD_DIM, QKV_COLS)
    qkv_ref[:, vs] = qkv[:, vs].astype(BF16)


def _in_call(x_parts, nw, mod, w_a, w_b, w_small, cos_t, sin_t, conv_w, conv_b, cfg):
    t, d = cfg.T, cfg.D
    tm = IN_TILE
    row = functools.partial(_mod_row, cfg=cfg, tm=tm)
    resident = lambda shape: pl.BlockSpec(shape, lambda i: (0, 0), pipeline_mode=pl.Buffered(1))
    lt, ct = cfg.n_lat // tm, cfg.n_ctx // tm
    seg_first = [b * lt for b in range(cfg.B)] + [cfg.B * lt + b * ct for b in range(cfg.B)]
    seg_last = [b * lt + lt - 1 for b in range(cfg.B)] + [cfg.B * lt + b * ct + ct - 1 for b in range(cfg.B)]
    half = M_CONV_COLS // 2
    tile = lambda w: pl.BlockSpec((tm, w), lambda i: (i, 0))
    return pl.pallas_call(
        functools.partial(_in_kernel, n_src=len(x_parts), n_lat_tiles=cfg.TL // tm,
                          seg_first=tuple(seg_first), seg_last=tuple(seg_last)),
        grid=(t // tm,),
        in_specs=_token_specs(x_parts, tm, cfg) + _halo_specs(x_parts, tm, cfg) + [
                  resident((1, d)),
                  pl.BlockSpec((1, 1, d), lambda i: (row(i) * 6 + 0, 0, 0)),
                  pl.BlockSpec((1, 1, d), lambda i: (row(i) * 6 + 1, 0, 0)),
                  resident((d, P_COLS)), resident((d, QKV_COLS)), resident((d, HEAD_DIM)),
                  tile(HEAD_DIM), tile(HEAD_DIM),
                  resident((3, M_CONV_COLS)), resident((1, M_CONV_COLS))],
        out_specs=[tile(half), tile(half), tile(P_COLS - M_CONV_COLS), tile(QKV_COLS), tile(HEAD_DIM)],
        out_shape=[jax.ShapeDtypeStruct((t, half), F32), jax.ShapeDtypeStruct((t, half), F32),
                   jax.ShapeDtypeStruct((t, P_COLS - M_CONV_COLS), F32),
                   jax.ShapeDtypeStruct((t, QKV_COLS), BF16), jax.ShapeDtypeStruct((t, HEAD_DIM), F32)],
        compiler_params=_cparams(1),
        name="in_proj",
    )(*x_parts, *[part for part in x_parts for _ in range(2)], nw, mod, mod, w_a, w_b, w_small, cos_t, sin_t,
      conv_w, conv_b.reshape(1, M_CONV_COLS))


M_HEADS = 4


def _split3(x):
    hi = x.astype(BF16)
    r1 = x - hi.astype(F32)
    mid = r1.astype(BF16)
    return hi, mid, (r1 - mid.astype(F32)).astype(BF16)


def _cumsum_matmul(mask, parts):
    return sum(jnp.dot(mask, part, preferred_element_type=F32) for part in parts)


def _mlstm_body(qf, kf, vf, gf, qb, kb, vb, gb, brow, bcol, of, ob, c_scr, m_scr):
    L = CHUNK
    row = lax.broadcasted_iota(jnp.int32, (L, L), 0)
    col = lax.broadcasted_iota(jnp.int32, (L, L), 1)
    neg_inf = jnp.float32(-jnp.inf)

    gcol = jnp.concatenate([gf[...], gb[...]], axis=0) + brow[...]
    grow = jnp.concatenate([gf[...], gb[...]], axis=0).T + bcol[...]
    r8 = lax.broadcasted_iota(jnp.int32, (8, 2 * L), 0)
    l8 = lax.broadcasted_iota(jnp.int32, (8, 2 * L), 1)
    fwd_row = r8 < M_HEADS
    own = (r8 // M_HEADS) == (l8 // L)
    ig = jnp.concatenate([grow[0:4], grow[8:12]], axis=0)
    logf = _log_sigmoid(jnp.concatenate([grow[4:8], grow[12:16]], axis=0))
    tl = lax.broadcasted_iota(jnp.int32, (2 * L, 2 * L), 0)
    ti = lax.broadcasted_iota(jnp.int32, (2 * L, 2 * L), 1)
    same_half = (tl // L) == (ti // L)
    scan_rows = jnp.logical_and(same_half, jnp.where(ti < L, tl - ti, ti - tl) <= 0)
    scan_cols = jnp.logical_and(same_half, jnp.where(tl < L, ti - tl, tl - ti) <= 0)
    cf = sum(jnp.dot(part, scan_rows.astype(BF16), preferred_element_type=F32) for part in _split3(logf))
    u = jnp.where(own, ig - cf, neg_inf)
    m_st = m_scr[...]
    end_lane = jnp.where(fwd_row, L - 1, L)
    f_end = jnp.broadcast_to(jnp.sum(jnp.where(l8 == end_lane, cf, 0.0), axis=-1, keepdims=True),
                             (8, 2 * L))
    dec = jnp.where(own, f_end - cf + ig, neg_inf)
    m_new = jnp.maximum(f_end + m_st, jnp.max(dec, axis=-1, keepdims=True))
    a_prev = jnp.exp(f_end + m_st - m_new)
    m_scr[...] = m_new
    u_keys = jnp.where(fwd_row, u, pltpu.roll(u, L, 1))
    cum_col = _cumsum_matmul(scan_cols.astype(BF16), _split3(_log_sigmoid(gcol)))

    rep = lambda colv: jnp.broadcast_to(colv, (L, HEAD_DIM))
    dir_refs = ((qf, kf, vf, of), (qb, kb, vb, ob))
    pairs = [(d, h) for d in range(2) for h in range(M_HEADS)]
    work = []
    for d, h in pairs:
        q_ref, k_ref, v_ref, _ = dir_refs[d]
        hs = slice(h * HEAD_DIM, (h + 1) * HEAD_DIM)
        q = q_ref[:, hs].astype(BF16)
        k32 = k_ref[:, hs]
        v1 = jnp.concatenate([v_ref[:, hs].astype(BF16), jnp.ones((L, HEAD_DIM), BF16)], axis=1)
        cn = c_scr[d, h]
        qk = lax.dot_general(q, k32.astype(BF16), (((1,), (1,)), ((), ())), preferred_element_type=F32)
        r_state = jnp.dot(q, cn.astype(BF16), preferred_element_type=F32)
        work.append((k32, v1, cn, qk, r_state))
    yield
    for (d, h), (k32, v1, cn, qk, r_state) in zip(pairs, work):
        o_ref = dir_refs[d][3]
        valid = (col <= row) if d == 0 else (col >= row)
        ts = slice(d * L, (d + 1) * L)
        r = d * M_HEADS + h
        ci, cfc = d * 8 + h, d * 8 + 4 + h
        hs = slice(h * HEAD_DIM, (h + 1) * HEAD_DIM)
        u_tile = jnp.where(valid, u_keys[r:r + 1, 0:L], neg_inf)
        m_rep = rep(jnp.maximum(jnp.max(u_tile, axis=-1, keepdims=True), m_st[r:r + 1, 0:1]))
        cf_rep = rep(cum_col[ts, cfc:cfc + 1])
        ig_rep = rep(gcol[ts, ci:ci + 1])
        w_inter = jnp.exp(m_st[r:r + 1] - m_rep)
        e = jnp.exp(u_tile - m_rep[:, 0:L])
        r_chunk = jnp.dot((qk * e).astype(BF16), v1, preferred_element_type=F32)
        num = w_inter * r_state[:, 0:HEAD_DIM] + r_chunk[:, 0:HEAD_DIM]
        den = w_inter * r_state[:, HEAD_DIM:] + r_chunk[:, HEAD_DIM:]
        o_ref[:, hs] = num / jnp.maximum(jnp.abs(den), jnp.exp(-(cf_rep + m_rep)))

        ws = jnp.exp(f_end[r:r + 1] - cf_rep + ig_rep - m_new[r:r + 1])
        wk = (ws * k32).astype(BF16)
        decay = jnp.concatenate([a_prev[r:r + 1], a_prev[r:r + 1]], axis=1)
        c_scr[d, h] = decay * cn + lax.dot_general(wk, v1, (((0,), (0,)), ((), ())),
                                                   preferred_element_type=F32)


N_MLSTM_REFS = 8
N_GLA_REFS = 6


def _scan_kernel(*refs, n_b):
    it = iter(refs)
    take = lambda n: [next(it) for _ in range(n)]
    m_in = [take(N_MLSTM_REFS) for _ in range(n_b)]
    brow, bcol = take(2)
    g_in = [take(N_GLA_REFS) for _ in range(n_b)]
    w2_ref, b2_ref = take(2)
    m_of, m_ob, g_of, g_ob = take(4)
    c_scr, m_scr, s_scr, gl_scr = take(4)

    @pl.when(pl.program_id(0) == 0)
    def _():
        c_scr[...] = jnp.zeros_like(c_scr)
        m_scr[...] = jnp.zeros_like(m_scr)
        s_scr[...] = jnp.zeros_like(s_scr)

    bodies = []
    for b in range(n_b):
        bodies.append(_mlstm_body(*m_in[b], brow, bcol, m_of.at[0, b], m_ob.at[0, b], c_scr.at[b], m_scr.at[b]))
        bodies.append(_gla_body(*g_in[b], w2_ref, b2_ref, g_of.at[0, b], g_ob.at[0, b], s_scr.at[b], gl_scr.at[b]))
    while bodies:
        bodies = [body for body in bodies if next(body, "done") != "done"]


def _scan_call(mq, mk, p, ps, gate_b, w2p, b2p, cfg):
    nc = (cfg.n_ctx + cfg.n_lat) // CHUNK
    fwd = lambda b, col: (lambda s: (_chunk_block(b, s, cfg), col))
    bwd = lambda b, col: (lambda s: (_chunk_block(b, _bwd_chunk(s, cfg), cfg), col))
    blk = lambda w, f: pl.BlockSpec((CHUNK, w), f)
    brow = jnp.zeros((1, HEAD_DIM), F32).at[0, :16].set(gate_b)
    const = lambda s: (0, 0)
    in_specs, args = [], []
    for b in range(cfg.B):
        for way in (fwd, bwd):
            in_specs += [blk(512, way(b, 0)), blk(512, way(b, 0)), blk(512, way(b, P_BLK_M_V)),
                         blk(HEAD_DIM, way(b, 0))]
            args += [mq, mk, p, ps]
    in_specs += [pl.BlockSpec((1, HEAD_DIM), const), pl.BlockSpec((HEAD_DIM, 1), const)]
    args += [brow, brow.reshape(HEAD_DIM, 1)]
    for b in range(cfg.B):
        for way in (fwd, bwd):
            in_specs += [blk(512, way(b, P_BLK_G_QK)), blk(512, way(b, P_BLK_G_V)), blk(HEAD_DIM, way(b, 0))]
            args += [p, p, ps]
    in_specs += [pl.BlockSpec((HEAD_DIM, 2 * G_QK), const), pl.BlockSpec((1, 2 * G_QK), const)]
    args += [w2p, b2p]
    out_f = pl.BlockSpec((1, cfg.B, CHUNK, 512), lambda s: (s, 0, 0, 0))
    out_b = pl.BlockSpec((1, cfg.B, CHUNK, 512), lambda s: (_bwd_chunk(s, cfg), 0, 0, 0))
    outs = pl.pallas_call(
        functools.partial(_scan_kernel, n_b=cfg.B),
        grid=(nc,),
        in_specs=in_specs,
        out_specs=[out_f, out_b, out_f, out_b],
        out_shape=[jax.ShapeDtypeStruct((nc, cfg.B, CHUNK, 512), F32)] * 4,
        scratch_shapes=[pltpu.VMEM((cfg.B, 2, M_HEADS, HEAD_DIM, 2 * HEAD_DIM), F32),
                        pltpu.VMEM((cfg.B, 2 * M_HEADS, 2 * CHUNK), F32),
                        pltpu.VMEM((cfg.B, 2, HEAD_DIM, G_QK), F32),
                        pltpu.VMEM((cfg.B, 2, CHUNK, G_QK), F32)],
        compiler_params=_cparams(1),
        name="recurrent_scan",
    )(*args)
    return outs


G_QK = 256
G_V = 512


G_HEADS = 4
G_DK = 64
GLA_BLK = 16


def _gla_body(qkf, vf, lf, qkb, vb, lb, w2_ref, b2_ref, of, ob, s_scr, gl_scr):
    L, C = CHUNK, GLA_BLK
    nb = L // C
    row = lax.broadcasted_iota(jnp.int32, (L, L), 0)
    col = lax.broadcasted_iota(jnp.int32, (L, L), 1)
    same_blk = (row // C) == (col // C)
    rows8 = lax.broadcasted_iota(jnp.int32, (8, G_QK), 0)
    neg_inf = jnp.float32(-jnp.inf)
    he_r = lax.broadcasted_iota(jnp.int32, (G_QK, G_V), 0) // G_DK
    he_c = lax.broadcasted_iota(jnp.int32, (G_QK, G_V), 1) // HEAD_DIM
    head_expand = (he_r == he_c).astype(BF16)

    def per_head_rows(x):
        lane_head = lax.broadcasted_iota(jnp.int32, x.shape, 1) // G_DK
        return jnp.concatenate([jnp.where(lane_head == h, x, 0.0) for h in range(G_HEADS)], axis=0).astype(BF16)

    def head_blocks(r, n):
        return jnp.concatenate([r[h * n:(h + 1) * n, h * HEAD_DIM:(h + 1) * HEAD_DIM] for h in range(G_HEADS)],
                               axis=1)

    def direction(d, qk_ref, v_ref, l_ref, o_ref):
        cs = slice(d * G_QK, (d + 1) * G_QK)
        z = jnp.dot(l_ref[...], w2_ref[:, cs], precision=HIGHEST, preferred_element_type=F32) + b2_ref[:, cs]
        log_a = _log_sigmoid(z) * (1.0 / GLA_TAU)
        tri = (col <= row) if d == 0 else (col >= row)
        log_a_parts = _split3(log_a)
        g = _cumsum_matmul(tri.astype(BF16), log_a_parts)
        gl = _cumsum_matmul(jnp.logical_and(tri, same_blk).astype(BF16), log_a_parts)
        gl_scr[d] = gl
        q = qk_ref[:, 0:G_QK] * (G_DK ** -0.5)
        k = qk_ref[:, G_QK:2 * G_QK]
        v = v_ref[...]
        vb16 = v.astype(BF16)
        g_end = g[L - 1:L, :] if d == 0 else g[0:1, :]
        st = s_scr[d]

        r = lax.dot_general(per_head_rows(q * jnp.exp(g)), st.astype(BF16), (((1,), (1,)), ((), ())),
                            preferred_element_type=F32)
        inter = jnp.concatenate([r[h * L:(h + 1) * L] for h in range(G_HEADS)], axis=1)
        yield

        ql = q * jnp.exp(gl)
        first = lambda b: b * C + (C - 1 if d == 0 else 0)
        tot = [gl[first(b):first(b) + 1] for b in range(nb)]
        k_end = [k[b * C:(b + 1) * C] * jnp.exp(tot[b] - gl[b * C:(b + 1) * C]) for b in range(nb)]

        atts, v_cats = {}, {}
        for bi in range(nb):
            earlier = list(range(bi)) if d == 0 else list(range(bi + 1, nb))
            if not earlier:
                continue
            ks, vs = [], []
            for bj in earlier:
                between = range(bj + 1, bi) if d == 0 else range(bi + 1, bj)
                kj = k_end[bj]
                if len(between):
                    kj = kj * jnp.exp(functools.reduce(jnp.add, [tot[m] for m in between]))
                ks.append(kj)
                vs.append(vb16[bj * C:(bj + 1) * C])
            v_cats[bi] = jnp.concatenate(vs, axis=0)
            atts[bi] = lax.dot_general(per_head_rows(ql[bi * C:(bi + 1) * C]),
                                       jnp.concatenate(ks, axis=0).astype(BF16), (((1,), (1,)), ((), ())),
                                       preferred_element_type=F32)

        k_hat = per_head_rows(k * jnp.exp(g_end - g))
        v_rows = jnp.concatenate([vb16[:, h * HEAD_DIM:(h + 1) * HEAD_DIM] for h in range(G_HEADS)], axis=0)
        s_new = st * jnp.exp(g_end) + lax.dot_general(v_rows, k_hat, (((0,), (0,)), ((), ())),
                                                      preferred_element_type=F32)
        yield

        pieces, spans = [], []
        for i in range(L):
            bi, il = divmod(i, C)
            lo, hi = ((il // 8) * 8, C) if d == 0 else (0, (il // 8) * 8 + 8)
            edge = (lo, lo + 8) if d == 0 else (hi - 8, hi)
            k_i = qk_ref[i:i + 1, G_QK:2 * G_QK]
            g_i = gl_scr[d, i:i + 1, :]
            vis = (rows8 >= il % 8) if d == 0 else (rows8 <= il % 8)
            diff = jnp.where(vis, gl[bi * C + edge[0]:bi * C + edge[1]] - g_i, neg_inf)
            if hi - lo > 8:
                other = gl[bi * C + 8:bi * C + 16] if d == 0 else gl[bi * C:bi * C + 8]
                diff = jnp.concatenate([diff, other - g_i] if d == 0 else [other - g_i, diff], axis=0)
            pieces.append((q[bi * C + lo:bi * C + hi] * k_i) * jnp.exp(diff))
            spans.append((lo, hi))
        w_all = jnp.dot(jnp.concatenate(pieces, axis=0).astype(BF16), head_expand, preferred_element_type=F32)
        yield
        cross = {bi: head_blocks(jnp.dot(att.astype(BF16), v_cats[bi], preferred_element_type=F32), C)
                 for bi, att in atts.items()}
        yield

        blocks = []
        w_off = 0
        for bi in range(nb):
            acc = inter[bi * C:(bi + 1) * C]
            if bi in cross:
                acc = acc + cross[bi]
            for il in range(C):
                i = bi * C + il
                lo, hi = spans[i]
                upd = acc[lo:hi] + w_all[w_off:w_off + hi - lo] * v_ref[i:i + 1, :]
                w_off += hi - lo
                parts = ([acc[:lo]] if lo > 0 else []) + [upd] + ([acc[hi:]] if hi < C else [])
                acc = jnp.concatenate(parts, axis=0) if len(parts) > 1 else upd
            blocks.append(acc)
        o_ref[...] = jnp.concatenate(blocks, axis=0)
        s_scr[d] = s_new

    stages = [direction(0, qkf, vf, lf, of), direction(1, qkb, vb, lb, ob)]
    for _ in range(5):
        for stage in stages:
            next(stage, None)
        yield


ATT_SUB_ROWS = 256
ATT_TQ = 1024
ATT_KEY_CHUNK = 1024


def _attn_kernel(*refs, has_lat, lam_init, n_cast):
    if has_lat:
        q_ref, kc_ref, vc_ref, kl_ref, vl_ref, dl_ref, sub_ref = refs[:7]
        kv = ((kc_ref, vc_ref), (kl_ref, vl_ref))
        refs = refs[7:]
    else:
        q_ref, kc_ref, vc_ref, dl_ref, sub_ref = refs[:5]
        kv = ((kc_ref, vc_ref),)
        refs = refs[5:]
    cast_in, refs = refs[:n_cast], refs[n_cast:]
    o_ref, refs = refs[0], refs[1:]
    cast_out, (s_scr, vo_scr) = refs[:n_cast], refs[n_cast:]
    for src, dst in zip(cast_in, cast_out):
        dst[...] = src[...].astype(BF16)

    @pl.when(pl.program_id(2) == 0)
    def _():
        off = 0
        for _, v_ref in kv:
            n = v_ref.shape[0]
            vo_scr[off:off + n, 0:HEAD_DIM] = v_ref[...]
            vo_scr[off:off + n, HEAD_DIM:2 * HEAD_DIM] = jnp.ones((n, HEAD_DIM), BF16)
            off += n

    n_sub = q_ref.shape[0] // ATT_SUB_ROWS
    lane = lax.broadcasted_iota(jnp.int32, (ATT_SUB_ROWS, HEAD_DIM), 1)
    zero = jnp.zeros((ATT_SUB_ROWS, HEAD_DIM), BF16)
    dl = dl_ref[...]
    lam = (jnp.exp(jnp.sum(dl[0:1] * dl[1:2], axis=-1, keepdims=True))
           - jnp.exp(jnp.sum(dl[2:3] * dl[3:4], axis=-1, keepdims=True)) + lam_init)
    key_chunks, off = [], 0
    for k_ref, _ in kv:
        for c0 in range(0, k_ref.shape[0], ATT_KEY_CHUNK):
            n = min(ATT_KEY_CHUNK, k_ref.shape[0] - c0)
            key_chunks.append((k_ref, c0, off + c0, n))
        off += k_ref.shape[0]

    prev = None
    for sb in range(n_sub + 1):
        if sb < n_sub:
            q = q_ref[sb * ATT_SUB_ROWS:(sb + 1) * ATT_SUB_ROWS, :]
            qms = (jnp.where(lane < 64, q, zero), jnp.where(lane >= 64, q, zero))
            part_max = [None, None]
        accs = [None, None]
        for k_ref, c0, off, n in key_chunks:
            for m in range(2):
                if sb < n_sub:
                    s = lax.dot_general(qms[m], k_ref[c0:c0 + n, :], (((1,), (1,)), ((), ())),
                                        preferred_element_type=F32)
                    s_scr[2 * sb + m, :, off:off + n] = s
                    folded = functools.reduce(jnp.maximum, [s[:, j:j + HEAD_DIM] for j in range(0, n, HEAD_DIM)])
                    part_max[m] = folded if part_max[m] is None else jnp.maximum(part_max[m], folded)
                if prev is not None:
                    p = jnp.exp2(s_scr[2 * (sb - 1) + m, :, off:off + n] - prev[m]).astype(BF16)
                    part = jnp.dot(p, vo_scr[off:off + n, :], preferred_element_type=F32)
                    accs[m] = part if accs[m] is None else accs[m] + part
        if prev is not None:
            outs = [a[:, 0:HEAD_DIM] / a[:, HEAD_DIM:HEAD_DIM + 1] for a in accs]
            out = outs[0] - lam * outs[1]
            ms = jnp.mean(out * out, axis=-1, keepdims=True)
            o_ref[(sb - 1) * ATT_SUB_ROWS:sb * ATT_SUB_ROWS, :] = (
                (out * lax.rsqrt(ms + NORM_EPS) * sub_ref[...]) * (1.0 - lam_init))
        prev = [jnp.max(pm, axis=-1, keepdims=True) for pm in part_max] if sb < n_sub else None


def _attn_call(qkv, d_lam, d_subln, lam_init, cfg, latent, cast=None):
    nlb = cfg.n_lat // ATT_TQ
    ctx_row0 = cfg.B * cfg.n_lat // cfg.n_ctx
    layer, cast_srcs = cast if cast is not None else (0, [])
    kern = functools.partial(_attn_kernel, has_lat=latent, lam_init=lam_init, n_cast=len(cast_srcs))
    n_keys = cfg.n_ctx + (cfg.n_lat if latent else 0)
    kc = pl.BlockSpec((cfg.n_ctx, HEAD_DIM), lambda b, h, i: (ctx_row0 + b, D_HEADS + h))
    vc = pl.BlockSpec((cfg.n_ctx, HEAD_DIM), lambda b, h, i: (ctx_row0 + b, 2 * D_HEADS + h))
    small = [pl.BlockSpec((4, 64), lambda b, h, i: (0, 0)), pl.BlockSpec((1, HEAD_DIM), lambda b, h, i: (0, 0))]
    if latent:
        tq = ATT_TQ
        grid = (cfg.B, D_HEADS, nlb)
        q_spec = pl.BlockSpec((tq, HEAD_DIM), lambda b, h, i: (b * nlb + i, h))
        kv = [kc, vc,
              pl.BlockSpec((cfg.n_lat, HEAD_DIM), lambda b, h, i: (b, D_HEADS + h)),
              pl.BlockSpec((cfg.n_lat, HEAD_DIM), lambda b, h, i: (b, 2 * D_HEADS + h))]
        out_spec = pl.BlockSpec((tq, HEAD_DIM), lambda b, h, i: (b * nlb + i, h))
        n_rows = cfg.TL
    else:
        tq = cfg.n_ctx
        grid = (cfg.B, D_HEADS, 1)
        q_spec = pl.BlockSpec((tq, HEAD_DIM), lambda b, h, i: (ctx_row0 + b, h))
        kv = [kc, vc]
        out_spec = pl.BlockSpec((tq, HEAD_DIM), lambda b, h, i: (b, h))
        n_rows = cfg.B * cfg.n_ctx
    args = [qkv] * (1 + len(kv)) + [d_lam, d_subln.reshape(1, HEAD_DIM)]
    n_steps = grid[0] * grid[1] * grid[2]
    step = lambda b, h, i: (b * grid[1] + h) * grid[2] + i
    cast_in, cast_out, cast_shapes = [], [], []
    for w in cast_srcs:
        _, r, c = w.shape
        per = 1
        while (r * per) % n_steps or (r * per // n_steps) % 16:
            per *= 2
        slab = r * per // n_steps
        cast_in.append(pl.BlockSpec((None, slab, c), lambda b, h, i, per=per: (layer, step(b, h, i) // per, 0)))
        cast_out.append(pl.BlockSpec((slab, c), lambda b, h, i, per=per: (step(b, h, i) // per, 0)))
        cast_shapes.append(jax.ShapeDtypeStruct((r, c), BF16))
    outs = pl.pallas_call(
        kern,
        grid=grid,
        in_specs=[q_spec] + kv + small + cast_in,
        out_specs=[out_spec] + cast_out,
        out_shape=[jax.ShapeDtypeStruct((n_rows, D_HEADS * HEAD_DIM), F32)] + cast_shapes,
        scratch_shapes=[pltpu.VMEM((2 * tq // ATT_SUB_ROWS, ATT_SUB_ROWS, n_keys), F32),
                        pltpu.VMEM((n_keys, 2 * HEAD_DIM), BF16)],
        compiler_params=_cparams(3),
        name="diff_attn_lat" if latent else "diff_attn_ctx",
    )(*args, *cast_srcs)
    return outs if cast_srcs else outs[0]


OUT_TILE = 256


def _group_rmsnorm(x, w, groups):
    parts = []
    for gi in range(groups):
        xs = x[:, gi * HEAD_DIM:(gi + 1) * HEAD_DIM]
        ms = jnp.mean(xs * xs, axis=-1, keepdims=True)
        parts.append(xs * lax.rsqrt(ms + NORM_EPS) * w[:, gi * HEAD_DIM:(gi + 1) * HEAD_DIM])
    return jnp.concatenate(parts, axis=-1)


def _out_kernel(*refs, n_src, n_lat_tiles, has_ctx):
    x_refs, refs = refs[:n_src], refs[n_src:]
    if has_ctx:
        hmf, hmb, hgf, hgb, hdl, hdc, mo, go, mn, gn, w_ref, nw_ref, gate_ref, o_ref = refs
        hd = jnp.where(pl.program_id(0) < n_lat_tiles, hdl[...], hdc[...])
    else:
        hmf, hmb, hgf, hgb, hdl, mo, go, mn, gn, w_ref, nw_ref, gate_ref, o_ref = refs
        hd = hdl[...]
    rows = lambda ref: ref[...].reshape(OUT_TILE, 512)
    ym = _group_rmsnorm(rows(hmf) + rows(hmb), mn[...], 4) * jax.nn.sigmoid(mo[...])
    yg = _group_rmsnorm(rows(hgf) + rows(hgb), gn[...], 4) * _silu(go[...])
    y = jnp.concatenate([ym.astype(BF16), yg.astype(BF16), hd.astype(BF16)], axis=-1)
    z = jnp.dot(y, w_ref[...], preferred_element_type=F32)
    ms = jnp.mean(z * z, axis=-1, keepdims=True)
    o_ref[...] = _token_tile(x_refs, n_lat_tiles) + gate_ref[0] * (z * lax.rsqrt(ms + NORM_EPS) * nw_ref[...])


def _out_call(x_parts, hmf, hmb, hgf, hgb, hd_lat, hd_ctx, p, m_norm, g_norm, w_out, layer, nw, mod, cfg, n_rows):
    d = cfg.D
    tm = OUT_TILE
    nl = cfg.TL // tm
    row = functools.partial(_mod_row, cfg=cfg, tm=tm)
    rt = lambda w, c: pl.BlockSpec((tm, w), lambda i: (i, c))
    const = lambda i: (0, 0)
    lt, ct = cfg.n_lat // tm, cfg.n_ctx // tm

    def scan_index(i):
        k = i - cfg.B * lt
        return (jnp.where(k < 0, ct + i % lt, k % ct), jnp.where(k < 0, i // lt, k // ct), 0, 0)

    scan_blk = pl.BlockSpec((tm // CHUNK, 1, CHUNK, 512), scan_index)
    hd_specs = [pl.BlockSpec((tm, 1024), lambda i: (jnp.minimum(i, nl - 1), 0))]
    hd_args = [hd_lat]
    if hd_ctx is not None:
        hd_specs.append(pl.BlockSpec((tm, 1024), lambda i: (jnp.maximum(i - nl, 0), 0)))
        hd_args.append(hd_ctx)
    return pl.pallas_call(
        functools.partial(_out_kernel, n_src=len(x_parts), n_lat_tiles=nl, has_ctx=hd_ctx is not None),
        grid=(n_rows // tm,),
        in_specs=_token_specs(x_parts, tm, cfg) + [scan_blk, scan_blk, scan_blk, scan_blk] + hd_specs + [
                  rt(512, P_BLK_M_O), rt(512, P_BLK_G_OUT),
                  pl.BlockSpec((1, 512), const), pl.BlockSpec((1, 512), const),
                  pl.BlockSpec((None, d, d), lambda i: (layer, 0, 0)), pl.BlockSpec((1, d), const),
                  pl.BlockSpec((1, 1, d), lambda i: (row(i) * 6 + 2, 0, 0))],
        out_specs=rt(d, 0),
        out_shape=jax.ShapeDtypeStruct((n_rows, d), F32),
        compiler_params=_cparams(1),
        name="out_proj",
    )(*x_parts, hmf, hmb, hgf, hgb, *hd_args, p, p, m_norm, g_norm, w_out, nw, mod)


def _ffn_kernel(x_ref, nw_ref, shift_ref, scale_ref, wg_ref, wu_ref, wd_ref, pw_ref, gate_ref, o_ref,
                h_scr, acc_scr):
    j = pl.program_id(1)

    @pl.when(j == 0)
    def _():
        h_scr[...] = _prenorm(x_ref[...], nw_ref[...], shift_ref[0], scale_ref[0]).astype(BF16)
        acc_scr[...] = jnp.zeros_like(acc_scr)

    h = h_scr[...]
    a = jnp.dot(h, wg_ref[...], preferred_element_type=F32)
    u = jnp.dot(h, wu_ref[...], preferred_element_type=F32)
    acc_scr[...] += jnp.dot((_silu(a) * u).astype(BF16), wd_ref[...], preferred_element_type=F32)

    @pl.when(j == pl.num_programs(1) - 1)
    def _():
        z = acc_scr[...]
        ms = jnp.mean(z * z, axis=-1, keepdims=True)
        o_ref[...] = x_ref[...] + gate_ref[0] * (z * lax.rsqrt(ms + NORM_EPS) * pw_ref[...])


FFN_LAT_TILE = 512
FFN_CTX_TILE = 512
FFN_HID_TILE = 512


def _ffn_call(x, nw_pre, nw_post, mod, wg, wu, wd, cfg, row0, n_rows, tm):
    d, f = wg.shape
    tf = FFN_HID_TILE
    t0 = row0 // tm
    row = lambda i: _mod_row(i + t0, cfg, tm)
    const = lambda i, j: (0, 0)
    modspec = lambda kk: pl.BlockSpec((1, 1, d), lambda i, j: (row(i) * 6 + kk, 0, 0))
    return pl.pallas_call(
        _ffn_kernel,
        grid=(n_rows // tm, f // tf),
        in_specs=[pl.BlockSpec((tm, d), lambda i, j: (i + t0, 0)),
                  pl.BlockSpec((1, d), const), modspec(3), modspec(4),
                  pl.BlockSpec((d, tf), lambda i, j: (0, j)),
                  pl.BlockSpec((d, tf), lambda i, j: (0, j)),
                  pl.BlockSpec((tf, d), lambda i, j: (j, 0)),
                  pl.BlockSpec((1, d), const), modspec(5)],
        out_specs=pl.BlockSpec((tm, d), lambda i, j: (i, 0)),
        out_shape=jax.ShapeDtypeStruct((n_rows, d), F32),
        scratch_shapes=[pltpu.VMEM((tm, d), BF16), pltpu.VMEM((tm, d), F32)],
        compiler_params=_cparams(2),
        name="ffn",
    )(x, nw_pre, mod, mod, wg, wu, wd, nw_post, mod)


_MIX = {}
_off = 0
for _name, _w in (("m_q", 512), ("m_k", 512), ("m_v", 512), ("m_o", 512), ("m_gates", 16),
                  ("g_q", 256), ("g_k", 256), ("g_v", 512), ("g_out", 512), ("g_lr", 32),
                  ("d_q", 1024), ("d_k", 1024), ("d_v", 1024)):
    _MIX[_name] = (_off, _w)
    _off += _w
_P_ORDER = ("m_q", "m_k", "m_v", "m_o", "g_q", "g_k", "g_v", "g_out")
_QKV_ORDER = ("d_q", "d_k", "d_v")


def _split_w_in(w_in):
    cols = lambda n: w_in[:, _MIX[n][0]:_MIX[n][0] + _MIX[n][1]]
    w_a = jnp.concatenate([cols(n) for n in _P_ORDER], axis=1).astype(BF16)
    w_b = jnp.concatenate([cols(n) for n in _QKV_ORDER], axis=1).astype(BF16)
    w_small = jnp.concatenate([cols("m_gates"), cols("g_lr"),
                               jnp.zeros((w_in.shape[0], HEAD_DIM - 48), w_in.dtype)], axis=1).astype(BF16)
    return w_a, w_b, w_small


def _rope_tables(cfg):
    rows = cfg.n_lat // GRID_W
    r = np.repeat(np.arange(rows, dtype=np.float64), GRID_W)
    c = np.tile(np.arange(GRID_W, dtype=np.float64), rows)
    half = 16
    inv_freq = ROPE_BASE ** (-np.arange(half, dtype=np.float64) / half)
    ang_r, ang_c = r[:, None] * inv_freq, c[:, None] * inv_freq
    ang = np.concatenate([ang_r, ang_r, ang_c, ang_c], axis=-1)
    ang = np.tile(ang, (cfg.B, 2))
    n_c = cfg.B * cfg.n_ctx
    cos_t = np.concatenate([np.cos(ang), np.ones((n_c, HEAD_DIM))], axis=0).astype(np.float32)
    sin_t = np.concatenate([np.sin(ang), np.zeros((n_c, HEAD_DIM))], axis=0).astype(np.float32)
    return jnp.asarray(cos_t), jnp.asarray(sin_t)


def _layer(x_parts, mod, lw, stacks, layer, lam_init, rope, cfg, need_ctx):
    d = cfg.D
    mq, mk, p, qkv, ps = _in_call(x_parts, lw["norm_mix_pre"].reshape(1, d), mod, *_split_w_in(lw["w_in"]), *rope,
                                  lw["mlstm_conv_w"], lw["mlstm_conv_b"], cfg)
    w2 = lw["gla_gate_w2"]
    w2p = jnp.zeros((HEAD_DIM, 2 * G_QK), F32)
    w2p = w2p.at[16:32, 0:G_QK].set(w2[0]).at[32:48, G_QK:].set(w2[1])
    hmf, hmb, hgf, hgb = _scan_call(mq, mk, p, ps, lw["mlstm_gate_b"], w2p,
                                    lw["gla_gate_b"].reshape(1, 2 * G_QK), cfg)

    hd_lat, wg, wu, wd = _attn_call(qkv, lw["diff_lambda"], lw["diff_subln"], lam_init, cfg, latent=True,
                                    cast=(layer, [stacks["w_ffn_gate"], stacks["w_ffn_up"], stacks["w_ffn_down"]]))
    hd_ctx = _attn_call(qkv, lw["diff_lambda"], lw["diff_subln"], lam_init, cfg, latent=False) if need_ctx else None

    n_rows = cfg.T if need_ctx else cfg.TL
    xt = _out_call(x_parts, hmf, hmb, hgf, hgb, hd_lat, hd_ctx, p, lw["mlstm_norm"].reshape(1, 512),
                   lw["gla_norm"].reshape(1, 512), stacks["w_out"], layer,
                   lw["norm_mix_post"].reshape(1, d), mod, cfg, n_rows)
    ffn = functools.partial(_ffn_call, xt, lw["norm_ffn_pre"].reshape(1, d), lw["norm_ffn_post"].reshape(1, d),
                            mod, wg, wu, wd, cfg)
    out = (ffn(0, cfg.TL, FFN_LAT_TILE),)
    if need_ctx:
        out += (ffn(cfg.TL, cfg.T - cfg.TL, FFN_CTX_TILE),)
    return out


_LAYER_KEYS = ("norm_mix_pre", "norm_mix_post", "norm_ffn_pre", "norm_ffn_post", "w_in", "mlstm_conv_w",
               "mlstm_conv_b", "mlstm_gate_b", "mlstm_norm", "gla_gate_w2", "gla_gate_b", "gla_norm",
               "diff_lambda", "diff_subln", "w_out", "w_ffn_gate", "w_ffn_up", "w_ffn_down")
_STACK_KEYS = ("w_out", "w_ffn_gate", "w_ffn_up", "w_ffn_down")


def kernel(x, c, ctx, c_ctx, w_mod, b_mod, norm_mix_pre, norm_mix_post, norm_ffn_pre, norm_ffn_post, w_in, mlstm_conv_w, mlstm_conv_b, mlstm_gate_b, mlstm_norm, gla_gate_w2, gla_gate_b, gla_norm, diff_lambda, diff_subln, w_out, w_ffn_gate, w_ffn_up, w_ffn_down):
    weights = dict(zip(_LAYER_KEYS, (norm_mix_pre, norm_mix_post, norm_ffn_pre, norm_ffn_post, w_in,
                                     mlstm_conv_w, mlstm_conv_b, mlstm_gate_b, mlstm_norm, gla_gate_w2,
                                     gla_gate_b, gla_norm, diff_lambda, diff_subln, w_out, w_ffn_gate,
                                     w_ffn_up, w_ffn_down)))
    b, n_lat, d = x.shape
    cfg = Cfg(B=b, n_ctx=ctx.shape[1], n_lat=n_lat, D=d, F=w_ffn_gate.shape[-1])
    depth = w_mod.shape[0]
    c8 = jnp.zeros((8, d), F32).at[0].set(c_ctx).at[1:1 + b].set(c)
    mods = _mod_call(c8, w_mod, b_mod).reshape(depth, 8 * 6, 1, d)
    rope = _rope_tables(cfg)
    x_parts = (x.reshape(b * n_lat, d), ctx.reshape(b * ctx.shape[1], d))
    stacks = {k: weights.pop(k) for k in _STACK_KEYS}
    stacks["w_out"] = stacks["w_out"].astype(BF16)
    for layer in range(depth):
        lw = {k: v[layer] for k, v in weights.items()}
        lam_init = 0.8 - 0.6 * math.exp(-0.3 * layer)
        x_parts = _layer(x_parts, mods[layer], lw, stacks, layer, lam_init, rope, cfg, need_ctx=layer < depth - 1)
    return x_parts[0].reshape(b, n_lat, d)
```

```python
import dataclasses
import functools
import math

import jax
import jax.numpy as jnp
import numpy as np
from jax import lax
from jax.experimental import pallas as pl
from jax.experimental.pallas import tpu as pltpu

F32 = jnp.float32
BF16 = jnp.bfloat16
NORM_EPS = 1e-6
CHUNK = 64
GRID_W = 64
ROPE_BASE = 10000.0
GLA_TAU = 16.0
HEAD_DIM = 128
VMEM_LIMIT_BYTES = 56 * 1024 * 1024
HIGHEST = lax.Precision.HIGHEST


@dataclasses.dataclass(frozen=True)
class Cfg:
    B: int = 2
    n_ctx: int = 256
    n_lat: int = 4096
    D: int = 2048
    F: int = 5632

    @property
    def T(self):
        return self.B * (self.n_ctx + self.n_lat)

    @property
    def TL(self):
        return self.B * self.n_lat


def _cparams(n_axes):
    return pltpu.CompilerParams(dimension_semantics=("arbitrary",) * n_axes,
                                vmem_limit_bytes=VMEM_LIMIT_BYTES)


def _mod_row(i, cfg, tm):
    lt = cfg.n_lat // tm
    return jnp.where(i < cfg.B * lt, 1 + i // lt, 0)


def _chunk_block(b, c, cfg):
    ncc, ncl = cfg.n_ctx // CHUNK, cfg.n_lat // CHUNK
    return jnp.where(c < ncc, cfg.B * ncl + b * ncc + c, b * ncl + (c - ncc))


def _bwd_chunk(s, cfg):
    ncc, ncl = cfg.n_ctx // CHUNK, cfg.n_lat // CHUNK
    return jnp.where(s < ncc, ncc - 1 - s, ncc + ncl - 1 - (s - ncc))


def _log_sigmoid(x):
    return jnp.minimum(x, 0.0) - jnp.log1p(jnp.exp(-jnp.abs(x)))


def _silu(x):
    return x * jax.nn.sigmoid(x)


MOD_K_TILE = 256


def _mod_kernel(c_ref, w0_ref, w1_ref, b_ref, o_ref):
    k = pl.program_id(1)

    @pl.when(k == 0)
    def _():
        o_ref[0] = jnp.broadcast_to(b_ref[0], o_ref.shape[1:])

    s = _silu(c_ref[...]).astype(BF16)
    half = MOD_K_TILE // 2
    o_ref[0] += (jnp.dot(s[:, 0:half], w0_ref[0].astype(BF16), preferred_element_type=F32)
                 + jnp.dot(s[:, half:], w1_ref[0].astype(BF16), preferred_element_type=F32))


def _mod_call(c8, w_mod, b_mod):
    depth, d, n = w_mod.shape
    tk = MOD_K_TILE
    return pl.pallas_call(
        _mod_kernel,
        grid=(depth, d // tk),
        in_specs=[pl.BlockSpec((8, tk), lambda l, k: (0, k)),
                  pl.BlockSpec((1, tk // 2, n), lambda l, k: (l, 2 * k, 0)),
                  pl.BlockSpec((1, tk // 2, n), lambda l, k: (l, 2 * k + 1, 0)),
                  pl.BlockSpec((1, 1, n), lambda l, k: (l, 0, 0))],
        out_specs=pl.BlockSpec((1, 8, n), lambda l, k: (l, 0, 0)),
        out_shape=jax.ShapeDtypeStruct((depth, 8, n), F32),
        compiler_params=_cparams(2),
        name="adaln_mod",
    )(c8, w_mod, w_mod, b_mod.reshape(depth, 1, n))


def _prenorm(x, nw, shift, scale):
    ms = jnp.mean(x * x, axis=-1, keepdims=True)
    return (x * lax.rsqrt(ms + NORM_EPS) * nw) * (1.0 + scale) + shift


D_HEADS = 8
P_COLS = 3584
M_CONV_COLS = 1024
P_BLK_M_V, P_BLK_M_O, P_BLK_G_QK, P_BLK_G_V, P_BLK_G_OUT = range(5)
QKV_COLS = 3072
IN_TILE = 256


def _token_specs(x_parts, tm, cfg):
    d = cfg.D
    if len(x_parts) == 1:
        return [pl.BlockSpec((tm, d), lambda i: (i, 0))]
    nl = cfg.TL // tm
    return [pl.BlockSpec((tm, d), lambda i: (jnp.minimum(i, nl - 1), 0)),
            pl.BlockSpec((tm, d), lambda i: (jnp.maximum(i - nl, 0), 0))]


def _token_tile(x_refs, n_lat_tiles):
    if len(x_refs) == 1:
        return x_refs[0][...]
    return jnp.where(pl.program_id(0) < n_lat_tiles, x_refs[0][...], x_refs[1][...])


def _halo_specs(x_parts, tm, cfg):
    d, r8 = cfg.D, tm // 8
    nl = cfg.TL // tm
    specs = []
    for part, tile0 in zip(x_parts, (0, nl)):
        last8 = part.shape[0] // 8 - 1
        clamp = lambda v, last8=last8: jnp.clip(v, 0, last8)
        specs.append(pl.BlockSpec((8, d), lambda i, t0=tile0, c=clamp: (c((i - t0) * r8 - 1), 0)))
        specs.append(pl.BlockSpec((8, d), lambda i, t0=tile0, c=clamp: (c((i - t0 + 1) * r8), 0)))
    return specs


def _in_kernel(*refs, n_src, n_lat_tiles, seg_first, seg_last):
    x_refs, halo_refs = refs[:n_src], refs[n_src:3 * n_src]
    (nw_ref, shift_ref, scale_ref, wa_ref, wb_ref, ws_ref, cos_ref, sin_ref, cw_ref, cb_ref,
     mq_ref, mk_ref, p_ref, qkv_ref, ps_ref) = refs[3 * n_src:]
    i = pl.program_id(0)
    norm = lambda x: _prenorm(x, nw_ref[...], shift_ref[0], scale_ref[0]).astype(BF16)
    h = norm(_token_tile(x_refs, n_lat_tiles))
    ps_ref[...] = jnp.dot(h, ws_ref[...], preferred_element_type=F32)
    p_ref[...] = jnp.dot(h, wa_ref[:, M_CONV_COLS:], preferred_element_type=F32)

    halo = jnp.concatenate([_token_tile(halo_refs[0::2], n_lat_tiles), _token_tile(halo_refs[1::2], n_lat_tiles)],
                           axis=0)
    pm = jnp.dot(jnp.concatenate([h, norm(halo)], axis=0), wa_ref[:, 0:M_CONV_COLS], preferred_element_type=F32)
    is_first = functools.reduce(jnp.logical_or, [i == s for s in seg_first])
    is_last = functools.reduce(jnp.logical_or, [i == s for s in seg_last])
    prev_row = jnp.where(is_first, 0.0, pm[IN_TILE + 7:IN_TILE + 8])
    next_row = jnp.where(is_last, 0.0, pm[IN_TILE + 8:IN_TILE + 9])
    xm = pm[0:IN_TILE]
    rows = lax.broadcasted_iota(jnp.int32, xm.shape, 0)
    x_prev = jnp.where(rows == 0, prev_row, pltpu.roll(xm, 1, 0))
    x_next = jnp.where(rows == IN_TILE - 1, next_row, pltpu.roll(xm, IN_TILE - 1, 0))
    conv = _silu(x_prev * cw_ref[0:1, :] + xm * cw_ref[1:2, :] + x_next * cw_ref[2:3, :] + cb_ref[...])
    mq_ref[...] = conv[:, 0:M_CONV_COLS // 2] * (HEAD_DIM ** -0.5)
    mk_ref[...] = conv[:, M_CONV_COLS // 2:]

    qkv = jnp.dot(h, wb_ref[...], preferred_element_type=F32)
    lane = lax.broadcasted_iota(jnp.int32, (IN_TILE, HEAD_DIM), 1)
    low = (lane % 32) < 16
    cos, sin = cos_ref[...], sin_ref[...]
    q_scale = 64 ** -0.5 * math.log2(math.e)
    for s in range(2 * D_HEADS):
        cs = slice(s * HEAD_DIM, (s + 1) * HEAD_DIM)
        x = qkv[:, cs]
        rot = jnp.where(low, -pltpu.roll(x, HEAD_DIM - 16, 1), pltpu.roll(x, 16, 1))
        y = x * cos + rot * sin
        qkv_ref[:, cs] = ((y * q_scale) if s < D_HEADS else y).astype(BF16)
    vs = slice(2 * D_HEADS * HEAD_DIM, QKV_COLS)
    qkv_ref[:, vs] = qkv[:, vs].astype(BF16)


def _in_call(x_parts, nw, mod, w_a, w_b, w_small, cos_t, sin_t, conv_w, conv_b, cfg):
    t, d = cfg.T, cfg.D
    tm = IN_TILE
    row = functools.partial(_mod_row, cfg=cfg, tm=tm)
    resident = lambda shape: pl.BlockSpec(shape, lambda i: (0, 0), pipeline_mode=pl.Buffered(1))
    lt, ct = cfg.n_lat // tm, cfg.n_ctx // tm
    seg_first = [b * lt for b in range(cfg.B)] + [cfg.B * lt + b * ct for b in range(cfg.B)]
    seg_last = [b * lt + lt - 1 for b in range(cfg.B)] + [cfg.B * lt + b * ct + ct - 1 for b in range(cfg.B)]
    half = M_CONV_COLS // 2
    tile = lambda w: pl.BlockSpec((tm, w), lambda i: (i, 0))
    return pl.pallas_call(
        functools.partial(_in_kernel, n_src=len(x_parts), n_lat_tiles=cfg.TL // tm,
                          seg_first=tuple(seg_first), seg_last=tuple(seg_last)),
        grid=(t // tm,),
        in_specs=_token_specs(x_parts, tm, cfg) + _halo_specs(x_parts, tm, cfg) + [
                  resident((1, d)),
                  pl.BlockSpec((1, 1, d), lambda i: (row(i) * 6 + 0, 0, 0)),
                  pl.BlockSpec((1, 1, d), lambda i: (row(i) * 6 + 1, 0, 0)),
                  resident((d, P_COLS)), resident((d, QKV_COLS)), resident((d, HEAD_DIM)),
                  tile(HEAD_DIM), tile(HEAD_DIM),
                  resident((3, M_CONV_COLS)), resident((1, M_CONV_COLS))],
        out_specs=[tile(half), tile(half), tile(P_COLS - M_CONV_COLS), tile(QKV_COLS), tile(HEAD_DIM)],
        out_shape=[jax.ShapeDtypeStruct((t, half), F32), jax.ShapeDtypeStruct((t, half), F32),
                   jax.ShapeDtypeStruct((t, P_COLS - M_CONV_COLS), F32),
                   jax.ShapeDtypeStruct((t, QKV_COLS), BF16), jax.ShapeDtypeStruct((t, HEAD_DIM), F32)],
        compiler_params=_cparams(1),
        name="in_proj",
    )(*x_parts, *[part for part in x_parts for _ in range(2)], nw, mod, mod, w_a, w_b, w_small, cos_t, sin_t,
      conv_w, conv_b.reshape(1, M_CONV_COLS))


M_HEADS = 4


def _split3(x):
    hi = x.astype(BF16)
    r1 = x - hi.astype(F32)
    mid = r1.astype(BF16)
    return hi, mid, (r1 - mid.astype(F32)).astype(BF16)


def _cumsum_matmul(mask, parts):
    return sum(jnp.dot(mask, part, preferred_element_type=F32) for part in parts)


def _mlstm_body(qf, kf, vf, gf, qb, kb, vb, gb, brow, bcol, of, ob, c_scr, m_scr):
    L = CHUNK
    row = lax.broadcasted_iota(jnp.int32, (L, L), 0)
    col = lax.broadcasted_iota(jnp.int32, (L, L), 1)
    neg_inf = jnp.float32(-jnp.inf)

    gcol = jnp.concatenate([gf[...], gb[...]], axis=0) + brow[...]
    grow = jnp.concatenate([gf[...], gb[...]], axis=0).T + bcol[...]
    r8 = lax.broadcasted_iota(jnp.int32, (8, 2 * L), 0)
    l8 = lax.broadcasted_iota(jnp.int32, (8, 2 * L), 1)
    fwd_row = r8 < M_HEADS
    own = (r8 // M_HEADS) == (l8 // L)
    ig = jnp.concatenate([grow[0:4], grow[8:12]], axis=0)
    logf = _log_sigmoid(jnp.concatenate([grow[4:8], grow[12:16]], axis=0))
    tl = lax.broadcasted_iota(jnp.int32, (2 * L, 2 * L), 0)
    ti = lax.broadcasted_iota(jnp.int32, (2 * L, 2 * L), 1)
    same_half = (tl // L) == (ti // L)
    scan_rows = jnp.logical_and(same_half, jnp.where(ti < L, tl - ti, ti - tl) <= 0)
    scan_cols = jnp.logical_and(same_half, jnp.where(tl < L, ti - tl, tl - ti) <= 0)
    cf = sum(jnp.dot(part, scan_rows.astype(BF16), preferred_element_type=F32) for part in _split3(logf))
    u = jnp.where(own, ig - cf, neg_inf)
    m_st = m_scr[...]
    end_lane = jnp.where(fwd_row, L - 1, L)
    f_end = jnp.broadcast_to(jnp.sum(jnp.where(l8 == end_lane, cf, 0.0), axis=-1, keepdims=True),
                             (8, 2 * L))
    dec = jnp.where(own, f_end - cf + ig, neg_inf)
    m_new = jnp.maximum(f_end + m_st, jnp.max(dec, axis=-1, keepdims=True))
    a_prev = jnp.exp(f_end + m_st - m_new)
    m_scr[...] = m_new
    u_keys = jnp.where(fwd_row, u, pltpu.roll(u, L, 1))
    cum_col = _cumsum_matmul(scan_cols.astype(BF16), _split3(_log_sigmoid(gcol)))

    rep = lambda colv: jnp.broadcast_to(colv, (L, HEAD_DIM))
    dir_refs = ((qf, kf, vf, of), (qb, kb, vb, ob))
    pairs = [(d, h) for d in range(2) for h in range(M_HEADS)]
    work = []
    for d, h in pairs:
        q_ref, k_ref, v_ref, _ = dir_refs[d]
        hs = slice(h * HEAD_DIM, (h + 1) * HEAD_DIM)
        q = q_ref[:, hs].astype(BF16)
        k32 = k_ref[:, hs]
        v1 = jnp.concatenate([v_ref[:, hs].astype(BF16), jnp.ones((L, HEAD_DIM), BF16)], axis=1)
        cn = c_scr[d, h]
        qk = lax.dot_general(q, k32.astype(BF16), (((1,), (1,)), ((), ())), preferred_element_type=F32)
        r_state = jnp.dot(q, cn.astype(BF16), preferred_element_type=F32)
        work.append((k32, v1, cn, qk, r_state))
    yield
    for (d, h), (k32, v1, cn, qk, r_state) in zip(pairs, work):
        o_ref = dir_refs[d][3]
        valid = (col <= row) if d == 0 else (col >= row)
        ts = slice(d * L, (d + 1) * L)
        r = d * M_HEADS + h
        ci, cfc = d * 8 + h, d * 8 + 4 + h
        hs = slice(h * HEAD_DIM, (h + 1) * HEAD_DIM)
        u_tile = jnp.where(valid, u_keys[r:r + 1, 0:L], neg_inf)
        m_rep = rep(jnp.maximum(jnp.max(u_tile, axis=-1, keepdims=True), m_st[r:r + 1, 0:1]))
        cf_rep = rep(cum_col[ts, cfc:cfc + 1])
        ig_rep = rep(gcol[ts, ci:ci + 1])
        w_inter = jnp.exp(m_st[r:r + 1] - m_rep)
        e = jnp.exp(u_tile - m_rep[:, 0:L])
        r_chunk = jnp.dot((qk * e).astype(BF16), v1, preferred_element_type=F32)
        num = w_inter * r_state[:, 0:HEAD_DIM] + r_chunk[:, 0:HEAD_DIM]
        den = w_inter * r_state[:, HEAD_DIM:] + r_chunk[:, HEAD_DIM:]
        o_ref[:, hs] = num / jnp.maximum(jnp.abs(den), jnp.exp(-(cf_rep + m_rep)))

        ws = jnp.exp(f_end[r:r + 1] - cf_rep + ig_rep - m_new[r:r + 1])
        wk = (ws * k32).astype(BF16)
        decay = jnp.concatenate([a_prev[r:r + 1], a_prev[r:r + 1]], axis=1)
        c_scr[d, h] = decay * cn + lax.dot_general(wk, v1, (((0,), (0,)), ((), ())),
                                                   preferred_element_type=F32)


N_MLSTM_REFS = 8
N_GLA_REFS = 6


def _scan_kernel(*refs, n_b):
    it = iter(refs)
    take = lambda n: [next(it) for _ in range(n)]
    m_in = [take(N_MLSTM_REFS) for _ in range(n_b)]
    brow, bcol = take(2)
    g_in = [take(N_GLA_REFS) for _ in range(n_b)]
    w2_ref, b2_ref = take(2)
    m_of, m_ob, g_of, g_ob = take(4)
    c_scr, m_scr, s_scr, gl_scr = take(4)

    @pl.when(pl.program_id(0) == 0)
    def _():
        c_scr[...] = jnp.zeros_like(c_scr)
        m_scr[...] = jnp.zeros_like(m_scr)
        s_scr[...] = jnp.zeros_like(s_scr)

    bodies = []
    for b in range(n_b):
        bodies.append(_mlstm_body(*m_in[b], brow, bcol, m_of.at[0, b], m_ob.at[0, b], c_scr.at[b], m_scr.at[b]))
        bodies.append(_gla_body(*g_in[b], w2_ref, b2_ref, g_of.at[0, b], g_ob.at[0, b], s_scr.at[b], gl_scr.at[b]))
    while bodies:
        bodies = [body for body in bodies if next(body, "done") != "done"]


def _scan_call(mq, mk, p, ps, gate_b, w2p, b2p, cfg):
    nc = (cfg.n_ctx + cfg.n_lat) // CHUNK
    fwd = lambda b, col: (lambda s: (_chunk_block(b, s, cfg), col))
    bwd = lambda b, col: (lambda s: (_chunk_block(b, _bwd_chunk(s, cfg), cfg), col))
    blk = lambda w, f: pl.BlockSpec((CHUNK, w), f)
    brow = jnp.zeros((1, HEAD_DIM), F32).at[0, :16].set(gate_b)
    const = lambda s: (0, 0)
    in_specs, args = [], []
    for b in range(cfg.B):
        for way in (fwd, bwd):
            in_specs += [blk(512, way(b, 0)), blk(512, way(b, 0)), blk(512, way(b, P_BLK_M_V)),
                         blk(HEAD_DIM, way(b, 0))]
            args += [mq, mk, p, ps]
    in_specs += [pl.BlockSpec((1, HEAD_DIM), const), pl.BlockSpec((HEAD_DIM, 1), const)]
    args += [brow, brow.reshape(HEAD_DIM, 1)]
    for b in range(cfg.B):
        for way in (fwd, bwd):
            in_specs += [blk(512, way(b, P_BLK_G_QK)), blk(512, way(b, P_BLK_G_V)), blk(HEAD_DIM, way(b, 0))]
            args += [p, p, ps]
    in_specs += [pl.BlockSpec((HEAD_DIM, 2 * G_QK), const), pl.BlockSpec((1, 2 * G_QK), const)]
    args += [w2p, b2p]
    out_f = pl.BlockSpec((1, cfg.B, CHUNK, 512), lambda s: (s, 0, 0, 0))
    out_b = pl.BlockSpec((1, cfg.B, CHUNK, 512), lambda s: (_bwd_chunk(s, cfg), 0, 0, 0))
    outs = pl.pallas_call(
        functools.partial(_scan_kernel, n_b=cfg.B),
        grid=(nc,),
        in_specs=in_specs,
        out_specs=[out_f, out_b, out_f, out_b],
        out_shape=[jax.ShapeDtypeStruct((nc, cfg.B, CHUNK, 512), F32)] * 4,
        scratch_shapes=[pltpu.VMEM((cfg.B, 2, M_HEADS, HEAD_DIM, 2 * HEAD_DIM), F32),
                        pltpu.VMEM((cfg.B, 2 * M_HEADS, 2 * CHUNK), F32),
                        pltpu.VMEM((cfg.B, 2, HEAD_DIM, G_QK), F32),
                        pltpu.VMEM((cfg.B, 2, CHUNK, G_QK), F32)],
        compiler_params=_cparams(1),
        name="recurrent_scan",
    )(*args)
    return outs


G_QK = 256
G_V = 512


G_HEADS = 4
G_DK = 64
GLA_BLK = 16


def _gla_body(qkf, vf, lf, qkb, vb, lb, w2_ref, b2_ref, of, ob, s_scr, gl_scr):
    L, C = CHUNK, GLA_BLK
    nb = L // C
    row = lax.broadcasted_iota(jnp.int32, (L, L), 0)
    col = lax.broadcasted_iota(jnp.int32, (L, L), 1)
    same_blk = (row // C) == (col // C)
    rows8 = lax.broadcasted_iota(jnp.int32, (8, G_QK), 0)
    neg_inf = jnp.float32(-jnp.inf)
    he_r = lax.broadcasted_iota(jnp.int32, (G_QK, G_V), 0) // G_DK
    he_c = lax.broadcasted_iota(jnp.int32, (G_QK, G_V), 1) // HEAD_DIM
    head_expand = (he_r == he_c).astype(BF16)

    def per_head_rows(x):
        lane_head = lax.broadcasted_iota(jnp.int32, x.shape, 1) // G_DK
        return jnp.concatenate([jnp.where(lane_head == h, x, 0.0) for h in range(G_HEADS)], axis=0).astype(BF16)

    def head_blocks(r, n):
        return jnp.concatenate([r[h * n:(h + 1) * n, h * HEAD_DIM:(h + 1) * HEAD_DIM] for h in range(G_HEADS)],
                               axis=1)

    def direction(d, qk_ref, v_ref, l_ref, o_ref):
        cs = slice(d * G_QK, (d + 1) * G_QK)
        z = jnp.dot(l_ref[...], w2_ref[:, cs], precision=HIGHEST, preferred_element_type=F32) + b2_ref[:, cs]
        log_a = _log_sigmoid(z) * (1.0 / GLA_TAU)
        tri = (col <= row) if d == 0 else (col >= row)
        log_a_parts = _split3(log_a)
        g = _cumsum_matmul(tri.astype(BF16), log_a_parts)
        gl = _cumsum_matmul(jnp.logical_and(tri, same_blk).astype(BF16), log_a_parts)
        gl_scr[d] = gl
        q = qk_ref[:, 0:G_QK] * (G_DK ** -0.5)
        k = qk_ref[:, G_QK:2 * G_QK]
        v = v_ref[...]
        vb16 = v.astype(BF16)
        g_end = g[L - 1:L, :] if d == 0 else g[0:1, :]
        st = s_scr[d]

        r = lax.dot_general(per_head_rows(q * jnp.exp(g)), st.astype(BF16), (((1,), (1,)), ((), ())),
                            preferred_element_type=F32)
        inter = jnp.concatenate([r[h * L:(h + 1) * L] for h in range(G_HEADS)], axis=1)
        yield

        ql = q * jnp.exp(gl)
        first = lambda b: b * C + (C - 1 if d == 0 else 0)
        tot = [gl[first(b):first(b) + 1] for b in range(nb)]
        k_end = [k[b * C:(b + 1) * C] * jnp.exp(tot[b] - gl[b * C:(b + 1) * C]) for b in range(nb)]

        atts, v_cats = {}, {}
        for bi in range(nb):
            earlier = list(range(bi)) if d == 0 else list(range(bi + 1, nb))
            if not earlier:
                continue
            ks, vs = [], []
            for bj in earlier:
                between = range(bj + 1, bi) if d == 0 else range(bi + 1, bj)
                kj = k_end[bj]
                if len(between):
                    kj = kj * jnp.exp(functools.reduce(jnp.add, [tot[m] for m in between]))
                ks.append(kj)
                vs.append(vb16[bj * C:(bj + 1) * C])
            v_cats[bi] = jnp.concatenate(vs, axis=0)
            atts[bi] = lax.dot_general(per_head_rows(ql[bi * C:(bi + 1) * C]),
                                       jnp.concatenate(ks, axis=0).astype(BF16), (((1,), (1,)), ((), ())),
                                       preferred_element_type=F32)

        k_hat = per_head_rows(k * jnp.exp(g_end - g))
        v_rows = jnp.concatenate([vb16[:, h * HEAD_DIM:(h + 1) * HEAD_DIM] for h in range(G_HEADS)], axis=0)
        s_new = st * jnp.exp(g_end) + lax.dot_general(v_rows, k_hat, (((0,), (0,)), ((), ())),
                                                      preferred_element_type=F32)
        yield

        pieces, spans = [], []
        for i in range(L):
            bi, il = divmod(i, C)
            lo, hi = ((il // 8) * 8, C) if d == 0 else (0, (il // 8) * 8 + 8)
            edge = (lo, lo + 8) if d == 0 else (hi - 8, hi)
            k_i = qk_ref[i:i + 1, G_QK:2 * G_QK]
            g_i = gl_scr[d, i:i + 1, :]
            vis = (rows8 >= il % 8) if d == 0 else (rows8 <= il % 8)
            diff = jnp.where(vis, gl[bi * C + edge[0]:bi * C + edge[1]] - g_i, neg_inf)
            if hi - lo > 8:
                other = gl[bi * C + 8:bi * C + 16] if d == 0 else gl[bi * C:bi * C + 8]
                diff = jnp.concatenate([diff, other - g_i] if d == 0 else [other - g_i, diff], axis=0)
            pieces.append((q[bi * C + lo:bi * C + hi] * k_i) * jnp.exp(diff))
            spans.append((lo, hi))
        w_all = jnp.dot(jnp.concatenate(pieces, axis=0).astype(BF16), head_expand, preferred_element_type=F32)
        yield
        cross = {bi: head_blocks(jnp.dot(att.astype(BF16), v_cats[bi], preferred_element_type=F32), C)
                 for bi, att in atts.items()}
        yield

        blocks = []
        w_off = 0
        for bi in range(nb):
            acc = inter[bi * C:(bi + 1) * C]
            if bi in cross:
                acc = acc + cross[bi]
            for il in range(C):
                i = bi * C + il
                lo, hi = spans[i]
                upd = acc[lo:hi] + w_all[w_off:w_off + hi - lo] * v_ref[i:i + 1, :]
                w_off += hi - lo
                parts = ([acc[:lo]] if lo > 0 else []) + [upd] + ([acc[hi:]] if hi < C else [])
                acc = jnp.concatenate(parts, axis=0) if len(parts) > 1 else upd
            blocks.append(acc)
        o_ref[...] = jnp.concatenate(blocks, axis=0)
        s_scr[d] = s_new

    stages = [direction(0, qkf, vf, lf, of), direction(1, qkb, vb, lb, ob)]
    for _ in range(5):
        for stage in stages:
            next(stage, None)
        yield


ATT_SUB_ROWS = 256
ATT_TQ = 1024
ATT_KEY_CHUNK = 1024


def _attn_kernel(*refs, has_lat, lam_init, n_cast, regroup):
    if has_lat:
        q_ref, kc_ref, vc_ref, kl_ref, vl_ref, dl_ref, sub_ref = refs[:7]
        kv = ((kc_ref, vc_ref), (kl_ref, vl_ref))
        refs = refs[7:]
    else:
        q_ref, kc_ref, vc_ref, dl_ref, sub_ref = refs[:5]
        kv = ((kc_ref, vc_ref),)
        refs = refs[5:]
    cast_in, refs = refs[:n_cast], refs[n_cast:]
    if regroup:
        rg_in, refs = refs[0], refs[1:]
    o_ref, refs = refs[0], refs[1:]
    cast_out, refs = refs[:n_cast], refs[n_cast:]
    if regroup:
        (wa_out, wb_out, ws_out), refs = refs[:3], refs[3:]
    s_scr, vo_scr = refs
    for src, dst in zip(cast_in, cast_out):
        dst[...] = src[...].astype(BF16)
    if regroup:
        w = rg_in[...]
        col = lambda n: w[:, _MIX[n][0]:_MIX[n][0] + _MIX[n][1]]
        wa_out[...] = jnp.concatenate([col(n) for n in _P_ORDER], axis=1).astype(BF16)
        wb_out[...] = jnp.concatenate([col(n) for n in _QKV_ORDER], axis=1).astype(BF16)
        pad = jnp.zeros((w.shape[0], HEAD_DIM - 48), F32)
        ws_out[...] = jnp.concatenate([col("m_gates"), col("g_lr"), pad], axis=1).astype(BF16)

    @pl.when(pl.program_id(2) == 0)
    def _():
        off = 0
        for _, v_ref in kv:
            n = v_ref.shape[0]
            vo_scr[off:off + n, 0:HEAD_DIM] = v_ref[...]
            vo_scr[off:off + n, HEAD_DIM:2 * HEAD_DIM] = jnp.ones((n, HEAD_DIM), BF16)
            off += n

    n_sub = q_ref.shape[0] // ATT_SUB_ROWS
    lane = lax.broadcasted_iota(jnp.int32, (ATT_SUB_ROWS, HEAD_DIM), 1)
    zero = jnp.zeros((ATT_SUB_ROWS, HEAD_DIM), BF16)
    dl = dl_ref[...]
    lam = (jnp.exp(jnp.sum(dl[0:1] * dl[1:2], axis=-1, keepdims=True))
           - jnp.exp(jnp.sum(dl[2:3] * dl[3:4], axis=-1, keepdims=True)) + lam_init)
    key_chunks, off = [], 0
    for k_ref, _ in kv:
        for c0 in range(0, k_ref.shape[0], ATT_KEY_CHUNK):
            n = min(ATT_KEY_CHUNK, k_ref.shape[0] - c0)
            key_chunks.append((k_ref, c0, off + c0, n))
        off += k_ref.shape[0]

    prev = None
    for sb in range(n_sub + 1):
        if sb < n_sub:
            q = q_ref[sb * ATT_SUB_ROWS:(sb + 1) * ATT_SUB_ROWS, :]
            qms = (jnp.where(lane < 64, q, zero), jnp.where(lane >= 64, q, zero))
            part_max = [None, None]
        accs = [None, None]
        for k_ref, c0, off, n in key_chunks:
            for m in range(2):
                if sb < n_sub:
                    s = lax.dot_general(qms[m], k_ref[c0:c0 + n, :], (((1,), (1,)), ((), ())),
                                        preferred_element_type=F32)
                    s_scr[2 * sb + m, :, off:off + n] = s
                    folded = functools.reduce(jnp.maximum, [s[:, j:j + HEAD_DIM] for j in range(0, n, HEAD_DIM)])
                    part_max[m] = folded if part_max[m] is None else jnp.maximum(part_max[m], folded)
                if prev is not None:
                    p = jnp.exp2(s_scr[2 * (sb - 1) + m, :, off:off + n] - prev[m]).astype(BF16)
                    part = jnp.dot(p, vo_scr[off:off + n, :], preferred_element_type=F32)
                    accs[m] = part if accs[m] is None else accs[m] + part
        if prev is not None:
            outs = [a[:, 0:HEAD_DIM] / a[:, HEAD_DIM:HEAD_DIM + 1] for a in accs]
            out = outs[0] - lam * outs[1]
            ms = jnp.mean(out * out, axis=-1, keepdims=True)
            o_ref[(sb - 1) * ATT_SUB_ROWS:sb * ATT_SUB_ROWS, :] = (
                (out * lax.rsqrt(ms + NORM_EPS) * sub_ref[...]) * (1.0 - lam_init))
        prev = [jnp.max(pm, axis=-1, keepdims=True) for pm in part_max] if sb < n_sub else None


def _attn_call(qkv, d_lam, d_subln, lam_init, cfg, latent, cast=None, regroup=None):
    nlb = cfg.n_lat // ATT_TQ
    ctx_row0 = cfg.B * cfg.n_lat // cfg.n_ctx
    layer, cast_srcs = cast if cast is not None else (0, [])
    kern = functools.partial(_attn_kernel, has_lat=latent, lam_init=lam_init, n_cast=len(cast_srcs),
                             regroup=regroup is not None)
    n_keys = cfg.n_ctx + (cfg.n_lat if latent else 0)
    kc = pl.BlockSpec((cfg.n_ctx, HEAD_DIM), lambda b, h, i: (ctx_row0 + b, D_HEADS + h))
    vc = pl.BlockSpec((cfg.n_ctx, HEAD_DIM), lambda b, h, i: (ctx_row0 + b, 2 * D_HEADS + h))
    small = [pl.BlockSpec((4, 64), lambda b, h, i: (0, 0)), pl.BlockSpec((1, HEAD_DIM), lambda b, h, i: (0, 0))]
    if latent:
        tq = ATT_TQ
        grid = (cfg.B, D_HEADS, nlb)
        q_spec = pl.BlockSpec((tq, HEAD_DIM), lambda b, h, i: (b * nlb + i, h))
        kv = [kc, vc,
              pl.BlockSpec((cfg.n_lat, HEAD_DIM), lambda b, h, i: (b, D_HEADS + h)),
              pl.BlockSpec((cfg.n_lat, HEAD_DIM), lambda b, h, i: (b, 2 * D_HEADS + h))]
        out_spec = pl.BlockSpec((tq, HEAD_DIM), lambda b, h, i: (b * nlb + i, h))
        n_rows = cfg.TL
    else:
        tq = cfg.n_ctx
        grid = (cfg.B, D_HEADS, 1)
        q_spec = pl.BlockSpec((tq, HEAD_DIM), lambda b, h, i: (ctx_row0 + b, h))
        kv = [kc, vc]
        out_spec = pl.BlockSpec((tq, HEAD_DIM), lambda b, h, i: (b, h))
        n_rows = cfg.B * cfg.n_ctx
    args = [qkv] * (1 + len(kv)) + [d_lam, d_subln.reshape(1, HEAD_DIM)]
    n_steps = grid[0] * grid[1] * grid[2]
    step = lambda b, h, i: (b * grid[1] + h) * grid[2] + i
    cast_in, cast_out, cast_shapes = [], [], []
    for w in cast_srcs:
        _, r, c = w.shape
        per = 1
        while (r * per) % n_steps or (r * per // n_steps) % 16:
            per *= 2
        slab = r * per // n_steps
        cast_in.append(pl.BlockSpec((None, slab, c), lambda b, h, i, per=per: (layer, step(b, h, i) // per, 0)))
        cast_out.append(pl.BlockSpec((slab, c), lambda b, h, i, per=per: (step(b, h, i) // per, 0)))
        cast_shapes.append(jax.ShapeDtypeStruct((r, c), BF16))
    if regroup is not None:
        rg_layer, w_in = regroup
        _, r, c = w_in.shape
        slab = r // n_steps
        assert r % n_steps == 0 and slab % 16 == 0, (r, n_steps)
        cast_in.append(pl.BlockSpec((None, slab, c), lambda b, h, i: (rg_layer, step(b, h, i), 0)))
        for width in (P_COLS, QKV_COLS, HEAD_DIM):
            cast_out.append(pl.BlockSpec((slab, width), lambda b, h, i: (step(b, h, i), 0)))
            cast_shapes.append(jax.ShapeDtypeStruct((r, width), BF16))
        cast_srcs = list(cast_srcs) + [w_in]
    outs = pl.pallas_call(
        kern,
        grid=grid,
        in_specs=[q_spec] + kv + small + cast_in,
        out_specs=[out_spec] + cast_out,
        out_shape=[jax.ShapeDtypeStruct((n_rows, D_HEADS * HEAD_DIM), F32)] + cast_shapes,
        scratch_shapes=[pltpu.VMEM((2 * tq // ATT_SUB_ROWS, ATT_SUB_ROWS, n_keys), F32),
                        pltpu.VMEM((n_keys, 2 * HEAD_DIM), BF16)],
        compiler_params=_cparams(3),
        name="diff_attn_lat" if latent else "diff_attn_ctx",
    )(*args, *cast_srcs)
    return outs if cast_srcs else outs[0]


OUT_TILE = 256


def _group_rmsnorm(x, w, groups):
    parts = []
    for gi in range(groups):
        xs = x[:, gi * HEAD_DIM:(gi + 1) * HEAD_DIM]
        ms = jnp.mean(xs * xs, axis=-1, keepdims=True)
        parts.append(xs * lax.rsqrt(ms + NORM_EPS) * w[:, gi * HEAD_DIM:(gi + 1) * HEAD_DIM])
    return jnp.concatenate(parts, axis=-1)


def _out_kernel(*refs, n_src, n_lat_tiles, has_ctx):
    x_refs, refs = refs[:n_src], refs[n_src:]
    if has_ctx:
        hmf, hmb, hgf, hgb, hdl, hdc, mo, go, mn, gn, w_ref, nw_ref, gate_ref, o_ref = refs
        hd = jnp.where(pl.program_id(0) < n_lat_tiles, hdl[...], hdc[...])
    else:
        hmf, hmb, hgf, hgb, hdl, mo, go, mn, gn, w_ref, nw_ref, gate_ref, o_ref = refs
        hd = hdl[...]
    rows = lambda ref: ref[...].reshape(OUT_TILE, 512)
    ym = _group_rmsnorm(rows(hmf) + rows(hmb), mn[...], 4) * jax.nn.sigmoid(mo[...])
    yg = _group_rmsnorm(rows(hgf) + rows(hgb), gn[...], 4) * _silu(go[...])
    y = jnp.concatenate([ym.astype(BF16), yg.astype(BF16), hd.astype(BF16)], axis=-1)
    z = jnp.dot(y, w_ref[...], preferred_element_type=F32)
    ms = jnp.mean(z * z, axis=-1, keepdims=True)
    o_ref[...] = _token_tile(x_refs, n_lat_tiles) + gate_ref[0] * (z * lax.rsqrt(ms + NORM_EPS) * nw_ref[...])


def _out_call(x_parts, hmf, hmb, hgf, hgb, hd_lat, hd_ctx, p, m_norm, g_norm, w_out, layer, nw, mod, cfg, n_rows):
    d = cfg.D
    tm = OUT_TILE
    nl = cfg.TL // tm
    row = functools.partial(_mod_row, cfg=cfg, tm=tm)
    rt = lambda w, c: pl.BlockSpec((tm, w), lambda i: (i, c))
    const = lambda i: (0, 0)
    lt, ct = cfg.n_lat // tm, cfg.n_ctx // tm

    def scan_index(i):
        k = i - cfg.B * lt
        return (jnp.where(k < 0, ct + i % lt, k % ct), jnp.where(k < 0, i // lt, k // ct), 0, 0)

    scan_blk = pl.BlockSpec((tm // CHUNK, 1, CHUNK, 512), scan_index)
    hd_specs = [pl.BlockSpec((tm, 1024), lambda i: (jnp.minimum(i, nl - 1), 0))]
    hd_args = [hd_lat]
    if hd_ctx is not None:
        hd_specs.append(pl.BlockSpec((tm, 1024), lambda i: (jnp.maximum(i - nl, 0), 0)))
        hd_args.append(hd_ctx)
    return pl.pallas_call(
        functools.partial(_out_kernel, n_src=len(x_parts), n_lat_tiles=nl, has_ctx=hd_ctx is not None),
        grid=(n_rows // tm,),
        in_specs=_token_specs(x_parts, tm, cfg) + [scan_blk, scan_blk, scan_blk, scan_blk] + hd_specs + [
                  rt(512, P_BLK_M_O), rt(512, P_BLK_G_OUT),
                  pl.BlockSpec((1, 512), const), pl.BlockSpec((1, 512), const),
                  pl.BlockSpec((None, d, d), lambda i: (layer, 0, 0)), pl.BlockSpec((1, d), const),
                  pl.BlockSpec((1, 1, d), lambda i: (row(i) * 6 + 2, 0, 0))],
        out_specs=rt(d, 0),
        out_shape=jax.ShapeDtypeStruct((n_rows, d), F32),
        compiler_params=_cparams(1),
        name="out_proj",
    )(*x_parts, hmf, hmb, hgf, hgb, *hd_args, p, p, m_norm, g_norm, w_out, nw, mod)


def _ffn_kernel(x_ref, nw_ref, shift_ref, scale_ref, wg_ref, wu_ref, wd_ref, pw_ref, gate_ref, o_ref,
                h_scr, acc_scr):
    j = pl.program_id(1)

    @pl.when(j == 0)
    def _():
        h_scr[...] = _prenorm(x_ref[...], nw_ref[...], shift_ref[0], scale_ref[0]).astype(BF16)
        acc_scr[...] = jnp.zeros_like(acc_scr)

    h = h_scr[...]
    a = jnp.dot(h, wg_ref[...], preferred_element_type=F32)
    u = jnp.dot(h, wu_ref[...], preferred_element_type=F32)
    acc_scr[...] += jnp.dot((_silu(a) * u).astype(BF16), wd_ref[...], preferred_element_type=F32)

    @pl.when(j == pl.num_programs(1) - 1)
    def _():
        z = acc_scr[...]
        ms = jnp.mean(z * z, axis=-1, keepdims=True)
        o_ref[...] = x_ref[...] + gate_ref[0] * (z * lax.rsqrt(ms + NORM_EPS) * pw_ref[...])


FFN_LAT_TILE = 512
FFN_CTX_TILE = 512
FFN_HID_TILE = 512


def _ffn_call(x, nw_pre, nw_post, mod, wg, wu, wd, cfg, row0, n_rows, tm):
    d, f = wg.shape
    tf = FFN_HID_TILE
    t0 = row0 // tm
    row = lambda i: _mod_row(i + t0, cfg, tm)
    const = lambda i, j: (0, 0)
    modspec = lambda kk: pl.BlockSpec((1, 1, d), lambda i, j: (row(i) * 6 + kk, 0, 0))
    return pl.pallas_call(
        _ffn_kernel,
        grid=(n_rows // tm, f // tf),
        in_specs=[pl.BlockSpec((tm, d), lambda i, j: (i + t0, 0)),
                  pl.BlockSpec((1, d), const), modspec(3), modspec(4),
                  pl.BlockSpec((d, tf), lambda i, j: (0, j)),
                  pl.BlockSpec((d, tf), lambda i, j: (0, j)),
                  pl.BlockSpec((tf, d), lambda i, j: (j, 0)),
                  pl.BlockSpec((1, d), const), modspec(5)],
        out_specs=pl.BlockSpec((tm, d), lambda i, j: (i, 0)),
        out_shape=jax.ShapeDtypeStruct((n_rows, d), F32),
        scratch_shapes=[pltpu.VMEM((tm, d), BF16), pltpu.VMEM((tm, d), F32)],
        compiler_params=_cparams(2),
        name="ffn",
    )(x, nw_pre, mod, mod, wg, wu, wd, nw_post, mod)


_MIX = {}
_off = 0
for _name, _w in (("m_q", 512), ("m_k", 512), ("m_v", 512), ("m_o", 512), ("m_gates", 16),
                  ("g_q", 256), ("g_k", 256), ("g_v", 512), ("g_out", 512), ("g_lr", 32),
                  ("d_q", 1024), ("d_k", 1024), ("d_v", 1024)):
    _MIX[_name] = (_off, _w)
    _off += _w
_P_ORDER = ("m_q", "m_k", "m_v", "m_o", "g_q", "g_k", "g_v", "g_out")
_QKV_ORDER = ("d_q", "d_k", "d_v")


def _split_w_in(w_in):
    cols = lambda n: w_in[:, _MIX[n][0]:_MIX[n][0] + _MIX[n][1]]
    w_a = jnp.concatenate([cols(n) for n in _P_ORDER], axis=1).astype(BF16)
    w_b = jnp.concatenate([cols(n) for n in _QKV_ORDER], axis=1).astype(BF16)
    w_small = jnp.concatenate([cols("m_gates"), cols("g_lr"),
                               jnp.zeros((w_in.shape[0], HEAD_DIM - 48), w_in.dtype)], axis=1).astype(BF16)
    return w_a, w_b, w_small


def _rope_tables(cfg):
    rows = cfg.n_lat // GRID_W
    r = np.repeat(np.arange(rows, dtype=np.float64), GRID_W)
    c = np.tile(np.arange(GRID_W, dtype=np.float64), rows)
    half = 16
    inv_freq = ROPE_BASE ** (-np.arange(half, dtype=np.float64) / half)
    ang_r, ang_c = r[:, None] * inv_freq, c[:, None] * inv_freq
    ang = np.concatenate([ang_r, ang_r, ang_c, ang_c], axis=-1)
    ang = np.tile(ang, (cfg.B, 2))
    n_c = cfg.B * cfg.n_ctx
    cos_t = np.concatenate([np.cos(ang), np.ones((n_c, HEAD_DIM))], axis=0).astype(np.float32)
    sin_t = np.concatenate([np.sin(ang), np.zeros((n_c, HEAD_DIM))], axis=0).astype(np.float32)
    return jnp.asarray(cos_t), jnp.asarray(sin_t)


def _layer(x_parts, mod, lw, stacks, layer, lam_init, rope, cfg, need_ctx, w_in_parts):
    d = cfg.D
    mq, mk, p, qkv, ps = _in_call(x_parts, lw["norm_mix_pre"].reshape(1, d), mod, *w_in_parts, *rope,
                                  lw["mlstm_conv_w"], lw["mlstm_conv_b"], cfg)
    w2 = lw["gla_gate_w2"]
    w2p = jnp.zeros((HEAD_DIM, 2 * G_QK), F32)
    w2p = w2p.at[16:32, 0:G_QK].set(w2[0]).at[32:48, G_QK:].set(w2[1])
    hmf, hmb, hgf, hgb = _scan_call(mq, mk, p, ps, lw["mlstm_gate_b"], w2p,
                                    lw["gla_gate_b"].reshape(1, 2 * G_QK), cfg)

    nxt = layer + 1 if need_ctx else None
    hd_lat, wg, wu, wd, *next_w_in = _attn_call(
        qkv, lw["diff_lambda"], lw["diff_subln"], lam_init, cfg, latent=True,
        cast=(layer, [stacks["w_ffn_gate"], stacks["w_ffn_up"], stacks["w_ffn_down"]]),
        regroup=None if nxt is None else (nxt, stacks["w_in"]))
    hd_ctx = _attn_call(qkv, lw["diff_lambda"], lw["diff_subln"], lam_init, cfg, latent=False) if need_ctx else None

    n_rows = cfg.T if need_ctx else cfg.TL
    xt = _out_call(x_parts, hmf, hmb, hgf, hgb, hd_lat, hd_ctx, p, lw["mlstm_norm"].reshape(1, 512),
                   lw["gla_norm"].reshape(1, 512), stacks["w_out"], layer,
                   lw["norm_mix_post"].reshape(1, d), mod, cfg, n_rows)
    ffn = functools.partial(_ffn_call, xt, lw["norm_ffn_pre"].reshape(1, d), lw["norm_ffn_post"].reshape(1, d),
                            mod, wg, wu, wd, cfg)
    out = (ffn(0, cfg.TL, FFN_LAT_TILE),)
    if need_ctx:
        out += (ffn(cfg.TL, cfg.T - cfg.TL, FFN_CTX_TILE),)
    return out, (tuple(next_w_in) if next_w_in else None)


_LAYER_KEYS = ("norm_mix_pre", "norm_mix_post", "norm_ffn_pre", "norm_ffn_post", "w_in", "mlstm_conv_w",
               "mlstm_conv_b", "mlstm_gate_b", "mlstm_norm", "gla_gate_w2", "gla_gate_b", "gla_norm",
               "diff_lambda", "diff_subln", "w_out", "w_ffn_gate", "w_ffn_up", "w_ffn_down")
_STACK_KEYS = ("w_in", "w_out", "w_ffn_gate", "w_ffn_up", "w_ffn_down")


def kernel(x, c, ctx, c_ctx, w_mod, b_mod, norm_mix_pre, norm_mix_post, norm_ffn_pre, norm_ffn_post, w_in, mlstm_conv_w, mlstm_conv_b, mlstm_gate_b, mlstm_norm, gla_gate_w2, gla_gate_b, gla_norm, diff_lambda, diff_subln, w_out, w_ffn_gate, w_ffn_up, w_ffn_down):
    weights = dict(zip(_LAYER_KEYS, (norm_mix_pre, norm_mix_post, norm_ffn_pre, norm_ffn_post, w_in,
                                     mlstm_conv_w, mlstm_conv_b, mlstm_gate_b, mlstm_norm, gla_gate_w2,
                                     gla_gate_b, gla_norm, diff_lambda, diff_subln, w_out, w_ffn_gate,
                                     w_ffn_up, w_ffn_down)))
    b, n_lat, d = x.shape
    cfg = Cfg(B=b, n_ctx=ctx.shape[1], n_lat=n_lat, D=d, F=w_ffn_gate.shape[-1])
    depth = w_mod.shape[0]
    c8 = jnp.zeros((8, d), F32).at[0].set(c_ctx).at[1:1 + b].set(c)
    mods = _mod_call(c8, w_mod, b_mod).reshape(depth, 8 * 6, 1, d)
    rope = _rope_tables(cfg)
    x_parts = (x.reshape(b * n_lat, d), ctx.reshape(b * ctx.shape[1], d))
    stacks = {k: weights.pop(k) for k in _STACK_KEYS}
    stacks["w_out"] = stacks["w_out"].astype(BF16)
    w_in_parts = _split_w_in(stacks["w_in"][0])
    for layer in range(depth):
        lw = {k: v[layer] for k, v in weights.items()}
        lam_init = 0.8 - 0.6 * math.exp(-0.3 * layer)
        x_parts, w_in_parts = _layer(x_parts, mods[layer], lw, stacks, layer, lam_init, rope, cfg,
                                     layer < depth - 1, w_in_parts)
    return x_parts[0].reshape(b, n_lat, d)
```

```python
import dataclasses
import functools
import math

import jax
import jax.numpy as jnp
import numpy as np
from jax import lax
from jax.experimental import pallas as pl
from jax.experimental.pallas import tpu as pltpu

F32 = jnp.float32
BF16 = jnp.bfloat16
NORM_EPS = 1e-6
CHUNK = 64
GRID_W = 64
ROPE_BASE = 10000.0
GLA_TAU = 16.0
HEAD_DIM = 128
VMEM_LIMIT_BYTES = 56 * 1024 * 1024
HIGHEST = lax.Precision.HIGHEST


@dataclasses.dataclass(frozen=True)
class Cfg:
    B: int = 2
    n_ctx: int = 256
    n_lat: int = 4096
    D: int = 2048
    F: int = 5632

    @property
    def T(self):
        return self.B * (self.n_ctx + self.n_lat)

    @property
    def TL(self):
        return self.B * self.n_lat


def _cparams(n_axes):
    return pltpu.CompilerParams(dimension_semantics=("arbitrary",) * n_axes,
                                vmem_limit_bytes=VMEM_LIMIT_BYTES)


def _mod_row(i, cfg, tm):
    lt = cfg.n_lat // tm
    return jnp.where(i < cfg.B * lt, 1 + i // lt, 0)


def _chunk_block(b, c, cfg):
    ncc, ncl = cfg.n_ctx // CHUNK, cfg.n_lat // CHUNK
    return jnp.where(c < ncc, cfg.B * ncl + b * ncc + c, b * ncl + (c - ncc))


def _bwd_chunk(s, cfg):
    ncc, ncl = cfg.n_ctx // CHUNK, cfg.n_lat // CHUNK
    return jnp.where(s < ncc, ncc - 1 - s, ncc + ncl - 1 - (s - ncc))


def _log_sigmoid(x):
    return jnp.minimum(x, 0.0) - jnp.log1p(jnp.exp(-jnp.abs(x)))


def _silu(x):
    return x * jax.nn.sigmoid(x)


MOD_K_TILE = 256


def _regroup_w_in(w, wa_out, wb_out, ws_out):
    col = lambda n: w[:, _MIX[n][0]:_MIX[n][0] + _MIX[n][1]]
    wa_out[...] = jnp.concatenate([col(n) for n in _P_ORDER], axis=1).astype(BF16)
    wb_out[...] = jnp.concatenate([col(n) for n in _QKV_ORDER], axis=1).astype(BF16)
    pad = jnp.zeros((w.shape[0], HEAD_DIM - 48), F32)
    ws_out[...] = jnp.concatenate([col("m_gates"), col("g_lr"), pad], axis=1).astype(BF16)


def _mod_kernel(c_ref, w0_ref, w1_ref, b_ref, win_ref, o_ref, wa_out, wb_out, ws_out):
    k = pl.program_id(1)
    _regroup_w_in(win_ref[...], wa_out, wb_out, ws_out)

    @pl.when(k == 0)
    def _():
        o_ref[0] = jnp.broadcast_to(b_ref[0], o_ref.shape[1:])

    s = _silu(c_ref[...]).astype(BF16)
    half = MOD_K_TILE // 2
    o_ref[0] += (jnp.dot(s[:, 0:half], w0_ref[0].astype(BF16), preferred_element_type=F32)
                 + jnp.dot(s[:, half:], w1_ref[0].astype(BF16), preferred_element_type=F32))


def _mod_call(c8, w_mod, b_mod, w_in):
    depth, d, n = w_mod.shape
    tk = MOD_K_TILE
    nk = d // tk
    _, r, c = w_in.shape
    slab = r // (depth * nk)
    assert r % (depth * nk) == 0 and slab % 16 == 0, (r, depth, nk)
    step = lambda l, k: l * nk + k
    regroup_out = [pl.BlockSpec((slab, width), lambda l, k: (step(l, k), 0)) for width in (P_COLS, QKV_COLS, HEAD_DIM)]
    outs = pl.pallas_call(
        _mod_kernel,
        grid=(depth, nk),
        in_specs=[pl.BlockSpec((8, tk), lambda l, k: (0, k)),
                  pl.BlockSpec((1, tk // 2, n), lambda l, k: (l, 2 * k, 0)),
                  pl.BlockSpec((1, tk // 2, n), lambda l, k: (l, 2 * k + 1, 0)),
                  pl.BlockSpec((1, 1, n), lambda l, k: (l, 0, 0)),
                  pl.BlockSpec((None, slab, c), lambda l, k: (0, step(l, k), 0))],
        out_specs=[pl.BlockSpec((1, 8, n), lambda l, k: (l, 0, 0))] + regroup_out,
        out_shape=[jax.ShapeDtypeStruct((depth, 8, n), F32)]
                  + [jax.ShapeDtypeStruct((r, width), BF16) for width in (P_COLS, QKV_COLS, HEAD_DIM)],
        compiler_params=_cparams(2),
        name="adaln_mod",
    )(c8, w_mod, w_mod, b_mod.reshape(depth, 1, n), w_in)
    return outs[0], tuple(outs[1:])


def _prenorm(x, nw, shift, scale):
    ms = jnp.mean(x * x, axis=-1, keepdims=True)
    return (x * lax.rsqrt(ms + NORM_EPS) * nw) * (1.0 + scale) + shift


D_HEADS = 8
P_COLS = 3584
M_CONV_COLS = 1024
P_BLK_M_V, P_BLK_M_O, P_BLK_G_QK, P_BLK_G_V, P_BLK_G_OUT = range(5)
QKV_COLS = 3072
IN_TILE = 256


def _token_specs(x_parts, tm, cfg):
    d = cfg.D
    if len(x_parts) == 1:
        return [pl.BlockSpec((tm, d), lambda i: (i, 0))]
    nl = cfg.TL // tm
    return [pl.BlockSpec((tm, d), lambda i: (jnp.minimum(i, nl - 1), 0)),
            pl.BlockSpec((tm, d), lambda i: (jnp.maximum(i - nl, 0), 0))]


def _token_tile(x_refs, n_lat_tiles):
    if len(x_refs) == 1:
        return x_refs[0][...]
    return jnp.where(pl.program_id(0) < n_lat_tiles, x_refs[0][...], x_refs[1][...])


def _halo_specs(x_parts, tm, cfg):
    d, r8 = cfg.D, tm // 8
    nl = cfg.TL // tm
    specs = []
    for part, tile0 in zip(x_parts, (0, nl)):
        last8 = part.shape[0] // 8 - 1
        clamp = lambda v, last8=last8: jnp.clip(v, 0, last8)
        specs.append(pl.BlockSpec((8, d), lambda i, t0=tile0, c=clamp: (c((i - t0) * r8 - 1), 0)))
        specs.append(pl.BlockSpec((8, d), lambda i, t0=tile0, c=clamp: (c((i - t0 + 1) * r8), 0)))
    return specs


def _in_kernel(*refs, n_src, n_lat_tiles, seg_first, seg_last):
    x_refs, halo_refs = refs[:n_src], refs[n_src:3 * n_src]
    (nw_ref, shift_ref, scale_ref, wa_ref, wb_ref, ws_ref, cos_ref, sin_ref, cw_ref, cb_ref,
     mq_ref, mk_ref, p_ref, qkv_ref, ps_ref) = refs[3 * n_src:]
    i = pl.program_id(0)
    norm = lambda x: _prenorm(x, nw_ref[...], shift_ref[0], scale_ref[0]).astype(BF16)
    h = norm(_token_tile(x_refs, n_lat_tiles))
    ps_ref[...] = jnp.dot(h, ws_ref[...], preferred_element_type=F32)
    p_ref[...] = jnp.dot(h, wa_ref[:, M_CONV_COLS:], preferred_element_type=F32)

    halo = jnp.concatenate([_token_tile(halo_refs[0::2], n_lat_tiles), _token_tile(halo_refs[1::2], n_lat_tiles)],
                           axis=0)
    pm = jnp.dot(jnp.concatenate([h, norm(halo)], axis=0), wa_ref[:, 0:M_CONV_COLS], preferred_element_type=F32)
    is_first = functools.reduce(jnp.logical_or, [i == s for s in seg_first])
    is_last = functools.reduce(jnp.logical_or, [i == s for s in seg_last])
    prev_row = jnp.where(is_first, 0.0, pm[IN_TILE + 7:IN_TILE + 8])
    next_row = jnp.where(is_last, 0.0, pm[IN_TILE + 8:IN_TILE + 9])
    xm = pm[0:IN_TILE]
    rows = lax.broadcasted_iota(jnp.int32, xm.shape, 0)
    x_prev = jnp.where(rows == 0, prev_row, pltpu.roll(xm, 1, 0))
    x_next = jnp.where(rows == IN_TILE - 1, next_row, pltpu.roll(xm, IN_TILE - 1, 0))
    conv = _silu(x_prev * cw_ref[0:1, :] + xm * cw_ref[1:2, :] + x_next * cw_ref[2:3, :] + cb_ref[...])
    mq_ref[...] = conv[:, 0:M_CONV_COLS // 2] * (HEAD_DIM ** -0.5)
    mk_ref[...] = conv[:, M_CONV_COLS // 2:]

    qkv = jnp.dot(h, wb_ref[...], preferred_element_type=F32)
    lane = lax.broadcasted_iota(jnp.int32, (IN_TILE, HEAD_DIM), 1)
    low = (lane % 32) < 16
    cos, sin = cos_ref[...], sin_ref[...]
    q_scale = 64 ** -0.5 * math.log2(math.e)
    for s in range(2 * D_HEADS):
        cs = slice(s * HEAD_DIM, (s + 1) * HEAD_DIM)
        x = qkv[:, cs]
        rot = jnp.where(low, -pltpu.roll(x, HEAD_DIM - 16, 1), pltpu.roll(x, 16, 1))
        y = x * cos + rot * sin
        qkv_ref[:, cs] = ((y * q_scale) if s < D_HEADS else y).astype(BF16)
    vs = slice(2 * D_HEADS * HEAD_DIM, QKV_COLS)
    qkv_ref[:, vs] = qkv[:, vs].astype(BF16)


def _in_call(x_parts, nw, mod, w_a, w_b, w_small, cos_t, sin_t, conv_w, conv_b, cfg):
    t, d = cfg.T, cfg.D
    tm = IN_TILE
    row = functools.partial(_mod_row, cfg=cfg, tm=tm)
    resident = lambda shape: pl.BlockSpec(shape, lambda i: (0, 0), pipeline_mode=pl.Buffered(1))
    lt, ct = cfg.n_lat // tm, cfg.n_ctx // tm
    seg_first = [b * lt for b in range(cfg.B)] + [cfg.B * lt + b * ct for b in range(cfg.B)]
    seg_last = [b * lt + lt - 1 for b in range(cfg.B)] + [cfg.B * lt + b * ct + ct - 1 for b in range(cfg.B)]
    half = M_CONV_COLS // 2
    tile = lambda w: pl.BlockSpec((tm, w), lambda i: (i, 0))
    return pl.pallas_call(
        functools.partial(_in_kernel, n_src=len(x_parts), n_lat_tiles=cfg.TL // tm,
                          seg_first=tuple(seg_first), seg_last=tuple(seg_last)),
        grid=(t // tm,),
        in_specs=_token_specs(x_parts, tm, cfg) + _halo_specs(x_parts, tm, cfg) + [
                  resident((1, d)),
                  pl.BlockSpec((1, 1, d), lambda i: (row(i) * 6 + 0, 0, 0)),
                  pl.BlockSpec((1, 1, d), lambda i: (row(i) * 6 + 1, 0, 0)),
                  resident((d, P_COLS)), resident((d, QKV_COLS)), resident((d, HEAD_DIM)),
                  tile(HEAD_DIM), tile(HEAD_DIM),
                  resident((3, M_CONV_COLS)), resident((1, M_CONV_COLS))],
        out_specs=[tile(half), tile(half), tile(P_COLS - M_CONV_COLS), tile(QKV_COLS), tile(HEAD_DIM)],
        out_shape=[jax.ShapeDtypeStruct((t, half), F32), jax.ShapeDtypeStruct((t, half), F32),
                   jax.ShapeDtypeStruct((t, P_COLS - M_CONV_COLS), F32),
                   jax.ShapeDtypeStruct((t, QKV_COLS), BF16), jax.ShapeDtypeStruct((t, HEAD_DIM), F32)],
        compiler_params=_cparams(1),
        name="in_proj",
    )(*x_parts, *[part for part in x_parts for _ in range(2)], nw, mod, mod, w_a, w_b, w_small, cos_t, sin_t,
      conv_w, conv_b.reshape(1, M_CONV_COLS))


M_HEADS = 4


def _split3(x):
    hi = x.astype(BF16)
    r1 = x - hi.astype(F32)
    mid = r1.astype(BF16)
    return hi, mid, (r1 - mid.astype(F32)).astype(BF16)


def _cumsum_matmul(mask, parts):
    return sum(jnp.dot(mask, part, preferred_element_type=F32) for part in parts)


def _mlstm_body(qf, kf, vf, gf, qb, kb, vb, gb, brow, bcol, of, ob, c_scr, m_scr):
    L = CHUNK
    row = lax.broadcasted_iota(jnp.int32, (L, L), 0)
    col = lax.broadcasted_iota(jnp.int32, (L, L), 1)
    neg_inf = jnp.float32(-jnp.inf)

    gcol = jnp.concatenate([gf[...], gb[...]], axis=0) + brow[...]
    grow = jnp.concatenate([gf[...], gb[...]], axis=0).T + bcol[...]
    r8 = lax.broadcasted_iota(jnp.int32, (8, 2 * L), 0)
    l8 = lax.broadcasted_iota(jnp.int32, (8, 2 * L), 1)
    fwd_row = r8 < M_HEADS
    own = (r8 // M_HEADS) == (l8 // L)
    ig = jnp.concatenate([grow[0:4], grow[8:12]], axis=0)
    logf = _log_sigmoid(jnp.concatenate([grow[4:8], grow[12:16]], axis=0))
    tl = lax.broadcasted_iota(jnp.int32, (2 * L, 2 * L), 0)
    ti = lax.broadcasted_iota(jnp.int32, (2 * L, 2 * L), 1)
    same_half = (tl // L) == (ti // L)
    scan_rows = jnp.logical_and(same_half, jnp.where(ti < L, tl - ti, ti - tl) <= 0)
    scan_cols = jnp.logical_and(same_half, jnp.where(tl < L, ti - tl, tl - ti) <= 0)
    cf = sum(jnp.dot(part, scan_rows.astype(BF16), preferred_element_type=F32) for part in _split3(logf))
    u = jnp.where(own, ig - cf, neg_inf)
    m_st = m_scr[...]
    end_lane = jnp.where(fwd_row, L - 1, L)
    f_end = jnp.broadcast_to(jnp.sum(jnp.where(l8 == end_lane, cf, 0.0), axis=-1, keepdims=True),
                             (8, 2 * L))
    dec = jnp.where(own, f_end - cf + ig, neg_inf)
    m_new = jnp.maximum(f_end + m_st, jnp.max(dec, axis=-1, keepdims=True))
    a_prev = jnp.exp(f_end + m_st - m_new)
    m_scr[...] = m_new
    u_keys = jnp.where(fwd_row, u, pltpu.roll(u, L, 1))
    cum_col = _cumsum_matmul(scan_cols.astype(BF16), _split3(_log_sigmoid(gcol)))

    rep = lambda colv: jnp.broadcast_to(colv, (L, HEAD_DIM))
    dir_refs = ((qf, kf, vf, of), (qb, kb, vb, ob))
    pairs = [(d, h) for d in range(2) for h in range(M_HEADS)]
    work = []
    for d, h in pairs:
        q_ref, k_ref, v_ref, _ = dir_refs[d]
        hs = slice(h * HEAD_DIM, (h + 1) * HEAD_DIM)
        q = q_ref[:, hs].astype(BF16)
        k32 = k_ref[:, hs]
        v1 = jnp.concatenate([v_ref[:, hs].astype(BF16), jnp.ones((L, HEAD_DIM), BF16)], axis=1)
        cn = c_scr[d, h]
        qk = lax.dot_general(q, k32.astype(BF16), (((1,), (1,)), ((), ())), preferred_element_type=F32)
        r_state = jnp.dot(q, cn.astype(BF16), preferred_element_type=F32)
        work.append((k32, v1, cn, qk, r_state))
    yield
    for (d, h), (k32, v1, cn, qk, r_state) in zip(pairs, work):
        o_ref = dir_refs[d][3]
        valid = (col <= row) if d == 0 else (col >= row)
        ts = slice(d * L, (d + 1) * L)
        r = d * M_HEADS + h
        ci, cfc = d * 8 + h, d * 8 + 4 + h
        hs = slice(h * HEAD_DIM, (h + 1) * HEAD_DIM)
        u_tile = jnp.where(valid, u_keys[r:r + 1, 0:L], neg_inf)
        m_rep = rep(jnp.maximum(jnp.max(u_tile, axis=-1, keepdims=True), m_st[r:r + 1, 0:1]))
        cf_rep = rep(cum_col[ts, cfc:cfc + 1])
        ig_rep = rep(gcol[ts, ci:ci + 1])
        w_inter = jnp.exp(m_st[r:r + 1] - m_rep)
        e = jnp.exp(u_tile - m_rep[:, 0:L])
        r_chunk = jnp.dot((qk * e).astype(BF16), v1, preferred_element_type=F32)
        num = w_inter * r_state[:, 0:HEAD_DIM] + r_chunk[:, 0:HEAD_DIM]
        den = w_inter * r_state[:, HEAD_DIM:] + r_chunk[:, HEAD_DIM:]
        o_ref[:, hs] = num / jnp.maximum(jnp.abs(den), jnp.exp(-(cf_rep + m_rep)))

        ws = jnp.exp(f_end[r:r + 1] - cf_rep + ig_rep - m_new[r:r + 1])
        wk = (ws * k32).astype(BF16)
        decay = jnp.concatenate([a_prev[r:r + 1], a_prev[r:r + 1]], axis=1)
        c_scr[d, h] = decay * cn + lax.dot_general(wk, v1, (((0,), (0,)), ((), ())),
                                                   preferred_element_type=F32)


N_MLSTM_REFS = 8
N_GLA_REFS = 6


def _scan_kernel(*refs, n_b):
    it = iter(refs)
    take = lambda n: [next(it) for _ in range(n)]
    m_in = [take(N_MLSTM_REFS) for _ in range(n_b)]
    brow, bcol = take(2)
    g_in = [take(N_GLA_REFS) for _ in range(n_b)]
    w2_ref, b2_ref = take(2)
    m_of, m_ob, g_of, g_ob = take(4)
    c_scr, m_scr, s_scr, gl_scr = take(4)

    @pl.when(pl.program_id(0) == 0)
    def _():
        c_scr[...] = jnp.zeros_like(c_scr)
        m_scr[...] = jnp.zeros_like(m_scr)
        s_scr[...] = jnp.zeros_like(s_scr)

    bodies = []
    for b in range(n_b):
        bodies.append(_mlstm_body(*m_in[b], brow, bcol, m_of.at[0, b], m_ob.at[0, b], c_scr.at[b], m_scr.at[b]))
        bodies.append(_gla_body(*g_in[b], w2_ref, b2_ref, g_of.at[0, b], g_ob.at[0, b], s_scr.at[b], gl_scr.at[b]))
    while bodies:
        bodies = [body for body in bodies if next(body, "done") != "done"]


def _scan_call(mq, mk, p, ps, gate_b, w2p, b2p, cfg):
    nc = (cfg.n_ctx + cfg.n_lat) // CHUNK
    fwd = lambda b, col: (lambda s: (_chunk_block(b, s, cfg), col))
    bwd = lambda b, col: (lambda s: (_chunk_block(b, _bwd_chunk(s, cfg), cfg), col))
    blk = lambda w, f: pl.BlockSpec((CHUNK, w), f)
    brow = jnp.zeros((1, HEAD_DIM), F32).at[0, :16].set(gate_b)
    const = lambda s: (0, 0)
    in_specs, args = [], []
    for b in range(cfg.B):
        for way in (fwd, bwd):
            in_specs += [blk(512, way(b, 0)), blk(512, way(b, 0)), blk(512, way(b, P_BLK_M_V)),
                         blk(HEAD_DIM, way(b, 0))]
            args += [mq, mk, p, ps]
    in_specs += [pl.BlockSpec((1, HEAD_DIM), const), pl.BlockSpec((HEAD_DIM, 1), const)]
    args += [brow, brow.reshape(HEAD_DIM, 1)]
    for b in range(cfg.B):
        for way in (fwd, bwd):
            in_specs += [blk(512, way(b, P_BLK_G_QK)), blk(512, way(b, P_BLK_G_V)), blk(HEAD_DIM, way(b, 0))]
            args += [p, p, ps]
    in_specs += [pl.BlockSpec((HEAD_DIM, 2 * G_QK), const), pl.BlockSpec((1, 2 * G_QK), const)]
    args += [w2p, b2p]
    out_f = pl.BlockSpec((1, cfg.B, CHUNK, 512), lambda s: (s, 0, 0, 0))
    out_b = pl.BlockSpec((1, cfg.B, CHUNK, 512), lambda s: (_bwd_chunk(s, cfg), 0, 0, 0))
    outs = pl.pallas_call(
        functools.partial(_scan_kernel, n_b=cfg.B),
        grid=(nc,),
        in_specs=in_specs,
        out_specs=[out_f, out_b, out_f, out_b],
        out_shape=[jax.ShapeDtypeStruct((nc, cfg.B, CHUNK, 512), F32)] * 4,
        scratch_shapes=[pltpu.VMEM((cfg.B, 2, M_HEADS, HEAD_DIM, 2 * HEAD_DIM), F32),
                        pltpu.VMEM((cfg.B, 2 * M_HEADS, 2 * CHUNK), F32),
                        pltpu.VMEM((cfg.B, 2, HEAD_DIM, G_QK), F32),
                        pltpu.VMEM((cfg.B, 2, CHUNK, G_QK), F32)],
        compiler_params=_cparams(1),
        name="recurrent_scan",
    )(*args)
    return outs


G_QK = 256
G_V = 512


G_HEADS = 4
G_DK = 64
GLA_BLK = 16


def _gla_body(qkf, vf, lf, qkb, vb, lb, w2_ref, b2_ref, of, ob, s_scr, gl_scr):
    L, C = CHUNK, GLA_BLK
    nb = L // C
    row = lax.broadcasted_iota(jnp.int32, (L, L), 0)
    col = lax.broadcasted_iota(jnp.int32, (L, L), 1)
    same_blk = (row // C) == (col // C)
    rows8 = lax.broadcasted_iota(jnp.int32, (8, G_QK), 0)
    neg_inf = jnp.float32(-jnp.inf)
    he_r = lax.broadcasted_iota(jnp.int32, (G_QK, G_V), 0) // G_DK
    he_c = lax.broadcasted_iota(jnp.int32, (G_QK, G_V), 1) // HEAD_DIM
    head_expand = (he_r == he_c).astype(BF16)

    def per_head_rows(x):
        lane_head = lax.broadcasted_iota(jnp.int32, x.shape, 1) // G_DK
        return jnp.concatenate([jnp.where(lane_head == h, x, 0.0) for h in range(G_HEADS)], axis=0).astype(BF16)

    def head_blocks(r, n):
        return jnp.concatenate([r[h * n:(h + 1) * n, h * HEAD_DIM:(h + 1) * HEAD_DIM] for h in range(G_HEADS)],
                               axis=1)

    def direction(d, qk_ref, v_ref, l_ref, o_ref):
        cs = slice(d * G_QK, (d + 1) * G_QK)
        z = jnp.dot(l_ref[...], w2_ref[:, cs], precision=HIGHEST, preferred_element_type=F32) + b2_ref[:, cs]
        log_a = _log_sigmoid(z) * (1.0 / GLA_TAU)
        tri = (col <= row) if d == 0 else (col >= row)
        log_a_parts = _split3(log_a)
        g = _cumsum_matmul(tri.astype(BF16), log_a_parts)
        gl = _cumsum_matmul(jnp.logical_and(tri, same_blk).astype(BF16), log_a_parts)
        gl_scr[d] = gl
        q = qk_ref[:, 0:G_QK] * (G_DK ** -0.5)
        k = qk_ref[:, G_QK:2 * G_QK]
        v = v_ref[...]
        vb16 = v.astype(BF16)
        g_end = g[L - 1:L, :] if d == 0 else g[0:1, :]
        st = s_scr[d]

        r = lax.dot_general(per_head_rows(q * jnp.exp(g)), st.astype(BF16), (((1,), (1,)), ((), ())),
                            preferred_element_type=F32)
        inter = jnp.concatenate([r[h * L:(h + 1) * L] for h in range(G_HEADS)], axis=1)
        yield

        ql = q * jnp.exp(gl)
        first = lambda b: b * C + (C - 1 if d == 0 else 0)
        tot = [gl[first(b):first(b) + 1] for b in range(nb)]
        k_end = [k[b * C:(b + 1) * C] * jnp.exp(tot[b] - gl[b * C:(b + 1) * C]) for b in range(nb)]

        atts, v_cats = {}, {}
        for bi in range(nb):
            earlier = list(range(bi)) if d == 0 else list(range(bi + 1, nb))
            if not earlier:
                continue
            ks, vs = [], []
            for bj in earlier:
                between = range(bj + 1, bi) if d == 0 else range(bi + 1, bj)
                kj = k_end[bj]
                if len(between):
                    kj = kj * jnp.exp(functools.reduce(jnp.add, [tot[m] for m in between]))
                ks.append(kj)
                vs.append(vb16[bj * C:(bj + 1) * C])
            v_cats[bi] = jnp.concatenate(vs, axis=0)
            atts[bi] = lax.dot_general(per_head_rows(ql[bi * C:(bi + 1) * C]),
                                       jnp.concatenate(ks, axis=0).astype(BF16), (((1,), (1,)), ((), ())),
                                       preferred_element_type=F32)

        k_hat = per_head_rows(k * jnp.exp(g_end - g))
        v_rows = jnp.concatenate([vb16[:, h * HEAD_DIM:(h + 1) * HEAD_DIM] for h in range(G_HEADS)], axis=0)
        s_new = st * jnp.exp(g_end) + lax.dot_general(v_rows, k_hat, (((0,), (0,)), ((), ())),
                                                      preferred_element_type=F32)
        yield

        pieces, spans = [], []
        for i in range(L):
            bi, il = divmod(i, C)
            lo, hi = ((il // 8) * 8, C) if d == 0 else (0, (il // 8) * 8 + 8)
            edge = (lo, lo + 8) if d == 0 else (hi - 8, hi)
            k_i = qk_ref[i:i + 1, G_QK:2 * G_QK]
            g_i = gl_scr[d, i:i + 1, :]
            vis = (rows8 >= il % 8) if d == 0 else (rows8 <= il % 8)
            diff = jnp.where(vis, gl[bi * C + edge[0]:bi * C + edge[1]] - g_i, neg_inf)
            if hi - lo > 8:
                other = gl[bi * C + 8:bi * C + 16] if d == 0 else gl[bi * C:bi * C + 8]
                diff = jnp.concatenate([diff, other - g_i] if d == 0 else [other - g_i, diff], axis=0)
            pieces.append((q[bi * C + lo:bi * C + hi] * k_i) * jnp.exp(diff))
            spans.append((lo, hi))
        w_all = jnp.dot(jnp.concatenate(pieces, axis=0).astype(BF16), head_expand, preferred_element_type=F32)
        yield
        cross = {bi: head_blocks(jnp.dot(att.astype(BF16), v_cats[bi], preferred_element_type=F32), C)
                 for bi, att in atts.items()}
        yield

        blocks = []
        w_off = 0
        for bi in range(nb):
            acc = inter[bi * C:(bi + 1) * C]
            if bi in cross:
                acc = acc + cross[bi]
            for il in range(C):
                i = bi * C + il
                lo, hi = spans[i]
                upd = acc[lo:hi] + w_all[w_off:w_off + hi - lo] * v_ref[i:i + 1, :]
                w_off += hi - lo
                parts = ([acc[:lo]] if lo > 0 else []) + [upd] + ([acc[hi:]] if hi < C else [])
                acc = jnp.concatenate(parts, axis=0) if len(parts) > 1 else upd
            blocks.append(acc)
        o_ref[...] = jnp.concatenate(blocks, axis=0)
        s_scr[d] = s_new

    stages = [direction(0, qkf, vf, lf, of), direction(1, qkb, vb, lb, ob)]
    for _ in range(5):
        for stage in stages:
            next(stage, None)
        yield


ATT_SUB_ROWS = 256
ATT_TQ = 1024
ATT_KEY_CHUNK = 1024


def _attn_kernel(*refs, has_lat, lam_init, n_cast, regroup):
    if has_lat:
        q_ref, kc_ref, vc_ref, kl_ref, vl_ref, dl_ref, sub_ref = refs[:7]
        kv = ((kc_ref, vc_ref), (kl_ref, vl_ref))
        refs = refs[7:]
    else:
        q_ref, kc_ref, vc_ref, dl_ref, sub_ref = refs[:5]
        kv = ((kc_ref, vc_ref),)
        refs = refs[5:]
    cast_in, refs = refs[:n_cast], refs[n_cast:]
    if regroup:
        rg_in, refs = refs[0], refs[1:]
    o_ref, refs = refs[0], refs[1:]
    cast_out, refs = refs[:n_cast], refs[n_cast:]
    if regroup:
        (wa_out, wb_out, ws_out), refs = refs[:3], refs[3:]
    s_scr, vo_scr = refs
    for src, dst in zip(cast_in, cast_out):
        dst[...] = src[...].astype(BF16)
    if regroup:
        _regroup_w_in(rg_in[...], wa_out, wb_out, ws_out)

    @pl.when(pl.program_id(2) == 0)
    def _():
        off = 0
        for _, v_ref in kv:
            n = v_ref.shape[0]
            vo_scr[off:off + n, 0:HEAD_DIM] = v_ref[...]
            vo_scr[off:off + n, HEAD_DIM:2 * HEAD_DIM] = jnp.ones((n, HEAD_DIM), BF16)
            off += n

    n_sub = q_ref.shape[0] // ATT_SUB_ROWS
    lane = lax.broadcasted_iota(jnp.int32, (ATT_SUB_ROWS, HEAD_DIM), 1)
    zero = jnp.zeros((ATT_SUB_ROWS, HEAD_DIM), BF16)
    dl = dl_ref[...]
    lam = (jnp.exp(jnp.sum(dl[0:1] * dl[1:2], axis=-1, keepdims=True))
           - jnp.exp(jnp.sum(dl[2:3] * dl[3:4], axis=-1, keepdims=True)) + lam_init)
    key_chunks, off = [], 0
    for k_ref, _ in kv:
        for c0 in range(0, k_ref.shape[0], ATT_KEY_CHUNK):
            n = min(ATT_KEY_CHUNK, k_ref.shape[0] - c0)
            key_chunks.append((k_ref, c0, off + c0, n))
        off += k_ref.shape[0]

    prev = None
    for sb in range(n_sub + 1):
        if sb < n_sub:
            q = q_ref[sb * ATT_SUB_ROWS:(sb + 1) * ATT_SUB_ROWS, :]
            qms = (jnp.where(lane < 64, q, zero), jnp.where(lane >= 64, q, zero))
            part_max = [None, None]
        accs = [None, None]
        for k_ref, c0, off, n in key_chunks:
            for m in range(2):
                if sb < n_sub:
                    s = lax.dot_general(qms[m], k_ref[c0:c0 + n, :], (((1,), (1,)), ((), ())),
                                        preferred_element_type=F32)
                    s_scr[2 * sb + m, :, off:off + n] = s
                    folded = functools.reduce(jnp.maximum, [s[:, j:j + HEAD_DIM] for j in range(0, n, HEAD_DIM)])
                    part_max[m] = folded if part_max[m] is None else jnp.maximum(part_max[m], folded)
                if prev is not None:
                    p = jnp.exp2(s_scr[2 * (sb - 1) + m, :, off:off + n] - prev[m]).astype(BF16)
                    part = jnp.dot(p, vo_scr[off:off + n, :], preferred_element_type=F32)
                    accs[m] = part if accs[m] is None else accs[m] + part
        if prev is not None:
            outs = [a[:, 0:HEAD_DIM] / a[:, HEAD_DIM:HEAD_DIM + 1] for a in accs]
            out = outs[0] - lam * outs[1]
            ms = jnp.mean(out * out, axis=-1, keepdims=True)
            o_ref[(sb - 1) * ATT_SUB_ROWS:sb * ATT_SUB_ROWS, :] = (
                (out * lax.rsqrt(ms + NORM_EPS) * sub_ref[...]) * (1.0 - lam_init))
        prev = [jnp.max(pm, axis=-1, keepdims=True) for pm in part_max] if sb < n_sub else None


def _attn_call(qkv, d_lam, d_subln, lam_init, cfg, latent, cast=None, regroup=None):
    nlb = cfg.n_lat // ATT_TQ
    ctx_row0 = cfg.B * cfg.n_lat // cfg.n_ctx
    layer, cast_srcs = cast if cast is not None else (0, [])
    kern = functools.partial(_attn_kernel, has_lat=latent, lam_init=lam_init, n_cast=len(cast_srcs),
                             regroup=regroup is not None)
    n_keys = cfg.n_ctx + (cfg.n_lat if latent else 0)
    kc = pl.BlockSpec((cfg.n_ctx, HEAD_DIM), lambda b, h, i: (ctx_row0 + b, D_HEADS + h))
    vc = pl.BlockSpec((cfg.n_ctx, HEAD_DIM), lambda b, h, i: (ctx_row0 + b, 2 * D_HEADS + h))
    small = [pl.BlockSpec((4, 64), lambda b, h, i: (0, 0)), pl.BlockSpec((1, HEAD_DIM), lambda b, h, i: (0, 0))]
    if latent:
        tq = ATT_TQ
        grid = (cfg.B, D_HEADS, nlb)
        q_spec = pl.BlockSpec((tq, HEAD_DIM), lambda b, h, i: (b * nlb + i, h))
        kv = [kc, vc,
              pl.BlockSpec((cfg.n_lat, HEAD_DIM), lambda b, h, i: (b, D_HEADS + h)),
              pl.BlockSpec((cfg.n_lat, HEAD_DIM), lambda b, h, i: (b, 2 * D_HEADS + h))]
        out_spec = pl.BlockSpec((tq, HEAD_DIM), lambda b, h, i: (b * nlb + i, h))
        n_rows = cfg.TL
    else:
        tq = cfg.n_ctx
        grid = (cfg.B, D_HEADS, 1)
        q_spec = pl.BlockSpec((tq, HEAD_DIM), lambda b, h, i: (ctx_row0 + b, h))
        kv = [kc, vc]
        out_spec = pl.BlockSpec((tq, HEAD_DIM), lambda b, h, i: (b, h))
        n_rows = cfg.B * cfg.n_ctx
    args = [qkv] * (1 + len(kv)) + [d_lam, d_subln.reshape(1, HEAD_DIM)]
    n_steps = grid[0] * grid[1] * grid[2]
    step = lambda b, h, i: (b * grid[1] + h) * grid[2] + i
    cast_in, cast_out, cast_shapes = [], [], []
    for w in cast_srcs:
        _, r, c = w.shape
        per = 1
        while (r * per) % n_steps or (r * per // n_steps) % 16:
            per *= 2
        slab = r * per // n_steps
        cast_in.append(pl.BlockSpec((None, slab, c), lambda b, h, i, per=per: (layer, step(b, h, i) // per, 0)))
        cast_out.append(pl.BlockSpec((slab, c), lambda b, h, i, per=per: (step(b, h, i) // per, 0)))
        cast_shapes.append(jax.ShapeDtypeStruct((r, c), BF16))
    if regroup is not None:
        rg_layer, w_in = regroup
        _, r, c = w_in.shape
        slab = r // n_steps
        assert r % n_steps == 0 and slab % 16 == 0, (r, n_steps)
        cast_in.append(pl.BlockSpec((None, slab, c), lambda b, h, i: (rg_layer, step(b, h, i), 0)))
        for width in (P_COLS, QKV_COLS, HEAD_DIM):
            cast_out.append(pl.BlockSpec((slab, width), lambda b, h, i: (step(b, h, i), 0)))
            cast_shapes.append(jax.ShapeDtypeStruct((r, width), BF16))
        cast_srcs = list(cast_srcs) + [w_in]
    outs = pl.pallas_call(
        kern,
        grid=grid,
        in_specs=[q_spec] + kv + small + cast_in,
        out_specs=[out_spec] + cast_out,
        out_shape=[jax.ShapeDtypeStruct((n_rows, D_HEADS * HEAD_DIM), F32)] + cast_shapes,
        scratch_shapes=[pltpu.VMEM((2 * tq // ATT_SUB_ROWS, ATT_SUB_ROWS, n_keys), F32),
                        pltpu.VMEM((n_keys, 2 * HEAD_DIM), BF16)],
        compiler_params=_cparams(3),
        name="diff_attn_lat" if latent else "diff_attn_ctx",
    )(*args, *cast_srcs)
    return outs if cast_srcs else outs[0]


OUT_TILE = 256


def _group_rmsnorm(x, w, groups):
    parts = []
    for gi in range(groups):
        xs = x[:, gi * HEAD_DIM:(gi + 1) * HEAD_DIM]
        ms = jnp.mean(xs * xs, axis=-1, keepdims=True)
        parts.append(xs * lax.rsqrt(ms + NORM_EPS) * w[:, gi * HEAD_DIM:(gi + 1) * HEAD_DIM])
    return jnp.concatenate(parts, axis=-1)


def _out_kernel(*refs, n_src, n_lat_tiles, has_ctx):
    x_refs, refs = refs[:n_src], refs[n_src:]
    if has_ctx:
        hmf, hmb, hgf, hgb, hdl, hdc, mo, go, mn, gn, w_ref, nw_ref, gate_ref, o_ref = refs
        hd = jnp.where(pl.program_id(0) < n_lat_tiles, hdl[...], hdc[...])
    else:
        hmf, hmb, hgf, hgb, hdl, mo, go, mn, gn, w_ref, nw_ref, gate_ref, o_ref = refs
        hd = hdl[...]
    rows = lambda ref: ref[...].reshape(OUT_TILE, 512)
    ym = _group_rmsnorm(rows(hmf) + rows(hmb), mn[...], 4) * jax.nn.sigmoid(mo[...])
    yg = _group_rmsnorm(rows(hgf) + rows(hgb), gn[...], 4) * _silu(go[...])
    y = jnp.concatenate([ym.astype(BF16), yg.astype(BF16), hd.astype(BF16)], axis=-1)
    z = jnp.dot(y, w_ref[...], preferred_element_type=F32)
    ms = jnp.mean(z * z, axis=-1, keepdims=True)
    o_ref[...] = _token_tile(x_refs, n_lat_tiles) + gate_ref[0] * (z * lax.rsqrt(ms + NORM_EPS) * nw_ref[...])


def _out_call(x_parts, hmf, hmb, hgf, hgb, hd_lat, hd_ctx, p, m_norm, g_norm, w_out, layer, nw, mod, cfg, n_rows):
    d = cfg.D
    tm = OUT_TILE
    nl = cfg.TL // tm
    row = functools.partial(_mod_row, cfg=cfg, tm=tm)
    rt = lambda w, c: pl.BlockSpec((tm, w), lambda i: (i, c))
    const = lambda i: (0, 0)
    lt, ct = cfg.n_lat // tm, cfg.n_ctx // tm

    def scan_index(i):
        k = i - cfg.B * lt
        return (jnp.where(k < 0, ct + i % lt, k % ct), jnp.where(k < 0, i // lt, k // ct), 0, 0)

    scan_blk = pl.BlockSpec((tm // CHUNK, 1, CHUNK, 512), scan_index)
    hd_specs = [pl.BlockSpec((tm, 1024), lambda i: (jnp.minimum(i, nl - 1), 0))]
    hd_args = [hd_lat]
    if hd_ctx is not None:
        hd_specs.append(pl.BlockSpec((tm, 1024), lambda i: (jnp.maximum(i - nl, 0), 0)))
        hd_args.append(hd_ctx)
    return pl.pallas_call(
        functools.partial(_out_kernel, n_src=len(x_parts), n_lat_tiles=nl, has_ctx=hd_ctx is not None),
        grid=(n_rows // tm,),
        in_specs=_token_specs(x_parts, tm, cfg) + [scan_blk, scan_blk, scan_blk, scan_blk] + hd_specs + [
                  rt(512, P_BLK_M_O), rt(512, P_BLK_G_OUT),
                  pl.BlockSpec((1, 512), const), pl.BlockSpec((1, 512), const),
                  pl.BlockSpec((None, d, d), lambda i: (layer, 0, 0)), pl.BlockSpec((1, d), const),
                  pl.BlockSpec((1, 1, d), lambda i: (row(i) * 6 + 2, 0, 0))],
        out_specs=rt(d, 0),
        out_shape=jax.ShapeDtypeStruct((n_rows, d), F32),
        compiler_params=_cparams(1),
        name="out_proj",
    )(*x_parts, hmf, hmb, hgf, hgb, *hd_args, p, p, m_norm, g_norm, w_out, nw, mod)


def _ffn_kernel(x_ref, nw_ref, shift_ref, scale_ref, wg_ref, wu_ref, wd_ref, pw_ref, gate_ref, o_ref,
                h_scr, acc_scr):
    j = pl.program_id(1)

    @pl.when(j == 0)
    def _():
        h_scr[...] = _prenorm(x_ref[...], nw_ref[...], shift_ref[0], scale_ref[0]).astype(BF16)
        acc_scr[...] = jnp.zeros_like(acc_scr)

    h = h_scr[...]
    a = jnp.dot(h, wg_ref[...], preferred_element_type=F32)
    u = jnp.dot(h, wu_ref[...], preferred_element_type=F32)
    acc_scr[...] += jnp.dot((_silu(a) * u).astype(BF16), wd_ref[...], preferred_element_type=F32)

    @pl.when(j == pl.num_programs(1) - 1)
    def _():
        z = acc_scr[...]
        ms = jnp.mean(z * z, axis=-1, keepdims=True)
        o_ref[...] = x_ref[...] + gate_ref[0] * (z * lax.rsqrt(ms + NORM_EPS) * pw_ref[...])


FFN_LAT_TILE = 512
FFN_CTX_TILE = 512
FFN_HID_TILE = 512


def _ffn_call(x, nw_pre, nw_post, mod, wg, wu, wd, cfg, row0, n_rows, tm):
    d, f = wg.shape
    tf = FFN_HID_TILE
    t0 = row0 // tm
    row = lambda i: _mod_row(i + t0, cfg, tm)
    const = lambda i, j: (0, 0)
    modspec = lambda kk: pl.BlockSpec((1, 1, d), lambda i, j: (row(i) * 6 + kk, 0, 0))
    return pl.pallas_call(
        _ffn_kernel,
        grid=(n_rows // tm, f // tf),
        in_specs=[pl.BlockSpec((tm, d), lambda i, j: (i + t0, 0)),
                  pl.BlockSpec((1, d), const), modspec(3), modspec(4),
                  pl.BlockSpec((d, tf), lambda i, j: (0, j)),
                  pl.BlockSpec((d, tf), lambda i, j: (0, j)),
                  pl.BlockSpec((tf, d), lambda i, j: (j, 0)),
                  pl.BlockSpec((1, d), const), modspec(5)],
        out_specs=pl.BlockSpec((tm, d), lambda i, j: (i, 0)),
        out_shape=jax.ShapeDtypeStruct((n_rows, d), F32),
        scratch_shapes=[pltpu.VMEM((tm, d), BF16), pltpu.VMEM((tm, d), F32)],
        compiler_params=_cparams(2),
        name="ffn",
    )(x, nw_pre, mod, mod, wg, wu, wd, nw_post, mod)


_MIX = {}
_off = 0
for _name, _w in (("m_q", 512), ("m_k", 512), ("m_v", 512), ("m_o", 512), ("m_gates", 16),
                  ("g_q", 256), ("g_k", 256), ("g_v", 512), ("g_out", 512), ("g_lr", 32),
                  ("d_q", 1024), ("d_k", 1024), ("d_v", 1024)):
    _MIX[_name] = (_off, _w)
    _off += _w
_P_ORDER = ("m_q", "m_k", "m_v", "m_o", "g_q", "g_k", "g_v", "g_out")
_QKV_ORDER = ("d_q", "d_k", "d_v")


def _split_w_in(w_in):
    cols = lambda n: w_in[:, _MIX[n][0]:_MIX[n][0] + _MIX[n][1]]
    w_a = jnp.concatenate([cols(n) for n in _P_ORDER], axis=1).astype(BF16)
    w_b = jnp.concatenate([cols(n) for n in _QKV_ORDER], axis=1).astype(BF16)
    w_small = jnp.concatenate([cols("m_gates"), cols("g_lr"),
                               jnp.zeros((w_in.shape[0], HEAD_DIM - 48), w_in.dtype)], axis=1).astype(BF16)
    return w_a, w_b, w_small


def _rope_tables(cfg):
    rows = cfg.n_lat // GRID_W
    r = np.repeat(np.arange(rows, dtype=np.float64), GRID_W)
    c = np.tile(np.arange(GRID_W, dtype=np.float64), rows)
    half = 16
    inv_freq = ROPE_BASE ** (-np.arange(half, dtype=np.float64) / half)
    ang_r, ang_c = r[:, None] * inv_freq, c[:, None] * inv_freq
    ang = np.concatenate([ang_r, ang_r, ang_c, ang_c], axis=-1)
    ang = np.tile(ang, (cfg.B, 2))
    n_c = cfg.B * cfg.n_ctx
    cos_t = np.concatenate([np.cos(ang), np.ones((n_c, HEAD_DIM))], axis=0).astype(np.float32)
    sin_t = np.concatenate([np.sin(ang), np.zeros((n_c, HEAD_DIM))], axis=0).astype(np.float32)
    return jnp.asarray(cos_t), jnp.asarray(sin_t)


def _layer(x_parts, mod, lw, stacks, layer, lam_init, rope, cfg, need_ctx, w_in_parts):
    d = cfg.D
    mq, mk, p, qkv, ps = _in_call(x_parts, lw["norm_mix_pre"].reshape(1, d), mod, *w_in_parts, *rope,
                                  lw["mlstm_conv_w"], lw["mlstm_conv_b"], cfg)
    w2 = lw["gla_gate_w2"]
    w2p = jnp.zeros((HEAD_DIM, 2 * G_QK), F32)
    w2p = w2p.at[16:32, 0:G_QK].set(w2[0]).at[32:48, G_QK:].set(w2[1])
    hmf, hmb, hgf, hgb = _scan_call(mq, mk, p, ps, lw["mlstm_gate_b"], w2p,
                                    lw["gla_gate_b"].reshape(1, 2 * G_QK), cfg)

    nxt = layer + 1 if need_ctx else None
    hd_lat, wg, wu, wd, *next_w_in = _attn_call(
        qkv, lw["diff_lambda"], lw["diff_subln"], lam_init, cfg, latent=True,
        cast=(layer, [stacks["w_ffn_gate"], stacks["w_ffn_up"], stacks["w_ffn_down"]]),
        regroup=None if nxt is None else (nxt, stacks["w_in"]))
    hd_ctx = _attn_call(qkv, lw["diff_lambda"], lw["diff_subln"], lam_init, cfg, latent=False) if need_ctx else None

    n_rows = cfg.T if need_ctx else cfg.TL
    xt = _out_call(x_parts, hmf, hmb, hgf, hgb, hd_lat, hd_ctx, p, lw["mlstm_norm"].reshape(1, 512),
                   lw["gla_norm"].reshape(1, 512), stacks["w_out"], layer,
                   lw["norm_mix_post"].reshape(1, d), mod, cfg, n_rows)
    ffn = functools.partial(_ffn_call, xt, lw["norm_ffn_pre"].reshape(1, d), lw["norm_ffn_post"].reshape(1, d),
                            mod, wg, wu, wd, cfg)
    out = (ffn(0, cfg.TL, FFN_LAT_TILE),)
    if need_ctx:
        out += (ffn(cfg.TL, cfg.T - cfg.TL, FFN_CTX_TILE),)
    return out, (tuple(next_w_in) if next_w_in else None)


_LAYER_KEYS = ("norm_mix_pre", "norm_mix_post", "norm_ffn_pre", "norm_ffn_post", "w_in", "mlstm_conv_w",
               "mlstm_conv_b", "mlstm_gate_b", "mlstm_norm", "gla_gate_w2", "gla_gate_b", "gla_norm",
               "diff_lambda", "diff_subln", "w_out", "w_ffn_gate", "w_ffn_up", "w_ffn_down")
_STACK_KEYS = ("w_in", "w_out", "w_ffn_gate", "w_ffn_up", "w_ffn_down")


def kernel(x, c, ctx, c_ctx, w_mod, b_mod, norm_mix_pre, norm_mix_post, norm_ffn_pre, norm_ffn_post, w_in, mlstm_conv_w, mlstm_conv_b, mlstm_gate_b, mlstm_norm, gla_gate_w2, gla_gate_b, gla_norm, diff_lambda, diff_subln, w_out, w_ffn_gate, w_ffn_up, w_ffn_down):
    weights = dict(zip(_LAYER_KEYS, (norm_mix_pre, norm_mix_post, norm_ffn_pre, norm_ffn_post, w_in,
                                     mlstm_conv_w, mlstm_conv_b, mlstm_gate_b, mlstm_norm, gla_gate_w2,
                                     gla_gate_b, gla_norm, diff_lambda, diff_subln, w_out, w_ffn_gate,
                                     w_ffn_up, w_ffn_down)))
    b, n_lat, d = x.shape
    cfg = Cfg(B=b, n_ctx=ctx.shape[1], n_lat=n_lat, D=d, F=w_ffn_gate.shape[-1])
    depth = w_mod.shape[0]
    c8 = jnp.zeros((8, d), F32).at[0].set(c_ctx).at[1:1 + b].set(c)
    mods, w_in_parts = _mod_call(c8, w_mod, b_mod, w_in)
    mods = mods.reshape(depth, 8 * 6, 1, d)
    rope = _rope_tables(cfg)
    x_parts = (x.reshape(b * n_lat, d), ctx.reshape(b * ctx.shape[1], d))
    stacks = {k: weights.pop(k) for k in _STACK_KEYS}
    stacks["w_out"] = stacks["w_out"].astype(BF16)
    for layer in range(depth):
        lw = {k: v[layer] for k, v in weights.items()}
        lam_init = 0.8 - 0.6 * math.exp(-0.3 * layer)
        x_parts, w_in_parts = _layer(x_parts, mods[layer], lw, stacks, layer, lam_init, rope, cfg,
                                     layer < depth - 1, w_in_parts)
    return x_parts[0].reshape(b, n_lat, d)
```
